```python
import math
import jax, jax.numpy as jnp
from jax import lax
import numpy as np

D_MODEL = 1024
BATCH = 4
SEQ = 8192
DEPTH = 2

D_MIX = D_MODEL
GROUP_W = D_MIX // 4
CONV_K = 31
NSA_HEADS = 4
NSA_HD = GROUP_W // NSA_HEADS
CMP_LEN = 32
CMP_STRIDE = 16
CMP_HIDDEN = 2 * NSA_HD
SLC_LEN = 64
N_SELECT = 16
WINDOW = 512
MLA_HEADS = 4
MLA_NOPE = 64
MLA_ROPE = 32
MLA_V = GROUP_W // MLA_HEADS
Q_LORA = 256
KV_LORA = 128
ROPE_THETA = 10000.0
POOL_WINDOWS = (2, 4, 8, 16)
POOL_GW = GROUP_W // len(POOL_WINDOWS)
D_FF = 4 * D_MODEL
N_BUCKETS = 32
MAX_DIST = 128
Q_BLOCK = 128
EPS = 1e-6
NEG_INF = -1e30
FORCE_SCORE = 1e9
IN_SPLITS = (2 * GROUP_W, NSA_HEADS * NSA_HD, 6 * NSA_HD, 3 * NSA_HEADS, Q_LORA, KV_LORA, MLA_ROPE, GROUP_W)
D_IN = sum(IN_SPLITS)
SPLIT_AT = tuple(sum(IN_SPLITS[:i + 1]) for i in range(len(IN_SPLITS) - 1))

kernel_name = 'hybrid_parallel_mixers'


def rmsnorm(x, g):
    xf = x.astype(jnp.float32)
    y = xf * lax.rsqrt(jnp.mean(xf * xf, axis=-1, keepdims=True) + EPS)
    return (y * g.astype(jnp.float32)).astype(x.dtype)


def layernorm(x, g, b):
    xf = x.astype(jnp.float32)
    mu = jnp.mean(xf, axis=-1, keepdims=True)
    var = jnp.mean(jnp.square(xf - mu), axis=-1, keepdims=True)
    y = (xf - mu) * lax.rsqrt(var + EPS) * g.astype(jnp.float32) + b.astype(jnp.float32)
    return y.astype(x.dtype)


def masked_softmax(logits, mask):
    lf = jnp.where(mask, logits.astype(jnp.float32), NEG_INF)
    p = jax.nn.softmax(lf, axis=-1)
    return jnp.where(mask, p, 0.0)


def rel_bucket(dist):
    n = jnp.maximum(dist, 0)
    max_exact = N_BUCKETS // 2
    nf = jnp.maximum(n, 1).astype(jnp.float32)
    large = max_exact + (jnp.log(nf / max_exact) / math.log(MAX_DIST / max_exact)
                         * (N_BUCKETS - max_exact)).astype(jnp.int32)
    large = jnp.minimum(large, N_BUCKETS - 1)
    return jnp.where(n < max_exact, n, large)


def rel_bias(dist, table):
    return table[rel_bucket(dist)]


def apply_rope(x, cos, sin):
    x1, x2 = jnp.split(x, 2, axis=-1)
    return jnp.concatenate([x1 * cos - x2 * sin, x1 * sin + x2 * cos], axis=-1).astype(x.dtype)


def conv_module(u, dw, dw_b, ln_g, ln_b, pw):
    a, g = jnp.split(u, 2, axis=-1)
    h = a * jax.nn.sigmoid(g)
    h = lax.conv_general_dilated(h, dw[:, None, :], (1,), [(CONV_K - 1, 0)],
                                 dimension_numbers=('NWC', 'WIO', 'NWC'),
                                 feature_group_count=GROUP_W) + dw_b
    h = jax.nn.silu(layernorm(h, ln_g, ln_b))
    return h @ pw


def pool_mixer(u, w_grp, scale):
    B_, S_, _ = u.shape
    uf = u.astype(jnp.float32)
    c = jnp.pad(jnp.cumsum(uf, axis=1), ((0, 0), (1, 0), (0, 0)))
    t = jnp.arange(S_)
    outs = []
    for gi, w in enumerate(POOL_WINDOWS):
        cg = c[..., gi * POOL_GW:(gi + 1) * POOL_GW]
        lo = jnp.maximum(t + 1 - w, 0)
        cnt = (t + 1 - lo).astype(jnp.float32)
        mean = (cg[:, 1:] - jnp.take(cg, lo, axis=1)) / cnt[None, :, None]
        outs.append(mean - uf[..., gi * POOL_GW:(gi + 1) * POOL_GW])
    d = jnp.stack(outs, axis=2).astype(u.dtype)
    y = jnp.einsum('bsgc,gcd->bsgd', d, w_grp).reshape(B_, S_, GROUP_W)
    return y * scale


def nsa_mixer(q, kv, gates, cmp_pos, cmp_w1, cmp_w2, rel_table):
    B_, S_ = q.shape[0], q.shape[1]
    k_c, v_c, k_s, v_s, k_w, v_w = jnp.split(kv, 6, axis=-1)
    n_cmp = (S_ - CMP_LEN) // CMP_STRIDE + 1
    blk_idx = jnp.arange(n_cmp)[:, None] * CMP_STRIDE + jnp.arange(CMP_LEN)[None, :]

    def compress(k, i):
        kb = k[:, blk_idx] + cmp_pos[i]
        hdn = jax.nn.silu(kb.reshape(B_, n_cmp, CMP_LEN * NSA_HD) @ cmp_w1[i])
        return hdn @ cmp_w2[i]

    Kc = compress(k_c, 0)
    Vc = compress(v_c, 1)
    cmp_end = jnp.arange(n_cmp) * CMP_STRIDE + CMP_LEN - 1
    n_slc = S_ // SLC_LEN
    c_lo = jnp.arange(n_cmp)[:, None] * CMP_STRIDE
    s_lo = jnp.arange(n_slc)[None, :] * SLC_LEN
    overlap = jnp.clip(jnp.minimum(c_lo + CMP_LEN, s_lo + SLC_LEN) - jnp.maximum(c_lo, s_lo),
                       0, None).astype(jnp.float32) / CMP_LEN
    n_top = min(N_SELECT, n_slc)
    k_w_pad = jnp.pad(k_w, ((0, 0), (WINDOW, 0), (0, 0)))
    v_w_pad = jnp.pad(v_w, ((0, 0), (WINDOW, 0), (0, 0)))
    n_qb = S_ // Q_BLOCK
    q_blocks = (q * NSA_HD ** -0.5).reshape(B_, n_qb, Q_BLOCK, NSA_HEADS, NSA_HD).transpose(1, 0, 3, 2, 4)
    g_blocks = gates.reshape(B_, n_qb, Q_BLOCK, NSA_HEADS, 3).transpose(1, 0, 3, 2, 4)
    j_slc = jnp.arange(n_slc)
    off = jnp.arange(SLC_LEN)
    band = jnp.arange(WINDOW + Q_BLOCK)
    gather_rows = jax.vmap(lambda a, i: a[i])

    def block(args):
        qb, qblk, gblk = args
        t = qb * Q_BLOCK + jnp.arange(Q_BLOCK)
        lc = jnp.einsum('bhqd,bcd->bhqc', qblk, Kc) + jnp.moveaxis(
            rel_bias(t[:, None] - cmp_end[None, :], rel_table), -1, 0)
        pc = masked_softmax(lc, cmp_end[None, :] <= t[:, None])
        o_c = jnp.einsum('bhqc,bcd->bhqd', pc.astype(Vc.dtype), Vc)
        imp = jnp.einsum('bhqc,cj->bqj', pc, overlap)
        jc = (t // SLC_LEN)[:, None]
        forced = (j_slc[None, :] == 0) | (j_slc[None, :] == jc) | (j_slc[None, :] == jc - 1)
        imp = jnp.where(forced, FORCE_SCORE, imp)
        imp = jnp.where(j_slc[None, :] <= jc, imp, NEG_INF)
        top_v, top_i = lax.top_k(imp, n_top)
        blk_ok = top_v > 0.5 * NEG_INF
        pos = (top_i[..., None] * SLC_LEN + off).reshape(B_, Q_BLOCK, n_top * SLC_LEN)
        pos_flat = pos.reshape(B_, Q_BLOCK * n_top * SLC_LEN)
        Ks = gather_rows(k_s, pos_flat).reshape(B_, Q_BLOCK, n_top * SLC_LEN, NSA_HD)
        Vs = gather_rows(v_s, pos_flat).reshape(B_, Q_BLOCK, n_top * SLC_LEN, NSA_HD)
        ms = (pos <= t[None, :, None]) & jnp.repeat(blk_ok, SLC_LEN, axis=-1)
        ls = jnp.einsum('bhqd,bqkd->bhqk', qblk, Ks) + jnp.moveaxis(
            rel_bias(t[None, :, None] - pos, rel_table), -1, 1)
        ps = masked_softmax(ls, ms[:, None])
        o_s = jnp.einsum('bhqk,bqkd->bhqd', ps.astype(Vs.dtype), Vs)
        start = qb * Q_BLOCK
        Kw = lax.dynamic_slice_in_dim(k_w_pad, start, WINDOW + Q_BLOCK, axis=1)
        Vw = lax.dynamic_slice_in_dim(v_w_pad, start, WINDOW + Q_BLOCK, axis=1)
        s_pos = start - WINDOW + band
        dist = t[:, None] - s_pos[None, :]
        mw = (dist >= 0) & (dist < WINDOW) & (s_pos[None, :] >= 0)
        lw = jnp.einsum('bhqd,bkd->bhqk', qblk, Kw) + jnp.moveaxis(rel_bias(dist, rel_table), -1, 0)
        pw = masked_softmax(lw, mw)
        o_w = jnp.einsum('bhqk,bkd->bhqd', pw.astype(Vw.dtype), Vw)
        return gblk[..., 0:1] * o_c + gblk[..., 1:2] * o_s + gblk[..., 2:3] * o_w

    out = lax.map(block, (jnp.arange(n_qb), q_blocks, g_blocks))
    return out.transpose(1, 0, 3, 2, 4).reshape(B_, S_, NSA_HEADS * NSA_HD)


def mla_mixer(c_q, c_kv, k_r, q_norm_g, w_uq, kv_norm_g, w_ukv, cos, sin):
    B_, S_ = c_q.shape[0], c_q.shape[1]
    q = (rmsnorm(c_q, q_norm_g) @ w_uq).reshape(B_, S_, MLA_HEADS, MLA_NOPE + MLA_ROPE)
    q_nope, q_rope = jnp.split(q, [MLA_NOPE], axis=-1)
    q_rope = apply_rope(q_rope, cos[:, None, :], sin[:, None, :])
    kv = (rmsnorm(c_kv, kv_norm_g) @ w_ukv).reshape(B_, S_, MLA_HEADS, MLA_NOPE + MLA_V)
    k_nope, v = jnp.split(kv, [MLA_NOPE], axis=-1)
    k_rope = apply_rope(k_r, cos, sin)
    scale = (MLA_NOPE + MLA_ROPE) ** -0.5
    n_qb = S_ // Q_BLOCK
    qn = (q_nope * scale).reshape(B_, n_qb, Q_BLOCK, MLA_HEADS, MLA_NOPE).transpose(1, 0, 2, 3, 4)
    qr = (q_rope * scale).reshape(B_, n_qb, Q_BLOCK, MLA_HEADS, MLA_ROPE).transpose(1, 0, 2, 3, 4)
    kpos = jnp.arange(S_)

    def block(args):
        qb, qn_b, qr_b = args
        t = qb * Q_BLOCK + jnp.arange(Q_BLOCK)
        lg = (jnp.einsum('bqhd,bkhd->bhqk', qn_b, k_nope)
              + jnp.einsum('bqhr,bkr->bhqk', qr_b, k_rope))
        p = masked_softmax(lg, kpos[None, :] <= t[:, None])
        return jnp.einsum('bhqk,bkhd->bqhd', p.astype(v.dtype), v)

    out = lax.map(block, (jnp.arange(n_qb), qn, qr))
    return out.transpose(1, 0, 2, 3, 4).reshape(B_, S_, MLA_HEADS * MLA_V)


def setup_inputs(seed: int = 0) -> dict:
    key = jax.random.key(seed)
    ks = jax.random.split(key, 23)
    f32 = jnp.float32

    def nrm(k, shape, scale):
        return jax.random.normal(k, shape, f32) * scale

    def gain(k, shape):
        return 1.0 + 0.02 * jax.random.normal(k, shape, f32)

    return {
        'x': nrm(ks[0], (BATCH, SEQ, D_MODEL), 1.0),
        'w_in': nrm(ks[1], (DEPTH, D_MODEL, D_IN), D_MODEL ** -0.5),
        'w_out': nrm(ks[2], (DEPTH, D_MIX, D_MODEL), D_MIX ** -0.5),
        'ln_mix_g': gain(ks[3], (DEPTH, D_MODEL)),
        'ln_mlp_g': gain(ks[4], (DEPTH, D_MODEL)),
        'conv_dw': nrm(ks[5], (DEPTH, CONV_K, GROUP_W), CONV_K ** -0.5),
        'conv_dw_b': nrm(ks[6], (DEPTH, GROUP_W), 0.02),
        'conv_ln_g': gain(ks[7], (DEPTH, GROUP_W)),
        'conv_ln_b': nrm(ks[8], (DEPTH, GROUP_W), 0.02),
        'conv_pw': nrm(ks[9], (DEPTH, GROUP_W, GROUP_W), GROUP_W ** -0.5),
        'nsa_cmp_pos': nrm(ks[10], (DEPTH, 2, CMP_LEN, NSA_HD), 0.1),
        'nsa_cmp_w1': nrm(ks[11], (DEPTH, 2, CMP_LEN * NSA_HD, CMP_HIDDEN), (CMP_LEN * NSA_HD) ** -0.5),
        'nsa_cmp_w2': nrm(ks[12], (DEPTH, 2, CMP_HIDDEN, NSA_HD), CMP_HIDDEN ** -0.5),
        'mla_q_norm_g': gain(ks[13], (DEPTH, Q_LORA)),
        'mla_w_uq': nrm(ks[14], (DEPTH, Q_LORA, MLA_HEADS * (MLA_NOPE + MLA_ROPE)), Q_LORA ** -0.5),
        'mla_kv_norm_g': gain(ks[15], (DEPTH, KV_LORA)),
        'mla_w_ukv': nrm(ks[16], (DEPTH, KV_LORA, MLA_HEADS * (MLA_NOPE + MLA_V)), KV_LORA ** -0.5),
        'pool_w': nrm(ks[17], (DEPTH, len(POOL_WINDOWS), POOL_GW, POOL_GW), POOL_GW ** -0.5),
        'pool_scale': gain(ks[18], (DEPTH, GROUP_W)),
        'mlp_w1': nrm(ks[19], (DEPTH, D_MODEL, D_FF), D_MODEL ** -0.5),
        'mlp_w2': nrm(ks[20], (DEPTH, D_FF, D_MODEL), D_FF ** -0.5),
        'rel_bias_table': nrm(ks[21], (N_BUCKETS, NSA_HEADS), 0.2),
        'final_norm_g': gain(ks[22], (D_MODEL,)),
    }


def reference(x, w_in, w_out, ln_mix_g, ln_mlp_g, conv_dw, conv_dw_b, conv_ln_g, conv_ln_b,
              conv_pw, nsa_cmp_pos, nsa_cmp_w1, nsa_cmp_w2, mla_q_norm_g, mla_w_uq,
              mla_kv_norm_g, mla_w_ukv, pool_w, pool_scale, mlp_w1, mlp_w2,
              rel_bias_table, final_norm_g):
    B_, S_ = x.shape[0], x.shape[1]
    pos = jnp.arange(S_, dtype=jnp.float32)
    inv_freq = ROPE_THETA ** (-jnp.arange(0, MLA_ROPE, 2, dtype=jnp.float32) / MLA_ROPE)
    ang = pos[:, None] * inv_freq[None, :]
    cos, sin = jnp.cos(ang), jnp.sin(ang)
    h = x
    for l in range(DEPTH):
        u = rmsnorm(h, ln_mix_g[l]) @ w_in[l]
        u_conv, u_q, u_kv, u_g, u_cq, u_ckv, u_kr, u_pool = jnp.split(u, SPLIT_AT, axis=-1)
        y_conv = conv_module(u_conv, conv_dw[l], conv_dw_b[l], conv_ln_g[l], conv_ln_b[l], conv_pw[l])
        y_nsa = nsa_mixer(u_q.reshape(B_, S_, NSA_HEADS, NSA_HD), u_kv,
                          jax.nn.sigmoid(u_g).reshape(B_, S_, NSA_HEADS, 3),
                          nsa_cmp_pos[l], nsa_cmp_w1[l], nsa_cmp_w2[l], rel_bias_table)
        y_mla = mla_mixer(u_cq, u_ckv, u_kr, mla_q_norm_g[l], mla_w_uq[l],
                          mla_kv_norm_g[l], mla_w_ukv[l], cos, sin)
        y_pool = pool_mixer(u_pool, pool_w[l], pool_scale[l])
        h = h + jnp.concatenate([y_conv, y_nsa, y_mla, y_pool], axis=-1) @ w_out[l]
        z = rmsnorm(h, ln_mlp_g[l]) @ mlp_w1[l]
        h = h + jnp.square(jax.nn.relu(z)) @ mlp_w2[l]
    return rmsnorm(h, final_norm_g)
```

```python
import functools
import math

import numpy as np
import jax
import jax.numpy as jnp
from jax import lax
from jax.experimental import pallas as pl
from jax.experimental.pallas import tpu as pltpu

F32 = jnp.float32
BF16 = jnp.bfloat16

D_MODEL = 1024
GROUP_W = 256
CONV_K = 31
NSA_HEADS = 4
NSA_HD = 64
CMP_LEN = 32
CMP_STRIDE = 16
SLC_LEN = 64
N_SELECT = 16
WINDOW = 512
MLA_HEADS = 4
MLA_NOPE = 64
MLA_ROPE = 32
MLA_V = 64
Q_LORA = 256
KV_LORA = 128
ROPE_THETA = 10000.0
POOL_WINDOWS = (2, 4, 8, 16)
D_FF = 4 * D_MODEL
N_BUCKETS = 32
MAX_DIST = 128
EPS = 1e-6
NEG_INF = -1e30
FORCE_SCORE = 1e9

LANES = 128
QB = 128
HALO = 32
VMEM_LIMIT = 56 * 1024 * 1024

_C_CONV, _C_Q, _C_KV, _C_G, _C_CQ, _C_CKV, _C_KR, _C_POOL = 0, 512, 768, 1152, 1164, 1420, 1548, 1580


def _cparams(sem):
    return pltpu.CompilerParams(dimension_semantics=sem, vmem_limit_bytes=VMEM_LIMIT)


def _rms(x, g):
    ms = jnp.mean(x * x, axis=-1, keepdims=True)
    return x * lax.rsqrt(ms + EPS) * g


def _dot(a, b):
    return jnp.dot(a, b, preferred_element_type=F32)


def _dot_nt(a, b):
    return lax.dot_general(a, b, (((1,), (1,)), ((), ())), preferred_element_type=F32)


def _const_spec(shape):
    nd = len(shape)
    return pl.BlockSpec(shape, lambda *_: (0,) * nd)


def _proj_kernel(x_ref, g_ref, wrow_ref, wt_ref, conv_ref, pool_ref, kcvc_ref, ksw_ref, mla_ref,
                 qt_ref, vswt_ref, gt_ref, *, tile):
    xn = _rms(x_ref[...], g_ref[...]).astype(BF16)
    conv_ref[...] = _dot(xn, wrow_ref[:, 0:512])
    pool_ref[...] = _dot(xn, wrow_ref[:, 512:768])
    kcvc_ref[...] = _dot(xn, wrow_ref[:, 768:896])
    ksw_ref[...] = _dot(xn, wrow_ref[:, 896:1024]).astype(BF16)
    mla_ref[...] = _dot(xn, wrow_ref[:, 1024:1664])
    qt_ref[...] = _dot_nt(wt_ref[0:256, :], xn).astype(BF16)
    vs = _dot_nt(wt_ref[256:384, :], xn).astype(BF16)
    for j in range(tile // LANES):
        vswt_ref[j] = vs[:, j * LANES:(j + 1) * LANES]
    gt_ref[...] = _dot_nt(wt_ref[384:400, :], xn)


def _proj(h, g, wrow, wt, *, tile=512):
    n = h.shape[0]
    grid = (n // tile,)
    outs = (
        jax.ShapeDtypeStruct((n, 512), F32),
        jax.ShapeDtypeStruct((n, 256), F32),
        jax.ShapeDtypeStruct((n, 128), F32),
        jax.ShapeDtypeStruct((n, 128), BF16),
        jax.ShapeDtypeStruct((n, 640), F32),
        jax.ShapeDtypeStruct((256, n), BF16),
        jax.ShapeDtypeStruct((n // LANES, 128, LANES), BF16),
        jax.ShapeDtypeStruct((16, n), F32),
    )
    row = lambda w: pl.BlockSpec((tile, w), lambda i: (i, 0))
    return pl.pallas_call(
        functools.partial(_proj_kernel, tile=tile),
        grid=grid,
        in_specs=[row(D_MODEL), _const_spec((1, D_MODEL)), _const_spec(wrow.shape), _const_spec(wt.shape)],
        out_specs=(row(512), row(256), row(128), row(128), row(640),
                   pl.BlockSpec((256, tile), lambda i: (0, i)),
                   pl.BlockSpec((tile // LANES, 128, LANES), lambda i: (i, 0, 0)),
                   pl.BlockSpec((16, tile), lambda i: (0, i))),
        out_shape=outs,
        compiler_params=_cparams(("parallel",)),
        name="in_proj",
    )(h, g, wrow, wt)


def _local_kernel(uc_ref, uch_ref, up_ref, uph_ref, dw_ref, dwb_ref, lng_ref, lnb_ref, pw_ref,
                  poolw_ref, pscale_ref, yc_ref, yp_ref, hbuf, ubuf, sa, sb, *, tile, rows):
    t = pl.program_id(1)
    keep = jnp.where(t > 0, 1.0, 0.0).astype(F32)

    def glu(u):
        return u[:, 0:GROUP_W] * jax.nn.sigmoid(u[:, GROUP_W:2 * GROUP_W])

    hbuf[0:HALO, :] = glu(uch_ref[...]) * keep
    hbuf[HALO:HALO + tile, :] = glu(uc_ref[...])
    dwb = dwb_ref[...]
    for c in range(tile // rows):
        acc = jnp.zeros((rows, GROUP_W), F32) + dwb
        for k in range(CONV_K):
            acc = acc + dw_ref[k:k + 1, :] * hbuf[pl.ds(c * rows + HALO - (CONV_K - 1) + k, rows), :]
        mu = jnp.mean(acc, axis=-1, keepdims=True)
        cen = acc - mu
        var = jnp.mean(cen * cen, axis=-1, keepdims=True)
        y = cen * lax.rsqrt(var + EPS) * lng_ref[...] + lnb_ref[...]
        y = y * jax.nn.sigmoid(y)
        yc_ref[c * rows:(c + 1) * rows, :] = _dot(y.astype(BF16), pw_ref[...]).astype(BF16)

    ext = HALO + tile
    ubuf[0:HALO, :] = uph_ref[...] * keep
    ubuf[HALO:ext, :] = up_ref[...]
    sa[pl.ds(8, ext - 8), :] = ubuf[pl.ds(8, ext - 8), :] + ubuf[pl.ds(7, ext - 8), :]
    sb[pl.ds(16, ext - 16), :] = sa[pl.ds(16, ext - 16), :] + sa[pl.ds(14, ext - 16), :]
    s2 = sa[pl.ds(HALO, tile), :]
    s4 = sb[pl.ds(HALO, tile), :]
    sa[pl.ds(24, ext - 24), :] = sb[pl.ds(24, ext - 24), :] + sb[pl.ds(20, ext - 24), :]
    s8 = sa[pl.ds(HALO, tile), :]
    s16 = s8 + sa[pl.ds(HALO - 8, tile), :]
    lane = lax.broadcasted_iota(jnp.int32, (tile, GROUP_W), 1)
    grp = lane // (GROUP_W // len(POOL_WINDOWS))
    wsum = jnp.where(grp == 0, s2, jnp.where(grp == 1, s4, jnp.where(grp == 2, s8, s16)))
    width = jnp.where(grp == 0, 2, jnp.where(grp == 1, 4, jnp.where(grp == 2, 8, 16)))
    pos = t * tile + lax.broadcasted_iota(jnp.int32, (tile, GROUP_W), 0)
    cnt = jnp.minimum(width, pos + 1).astype(F32)
    d = wsum / cnt - up_ref[...]
    yp_ref[...] = (_dot(d.astype(BF16), poolw_ref[...]) * pscale_ref[...]).astype(BF16)


def _local(uc, up, dw, dwb, lng, lnb, pw, poolw, pscale, *, batch, seq, tile=512, rows=128):
    uc3 = uc.reshape(batch, seq, 512)
    up3 = up.reshape(batch, seq, GROUP_W)
    hpt = tile // HALO
    cur = lambda w: pl.BlockSpec((None, tile, w), lambda b, t: (b, t, 0))
    halo = lambda w: pl.BlockSpec((None, HALO, w), lambda b, t: (b, jnp.maximum(t * hpt - 1, 0), 0))
    yc, yp = pl.pallas_call(
        functools.partial(_local_kernel, tile=tile, rows=rows),
        grid=(batch, seq // tile),
        in_specs=[cur(512), halo(512), cur(GROUP_W), halo(GROUP_W),
                  _const_spec(dw.shape), _const_spec((1, GROUP_W)), _const_spec((1, GROUP_W)),
                  _const_spec((1, GROUP_W)), _const_spec((GROUP_W, GROUP_W)),
                  _const_spec((GROUP_W, GROUP_W)), _const_spec((1, GROUP_W))],
        out_specs=(cur(GROUP_W), cur(GROUP_W)),
        out_shape=(jax.ShapeDtypeStruct((batch, seq, GROUP_W), BF16),
                   jax.ShapeDtypeStruct((batch, seq, GROUP_W), BF16)),
        scratch_shapes=[pltpu.VMEM((HALO + tile, GROUP_W), F32)] * 4,
        compiler_params=_cparams(("parallel", "parallel")),
        name="local_mixers",
    )(uc3, uc3, up3, up3, dw, dwb, lng, lnb, pw, poolw, pscale)
    return yc.reshape(batch * seq, GROUP_W), yp.reshape(batch * seq, GROUP_W)


def _compress_kernel(kr_ref, posa_ref, posb_ref, w1a_ref, w1b_ref, w2_ref, kv_ref, kvt_ref, bbuf, *, nc):
    kr = kr_ref[...]
    a = _dot((kr + posa_ref[...]).astype(BF16), w1a_ref[...])
    b = _dot((kr + posb_ref[...]).astype(BF16), w1b_ref[...])
    bbuf[0:nc, :] = b
    bbuf[nc:nc + 8, :] = jnp.zeros((8, 256), F32)
    hdn = a + bbuf[pl.ds(1, nc), :]
    hdn = hdn * jax.nn.sigmoid(hdn)
    kv = _dot(hdn.astype(BF16), w2_ref[...])
    kv_ref[...] = kv.astype(BF16)
    kvt_ref[...] = kv.T.astype(BF16)


def _compress(kcvc, posa, posb, w1a, w1b, w2bd, *, batch, seq):
    nc = seq // CMP_STRIDE
    kr = kcvc.reshape(batch, nc, CMP_STRIDE * 128)
    return pl.pallas_call(
        functools.partial(_compress_kernel, nc=nc),
        grid=(batch,),
        in_specs=[pl.BlockSpec((None, nc, CMP_STRIDE * 128), lambda b: (b, 0, 0)),
                  _const_spec(posa.shape), _const_spec(posb.shape), _const_spec(w1a.shape),
                  _const_spec(w1b.shape), _const_spec(w2bd.shape)],
        out_specs=(pl.BlockSpec((None, nc, 128), lambda b: (b, 0, 0)),
                   pl.BlockSpec((None, 128, nc), lambda b: (b, 0, 0))),
        out_shape=(jax.ShapeDtypeStruct((batch, nc, 128), BF16),
                   jax.ShapeDtypeStruct((batch, 128, nc), BF16)),
        scratch_shapes=[pltpu.VMEM((nc + 8, 256), F32)],
        compiler_params=_cparams(("parallel",)),
        name="nsa_compress",
    )(kr, posa, posb, w1a, w1b, w2bd)


def _flash_step(s, valid, vt, m_ref, l_ref, acc_ref):
    if valid is not None:
        s = jnp.where(valid, s, NEG_INF)
    m_old = m_ref[...]
    m_new = jnp.maximum(m_old, jnp.max(s, axis=0, keepdims=True))
    p = jnp.exp(s - m_new)
    if valid is not None:
        p = jnp.where(valid, p, 0.0)
    alpha = jnp.exp(m_old - m_new)
    l_ref[...] = alpha * l_ref[...] + jnp.sum(p, axis=0, keepdims=True)
    acc_ref[...] = alpha * acc_ref[...] + _dot(vt, p.astype(BF16))
    m_ref[...] = m_new


def _nsa_kernel(kvc_ref, kvct_ref, ksw_ref, vswt_ref, qt_ref, gt_ref, ovl_ref, bd_ref, bd1_ref, bc_ref,
                y_ref, s_ref, imp_ref, sel_ref, m_ref, l_ref, acc_ref, oc_ref, os_ref, *, nc, ns, n_top):
    qb = pl.program_id(1)
    hq = NSA_HEADS * QB
    qt = jnp.concatenate([qt_ref[h * NSA_HD:(h + 1) * NSA_HD, :] for h in range(NSA_HEADS)], axis=1)
    zer = jnp.zeros_like(qt)
    qts = jnp.concatenate([qt, zer], axis=0)
    qtw = jnp.concatenate([zer, qt], axis=0)
    lane_q = lax.broadcasted_iota(jnp.int32, (1, hq), 1) % QB

    pad = 16
    s_ref[0:pad, :] = jnp.zeros((pad, hq), F32)
    s_ref[pad + nc:pad + nc + pad, :] = jnp.zeros((pad, hq), F32)
    s_ref[pad:pad + nc, :] = _dot(kvc_ref[...], qts)
    w0 = pl.multiple_of(qb * 8, 8)
    s_ref[pl.ds(w0, 32), :] = s_ref[pl.ds(w0, 32), :] + bc_ref[...]
    sc = s_ref[pad:pad + nc, :]
    cidx = lax.broadcasted_iota(jnp.int32, (nc, hq), 0)
    tq = qb * QB + lane_q
    valid_c = (cidx * CMP_STRIDE + (CMP_LEN - 1) <= tq) & (cidx < nc - 1)
    sc = jnp.where(valid_c, sc, NEG_INF)
    mc = jnp.max(sc, axis=0, keepdims=True)
    pc = jnp.where(valid_c, jnp.exp(sc - mc), 0.0)
    lc = jnp.sum(pc, axis=0, keepdims=True)
    pc = pc * jnp.where(lc > 0.0, 1.0 / lc, 0.0)
    oc_ref[...] = _dot(kvct_ref[NSA_HD:2 * NSA_HD, :], pc.astype(BF16))

    psum = pc[:, 0:QB]
    for h in range(1, NSA_HEADS):
        psum = psum + pc[:, h * QB:(h + 1) * QB]
    p_hi = psum.astype(BF16)
    p_lo = (psum - p_hi.astype(F32)).astype(BF16)
    imp = _dot(ovl_ref[...], p_hi) + _dot(ovl_ref[...], p_lo)
    jidx = lax.broadcasted_iota(jnp.int32, (ns, QB), 0)
    iq = lax.broadcasted_iota(jnp.int32, (ns, QB), 1)
    jc = 2 * qb + (iq >= SLC_LEN).astype(jnp.int32)
    forced = (jidx == 0) | (jidx == jc) | (jidx == jc - 1)
    imp = jnp.where(forced, FORCE_SCORE, imp)
    imp = jnp.where(jidx <= jc, imp, NEG_INF)
    imp_ref[...] = imp

    def rank_body(jp, cnt):
        row = imp_ref[pl.ds(jp, 1), :]
        beats = (row > imp) | ((row == imp) & (jidx > jp))
        return cnt + beats.astype(jnp.int32)

    cnt = lax.fori_loop(0, jnp.minimum(2 * qb + 2, ns), rank_body, jnp.zeros((ns, QB), jnp.int32))
    sel_ref[...] = jnp.where((cnt < n_top) & (imp > 0.5 * NEG_INF), 1.0, 0.0)

    kl = lax.broadcasted_iota(jnp.int32, (QB, hq), 0)
    causal = kl <= lane_q
    anti = kl > lane_q

    def init_state():
        m_ref[...] = jnp.full((1, hq), NEG_INF, F32)
        l_ref[...] = jnp.zeros((1, hq), F32)
        acc_ref[...] = jnp.zeros((NSA_HD, hq), F32)

    def sel_mask(kt):
        m0 = jnp.broadcast_to(sel_ref[pl.ds(2 * kt, 1), :], (SLC_LEN, QB))
        m1 = jnp.broadcast_to(sel_ref[pl.ds(2 * kt + 1, 1), :], (SLC_LEN, QB))
        mt = jnp.concatenate([m0, m1], axis=0)
        return jnp.concatenate([mt] * NSA_HEADS, axis=1) > 0.5

    def ktile(kt):
        return ksw_ref[pl.ds(pl.multiple_of(kt * QB, QB), QB), :]

    init_state()

    def far_body(kt, carry):
        _flash_step(_dot(ktile(kt), qts), sel_mask(kt), vswt_ref[kt, 0:NSA_HD, :], m_ref, l_ref, acc_ref)
        return carry

    lax.fori_loop(0, jnp.maximum(qb - 1, 0), far_body, 0)

    @pl.when(qb >= 1)
    def _():
        kt = qb - 1
        _flash_step(_dot(ktile(kt), qts) + bd1_ref[...], sel_mask(kt), vswt_ref[kt, 0:NSA_HD, :],
                    m_ref, l_ref, acc_ref)

    _flash_step(_dot(ktile(qb), qts) + bd_ref[...], sel_mask(qb) & causal, vswt_ref[qb, 0:NSA_HD, :],
                m_ref, l_ref, acc_ref)
    os_ref[...] = acc_ref[...] * (1.0 / l_ref[...])

    init_state()
    n_w = WINDOW // QB
    for back in range(n_w, 0, -1):
        @pl.when(qb >= back)
        def _(back=back):
            kt = qb - back
            s = _dot(ktile(kt), qtw)
            if back == 1:
                s = s + bd1_ref[...]
            _flash_step(s, anti if back == n_w else None, vswt_ref[kt, NSA_HD:2 * NSA_HD, :],
                        m_ref, l_ref, acc_ref)
    _flash_step(_dot(ktile(qb), qtw) + bd_ref[...], causal, vswt_ref[qb, NSA_HD:2 * NSA_HD, :],
                m_ref, l_ref, acc_ref)
    ow = acc_ref[...] * (1.0 / l_ref[...])

    gates = jax.nn.sigmoid(gt_ref[...])
    oc = oc_ref[...]
    osel = os_ref[...]
    parts = []
    for h in range(NSA_HEADS):
        sl = slice(h * QB, (h + 1) * QB)
        parts.append(gates[3 * h:3 * h + 1, :] * oc[:, sl] + gates[3 * h + 1:3 * h + 2, :] * osel[:, sl]
                     + gates[3 * h + 2:3 * h + 3, :] * ow[:, sl])
    y_ref[...] = jnp.concatenate(parts, axis=0).T.astype(BF16)


def _nsa(kvc, kvct, ksw, vswt, qt, gt, ovl_t, bd, bd1, bc, *, batch, seq):
    nc = seq // CMP_STRIDE
    ns = seq // SLC_LEN
    nqb = seq // QB
    hq = NSA_HEADS * QB
    ksw3 = ksw.reshape(batch, seq, 128)
    vswt4 = vswt.reshape(batch, nqb, 128, LANES)
    return pl.pallas_call(
        functools.partial(_nsa_kernel, nc=nc, ns=ns, n_top=min(N_SELECT, ns)),
        grid=(batch, nqb),
        in_specs=[pl.BlockSpec((None, nc, 128), lambda b, q: (b, 0, 0)),
                  pl.BlockSpec((None, 128, nc), lambda b, q: (b, 0, 0)),
                  pl.BlockSpec((None, seq, 128), lambda b, q: (b, 0, 0)),
                  pl.BlockSpec((None, nqb, 128, LANES), lambda b, q: (b, 0, 0, 0)),
                  pl.BlockSpec((256, QB), lambda b, q: (0, b * nqb + q)),
                  pl.BlockSpec((16, QB), lambda b, q: (0, b * nqb + q)),
                  _const_spec(ovl_t.shape), _const_spec(bd.shape), _const_spec(bd1.shape),
                  _const_spec(bc.shape)],
        out_specs=pl.BlockSpec((QB, 256), lambda b, q: (b * nqb + q, 0)),
        out_shape=jax.ShapeDtypeStruct((batch * seq, 256), BF16),
        scratch_shapes=[pltpu.VMEM((nc + 32, hq), F32),
                        pltpu.VMEM((ns, QB), F32),
                        pltpu.VMEM((ns, QB), F32),
                        pltpu.VMEM((1, hq), F32), pltpu.VMEM((1, hq), F32), pltpu.VMEM((NSA_HD, hq), F32),
                        pltpu.VMEM((NSA_HD, hq), F32), pltpu.VMEM((NSA_HD, hq), F32)],
        compiler_params=_cparams(("parallel", "parallel")),
        name="nsa_attention",
    )(kvc, kvct, ksw3, vswt4, qt, gt, ovl_t, bd, bd1, bc)


def _mla_prep_kernel(u_ref, qg_ref, kg_ref, wqt_ref, wqpt_ref, wk_ref, wvt_ref, cosp_ref, sinp_ref,
                     cost_ref, sint_ref, k_ref, qt_ref, vt_ref):
    scale = (MLA_NOPE + MLA_ROPE) ** -0.5
    cqn = _rms(u_ref[:, 0:Q_LORA], qg_ref[...]).astype(BF16)
    ckvn = _rms(u_ref[:, Q_LORA:Q_LORA + KV_LORA], kg_ref[...]).astype(BF16)
    qta = _dot_nt(wqt_ref[...], cqn)
    qtp = _dot_nt(wqpt_ref[...], cqn)
    cos_t = cost_ref[...]
    sin_t = sint_ref[...]
    for h in range(MLA_HEADS):
        r0 = h * 128
        qt_ref[r0:r0 + MLA_NOPE, :] = (qta[r0:r0 + MLA_NOPE, :] * scale).astype(BF16)
        rope = qta[r0 + MLA_NOPE:r0 + MLA_NOPE + MLA_ROPE, :] * cos_t + qtp[h * MLA_ROPE:(h + 1) * MLA_ROPE, :] * sin_t
        qt_ref[r0 + MLA_NOPE:r0 + MLA_NOPE + MLA_ROPE, :] = (rope * scale).astype(BF16)
        qt_ref[r0 + MLA_NOPE + MLA_ROPE:r0 + 128, :] = jnp.zeros((128 - MLA_NOPE - MLA_ROPE, qta.shape[1]), BF16)
    krope = u_ref[:, 384:512] * cosp_ref[...] + u_ref[:, 512:640] * sinp_ref[...]
    knope = _dot(ckvn, wk_ref[...])
    for h in range(MLA_HEADS):
        k_ref[:, h * 128:(h + 1) * 128] = (knope[:, h * 128:(h + 1) * 128] + krope).astype(BF16)
    vt_ref[...] = _dot_nt(wvt_ref[...], ckvn).astype(BF16)


def _mla_prep(mla_in, qg, kg, wqt, wqpt, wk, wvt, cosp, sinp, cost, sint, *, batch, seq, tile=256):
    u3 = mla_in.reshape(batch, seq, 640)
    nt = seq // tile
    return pl.pallas_call(
        _mla_prep_kernel,
        grid=(batch, nt),
        in_specs=[pl.BlockSpec((None, tile, 640), lambda b, t: (b, t, 0)),
                  _const_spec((1, Q_LORA)), _const_spec((1, KV_LORA)),
                  _const_spec(wqt.shape), _const_spec(wqpt.shape), _const_spec(wk.shape), _const_spec(wvt.shape),
                  pl.BlockSpec((tile, 128), lambda b, t: (t, 0)), pl.BlockSpec((tile, 128), lambda b, t: (t, 0)),
                  pl.BlockSpec((MLA_ROPE, tile), lambda b, t: (0, t)), pl.BlockSpec((MLA_ROPE, tile), lambda b, t: (0, t))],
        out_specs=(pl.BlockSpec((None, tile, 512), lambda b, t: (b, t, 0)),
                   pl.BlockSpec((None, 512, tile), lambda b, t: (b, 0, t)),
                   pl.BlockSpec((None, None, 256, tile), lambda b, t: (b, t, 0, 0))),
        out_shape=(jax.ShapeDtypeStruct((batch, seq, 512), BF16),
                   jax.ShapeDtypeStruct((batch, 512, seq), BF16),
                   jax.ShapeDtypeStruct((batch, nt, 256, tile), BF16)),
        compiler_params=_cparams(("parallel", "parallel")),
        name="mla_prep",
    )(u3, qg, kg, wqt, wqpt, wk, wvt, cosp, sinp, cost, sint)


def _mla_attn_kernel(k_ref, vt_ref, qt_ref, o_ref, *, tile):
    qi = pl.program_id(1)
    kl = lax.broadcasted_iota(jnp.int32, (tile, tile), 0)
    iq = lax.broadcasted_iota(jnp.int32, (tile, tile), 1)
    causal = kl <= iq
    outs = []
    for h in range(MLA_HEADS):
        qt = qt_ref[h * 128:(h + 1) * 128, :]

        def update(s, vt, m, l, acc):
            m_new = jnp.maximum(m, jnp.max(s, axis=0, keepdims=True))
            p = jnp.exp(s - m_new)
            alpha = jnp.exp(m - m_new)
            return (m_new, alpha * l + jnp.sum(p, axis=0, keepdims=True),
                    alpha * acc + _dot(vt, p.astype(BF16)))

        def body(kt, carry, h=h, qt=qt):
            k = k_ref[pl.ds(pl.multiple_of(kt * tile, tile), tile), h * 128:(h + 1) * 128]
            return update(_dot(k, qt), vt_ref[kt, h * MLA_V:(h + 1) * MLA_V, :], *carry)

        init = (jnp.full((1, tile), NEG_INF, F32), jnp.zeros((1, tile), F32), jnp.zeros((MLA_V, tile), F32))
        m, l, acc = lax.fori_loop(0, qi, body, init)
        k = k_ref[pl.ds(pl.multiple_of(qi * tile, tile), tile), h * 128:(h + 1) * 128]
        s = jnp.where(causal, _dot(k, qt), NEG_INF)
        m, l, acc = update(s, vt_ref[qi, h * MLA_V:(h + 1) * MLA_V, :], m, l, acc)
        outs.append(acc * (1.0 / l))
    o_ref[...] = jnp.concatenate(outs, axis=0).T.astype(BF16)


def _mla_attn(k, qt, vt, *, batch, seq, tile=256):
    nt = seq // tile
    y = pl.pallas_call(
        functools.partial(_mla_attn_kernel, tile=tile),
        grid=(batch, nt),
        in_specs=[pl.BlockSpec((None, seq, 512), lambda b, q: (b, 0, 0)),
                  pl.BlockSpec((None, nt, 256, tile), lambda b, q: (b, 0, 0, 0)),
                  pl.BlockSpec((None, 512, tile), lambda b, q: (b, 0, q))],
        out_specs=pl.BlockSpec((None, tile, 256), lambda b, q: (b, q, 0)),
        out_shape=jax.ShapeDtypeStruct((batch, seq, 256), BF16),
        compiler_params=_cparams(("parallel", "parallel")),
        name="mla_attention",
    )(k, vt, qt)
    return y.reshape(batch * seq, 256)


def _mlp_kernel(h_ref, yc_ref, yn_ref, ym_ref, yp_ref, wo_ref, g_ref, w1_ref, w2_ref, fg_ref, o_ref, *, final, chunk):
    y = jnp.concatenate([yc_ref[...], yn_ref[...], ym_ref[...], yp_ref[...]], axis=1)
    h2 = h_ref[...] + _dot(y, wo_ref[...])
    xn = _rms(h2, g_ref[...]).astype(BF16)
    acc = h2
    for c in range(D_FF // chunk):
        z = _dot(xn, w1_ref[:, c * chunk:(c + 1) * chunk])
        a = jnp.square(jnp.maximum(z, 0.0)).astype(BF16)
        acc = acc + _dot(a, w2_ref[c * chunk:(c + 1) * chunk, :])
    if final:
        acc = _rms(acc, fg_ref[...])
    o_ref[...] = acc


def _out_mlp(h, yc, yn, ym, yp, wo, g, w1, w2, fg, *, final, tile=512, chunk=1024):
    n = h.shape[0]
    row = lambda w: pl.BlockSpec((tile, w), lambda i: (i, 0))
    once = lambda shape: pl.BlockSpec(shape, lambda i: (0,) * len(shape), pipeline_mode=pl.Buffered(1))
    return pl.pallas_call(
        functools.partial(_mlp_kernel, final=final, chunk=chunk),
        grid=(n // tile,),
        in_specs=[row(D_MODEL), row(256), row(256), row(256), row(256),
                  once(wo.shape), _const_spec((1, D_MODEL)), once(w1.shape), once(w2.shape),
                  _const_spec((1, D_MODEL))],
        out_specs=row(D_MODEL),
        out_shape=jax.ShapeDtypeStruct((n, D_MODEL), F32),
        compiler_params=_cparams(("parallel",)),
        name="out_mlp",
    )(h, yc, yn, ym, yp, wo, g, w1, w2, fg)


def _bucket_np(n):
    n = np.maximum(n, 0)
    max_exact = N_BUCKETS // 2
    nf = np.maximum(n, 1).astype(np.float32)
    large = max_exact + (np.log(nf / np.float32(max_exact)) / np.float32(math.log(MAX_DIST / max_exact))
                         * np.float32(N_BUCKETS - max_exact)).astype(np.int32)
    large = np.minimum(large, N_BUCKETS - 1)
    return np.where(n < max_exact, n, large).astype(np.int32)


def _bias_tiles(rel_table):
    shifted = rel_table - rel_table[N_BUCKETS - 1][None, :]
    kl = np.arange(QB)[:, None]
    iq = np.arange(QB)[None, :]

    def tile(dist):
        t = jnp.take(shifted, jnp.asarray(_bucket_np(dist)), axis=0)
        return jnp.transpose(t, (0, 2, 1)).reshape(dist.shape[0], NSA_HEADS * QB).astype(F32)

    bd = tile(iq - kl)
    bd1 = tile(iq + QB - kl)
    cc = np.arange(32)[:, None]
    bc = tile(iq - CMP_STRIDE * (cc - 16) - (CMP_LEN - 1))
    return bd, bd1, bc


def _overlap_t(seq):
    nc = seq // CMP_STRIDE
    ns = seq // SLC_LEN
    c_lo = np.arange(nc)[None, :] * CMP_STRIDE
    s_lo = np.arange(ns)[:, None] * SLC_LEN
    ov = np.clip(np.minimum(c_lo + CMP_LEN, s_lo + SLC_LEN) - np.maximum(c_lo, s_lo), 0, None) / CMP_LEN
    ov[:, nc - 1] = 0.0
    return jnp.asarray(ov, BF16)


def _rope_tables(seq):
    pos = jnp.arange(seq, dtype=F32)
    inv_freq = ROPE_THETA ** (-jnp.arange(0, MLA_ROPE, 2, dtype=F32) / MLA_ROPE)
    ang = pos[:, None] * inv_freq[None, :]
    cos = jnp.concatenate([jnp.cos(ang)] * 2, axis=1)
    sin = jnp.concatenate([jnp.sin(ang)] * 2, axis=1)
    place = lambda a: jnp.pad(a, ((0, 0), (MLA_NOPE, 128 - MLA_NOPE - MLA_ROPE)))
    return place(cos), place(sin), cos.T, sin.T


def _rot_cols(w):
    half = MLA_ROPE // 2
    return jnp.concatenate([-w[..., half:], w[..., :half]], axis=-1)


def _layer_weights(w_in, conv_dw, cmp_pos, cmp_w1, cmp_w2, w_uq, w_ukv, pool_w):
    kv = lambda i: w_in[:, _C_KV + 64 * i:_C_KV + 64 * (i + 1)]
    kr = w_in[:, _C_KR:_C_KR + MLA_ROPE]
    place = lambda a: jnp.pad(a, ((0, 0), (MLA_NOPE, 128 - MLA_NOPE - MLA_ROPE)))
    wrow = jnp.concatenate([
        w_in[:, _C_CONV:_C_CONV + 512], w_in[:, _C_POOL:_C_POOL + 256], kv(0), kv(1), kv(2), kv(4),
        w_in[:, _C_CQ:_C_CQ + Q_LORA], w_in[:, _C_CKV:_C_CKV + KV_LORA], place(kr), place(_rot_cols(kr))],
        axis=1).astype(BF16)
    wt = jnp.concatenate([
        w_in[:, _C_Q:_C_Q + 256].T * (NSA_HD ** -0.5), kv(3).T, kv(5).T,
        jnp.pad(w_in[:, _C_G:_C_G + 12].T, ((0, 4), (0, 0)))], axis=0).astype(BF16)

    dw = jnp.pad(conv_dw, ((0, 1), (0, 0)))

    w1r = cmp_w1.reshape(2, CMP_LEN, NSA_HD, 128)
    zk = jnp.zeros_like(w1r[0])
    w1kv = jnp.concatenate([jnp.concatenate([w1r[0], zk], axis=-1),
                            jnp.concatenate([zk, w1r[1]], axis=-1)], axis=1)
    w1a = w1kv[:16].reshape(16 * 128, 256).astype(BF16)
    w1b = w1kv[16:].reshape(16 * 128, 256).astype(BF16)
    pos_kv = jnp.concatenate([cmp_pos[0], cmp_pos[1]], axis=-1)
    posa = pos_kv[:16].reshape(1, 16 * 128)
    posb = pos_kv[16:].reshape(1, 16 * 128)
    z2 = jnp.zeros_like(cmp_w2[0])
    w2bd = jnp.concatenate([jnp.concatenate([cmp_w2[0], z2], axis=1),
                            jnp.concatenate([z2, cmp_w2[1]], axis=1)], axis=0).astype(BF16)

    wq = w_uq.reshape(Q_LORA, MLA_HEADS, MLA_NOPE + MLA_ROPE)
    wqt = jnp.pad(wq, ((0, 0), (0, 0), (0, 128 - MLA_NOPE - MLA_ROPE))).reshape(Q_LORA, 512).T.astype(BF16)
    wqpt = _rot_cols(wq[:, :, MLA_NOPE:]).reshape(Q_LORA, MLA_HEADS * MLA_ROPE).T.astype(BF16)
    wkv = w_ukv.reshape(KV_LORA, MLA_HEADS, MLA_NOPE + MLA_V)
    wk = jnp.pad(wkv[:, :, :MLA_NOPE], ((0, 0), (0, 0), (0, 128 - MLA_NOPE))).reshape(KV_LORA, 512).astype(BF16)
    wvt = wkv[:, :, MLA_NOPE:].reshape(KV_LORA, MLA_HEADS * MLA_V).T.astype(BF16)

    poolw = jax.scipy.linalg.block_diag(*[pool_w[g] for g in range(len(POOL_WINDOWS))]).astype(BF16)
    return dict(wrow=wrow, wt=wt, dw=dw, w1a=w1a, w1b=w1b, posa=posa, posb=posb, w2bd=w2bd,
                wqt=wqt, wqpt=wqpt, wk=wk, wvt=wvt, poolw=poolw)


def kernel(x, w_in, w_out, ln_mix_g, ln_mlp_g, conv_dw, conv_dw_b, conv_ln_g, conv_ln_b, conv_pw,
           nsa_cmp_pos, nsa_cmp_w1, nsa_cmp_w2, mla_q_norm_g, mla_w_uq, mla_kv_norm_g, mla_w_ukv,
           pool_w, pool_scale, mlp_w1, mlp_w2, rel_bias_table, final_norm_g):
    batch, seq, _ = x.shape
    depth = w_in.shape[0]
    bd, bd1, bc = _bias_tiles(rel_bias_table)
    ovl_t = _overlap_t(seq)
    cosp, sinp, cost, sint = _rope_tables(seq)
    r1 = lambda v: v.reshape(1, -1)
    h = x.reshape(batch * seq, D_MODEL)
    for l in range(depth):
        w = _layer_weights(w_in[l], conv_dw[l], nsa_cmp_pos[l], nsa_cmp_w1[l], nsa_cmp_w2[l],
                           mla_w_uq[l], mla_w_ukv[l], pool_w[l])
        uc, up, kcvc, ksw, mla_in, qt, vswt, gt = _proj(h, r1(ln_mix_g[l]), w["wrow"], w["wt"])
        yc, yp = _local(uc, up, w["dw"], r1(conv_dw_b[l]), r1(conv_ln_g[l]), r1(conv_ln_b[l]),
                        conv_pw[l].astype(BF16), w["poolw"], r1(pool_scale[l]), batch=batch, seq=seq)
        kvc, kvct = _compress(kcvc, w["posa"], w["posb"], w["w1a"], w["w1b"], w["w2bd"], batch=batch, seq=seq)
        yn = _nsa(kvc, kvct, ksw, vswt, qt, gt, ovl_t, bd, bd1, bc, batch=batch, seq=seq)
        km, qtm, vtm = _mla_prep(mla_in, r1(mla_q_norm_g[l]), r1(mla_kv_norm_g[l]), w["wqt"], w["wqpt"],
                                 w["wk"], w["wvt"], cosp, sinp, cost, sint, batch=batch, seq=seq)
        ym = _mla_attn(km, qtm, vtm, batch=batch, seq=seq)
        h = _out_mlp(h, yc, yn, ym, yp, w_out[l].astype(BF16), r1(ln_mlp_g[l]), mlp_w1[l].astype(BF16),
                     mlp_w2[l].astype(BF16), r1(final_norm_g), final=(l == depth - 1))
    return h.reshape(batch, seq, D_MODEL)
```

```python
import functools
import math

import numpy as np
import jax
import jax.numpy as jnp
from jax import lax
from jax.experimental import pallas as pl
from jax.experimental.pallas import tpu as pltpu

F32 = jnp.float32
BF16 = jnp.bfloat16

D_MODEL = 1024
GROUP_W = 256
CONV_K = 31
NSA_HEADS = 4
NSA_HD = 64
CMP_LEN = 32
CMP_STRIDE = 16
SLC_LEN = 64
N_SELECT = 16
WINDOW = 512
MLA_HEADS = 4
MLA_NOPE = 64
MLA_ROPE = 32
MLA_V = 64
Q_LORA = 256
KV_LORA = 128
ROPE_THETA = 10000.0
POOL_WINDOWS = (2, 4, 8, 16)
D_FF = 4 * D_MODEL
N_BUCKETS = 32
MAX_DIST = 128
EPS = 1e-6
NEG_INF = -1e30
FORCE_SCORE = 1e9

LANES = 128
QB = 128
HALO = 32
VMEM_LIMIT = 56 * 1024 * 1024

_C_CONV, _C_Q, _C_KV, _C_G, _C_CQ, _C_CKV, _C_KR, _C_POOL = 0, 512, 768, 1152, 1164, 1420, 1548, 1580


def _cparams(sem):
    return pltpu.CompilerParams(dimension_semantics=sem, vmem_limit_bytes=VMEM_LIMIT)


def _rms(x, g):
    ms = jnp.mean(x * x, axis=-1, keepdims=True)
    return x * lax.rsqrt(ms + EPS) * g


def _dot(a, b):
    return jnp.dot(a, b, preferred_element_type=F32)


def _dot_nt(a, b):
    return lax.dot_general(a, b, (((1,), (1,)), ((), ())), preferred_element_type=F32)


def _const_spec(shape):
    nd = len(shape)
    return pl.BlockSpec(shape, lambda *_: (0,) * nd)


def _proj_kernel(x_ref, g_ref, wrow_ref, wt_ref, conv_ref, pool_ref, kcvc_ref, ksw_ref, mla_ref,
                 qt_ref, vswt_ref, gt_ref, *, tile):
    xn = _rms(x_ref[...], g_ref[...]).astype(BF16)
    conv_ref[...] = _dot(xn, wrow_ref[:, 0:512])
    pool_ref[...] = _dot(xn, wrow_ref[:, 512:768])
    kcvc_ref[...] = _dot(xn, wrow_ref[:, 768:896])
    ksw_ref[...] = _dot(xn, wrow_ref[:, 896:1024]).astype(BF16)
    mla_ref[...] = _dot(xn, wrow_ref[:, 1024:1664])
    qt_ref[...] = _dot_nt(wt_ref[0:256, :], xn).astype(BF16)
    vs = _dot_nt(wt_ref[256:384, :], xn).astype(BF16)
    for j in range(tile // LANES):
        vswt_ref[j] = vs[:, j * LANES:(j + 1) * LANES]
    gt_ref[...] = _dot_nt(wt_ref[384:400, :], xn)


def _proj(h, g, wrow, wt, *, tile=512):
    n = h.shape[0]
    grid = (n // tile,)
    outs = (
        jax.ShapeDtypeStruct((n, 512), F32),
        jax.ShapeDtypeStruct((n, 256), F32),
        jax.ShapeDtypeStruct((n, 128), F32),
        jax.ShapeDtypeStruct((n, 128), BF16),
        jax.ShapeDtypeStruct((n, 640), F32),
        jax.ShapeDtypeStruct((256, n), BF16),
        jax.ShapeDtypeStruct((n // LANES, 128, LANES), BF16),
        jax.ShapeDtypeStruct((16, n), F32),
    )
    row = lambda w: pl.BlockSpec((tile, w), lambda i: (i, 0))
    return pl.pallas_call(
        functools.partial(_proj_kernel, tile=tile),
        grid=grid,
        in_specs=[row(D_MODEL), _const_spec((1, D_MODEL)), _const_spec(wrow.shape), _const_spec(wt.shape)],
        out_specs=(row(512), row(256), row(128), row(128), row(640),
                   pl.BlockSpec((256, tile), lambda i: (0, i)),
                   pl.BlockSpec((tile // LANES, 128, LANES), lambda i: (i, 0, 0)),
                   pl.BlockSpec((16, tile), lambda i: (0, i))),
        out_shape=outs,
        compiler_params=_cparams(("parallel",)),
        name="in_proj",
    )(h, g, wrow, wt)


def _local_kernel(uc_ref, uch_ref, up_ref, uph_ref, dw_ref, dwb_ref, lng_ref, lnb_ref, pw_ref,
                  poolw_ref, pscale_ref, yc_ref, yp_ref, hbuf, ubuf, sa, sb, *, tile, rows):
    t = pl.program_id(1)
    keep = jnp.where(t > 0, 1.0, 0.0).astype(F32)

    def glu(u):
        return u[:, 0:GROUP_W] * jax.nn.sigmoid(u[:, GROUP_W:2 * GROUP_W])

    hbuf[0:HALO, :] = glu(uch_ref[...]) * keep
    hbuf[HALO:HALO + tile, :] = glu(uc_ref[...])
    dwb = dwb_ref[...]
    for c in range(tile // rows):
        acc = jnp.zeros((rows, GROUP_W), F32) + dwb
        for k in range(CONV_K):
            acc = acc + dw_ref[k:k + 1, :] * hbuf[pl.ds(c * rows + HALO - (CONV_K - 1) + k, rows), :]
        mu = jnp.mean(acc, axis=-1, keepdims=True)
        cen = acc - mu
        var = jnp.mean(cen * cen, axis=-1, keepdims=True)
        y = cen * lax.rsqrt(var + EPS) * lng_ref[...] + lnb_ref[...]
        y = y * jax.nn.sigmoid(y)
        yc_ref[c * rows:(c + 1) * rows, :] = _dot(y.astype(BF16), pw_ref[...]).astype(BF16)

    ext = HALO + tile
    ubuf[0:HALO, :] = uph_ref[...] * keep
    ubuf[HALO:ext, :] = up_ref[...]
    sa[pl.ds(8, ext - 8), :] = ubuf[pl.ds(8, ext - 8), :] + ubuf[pl.ds(7, ext - 8), :]
    sb[pl.ds(16, ext - 16), :] = sa[pl.ds(16, ext - 16), :] + sa[pl.ds(14, ext - 16), :]
    s2 = sa[pl.ds(HALO, tile), :]
    s4 = sb[pl.ds(HALO, tile), :]
    sa[pl.ds(24, ext - 24), :] = sb[pl.ds(24, ext - 24), :] + sb[pl.ds(20, ext - 24), :]
    s8 = sa[pl.ds(HALO, tile), :]
    s16 = s8 + sa[pl.ds(HALO - 8, tile), :]
    lane = lax.broadcasted_iota(jnp.int32, (tile, GROUP_W), 1)
    grp = lane // (GROUP_W // len(POOL_WINDOWS))
    wsum = jnp.where(grp == 0, s2, jnp.where(grp == 1, s4, jnp.where(grp == 2, s8, s16)))
    width = jnp.where(grp == 0, 2, jnp.where(grp == 1, 4, jnp.where(grp == 2, 8, 16)))
    pos = t * tile + lax.broadcasted_iota(jnp.int32, (tile, GROUP_W), 0)
    cnt = jnp.minimum(width, pos + 1).astype(F32)
    d = wsum / cnt - up_ref[...]
    yp_ref[...] = (_dot(d.astype(BF16), poolw_ref[...]) * pscale_ref[...]).astype(BF16)


def _local(uc, up, dw, dwb, lng, lnb, pw, poolw, pscale, *, batch, seq, tile=512, rows=128):
    uc3 = uc.reshape(batch, seq, 512)
    up3 = up.reshape(batch, seq, GROUP_W)
    hpt = tile // HALO
    cur = lambda w: pl.BlockSpec((None, tile, w), lambda b, t: (b, t, 0))
    halo = lambda w: pl.BlockSpec((None, HALO, w), lambda b, t: (b, jnp.maximum(t * hpt - 1, 0), 0))
    yc, yp = pl.pallas_call(
        functools.partial(_local_kernel, tile=tile, rows=rows),
        grid=(batch, seq // tile),
        in_specs=[cur(512), halo(512), cur(GROUP_W), halo(GROUP_W),
                  _const_spec(dw.shape), _const_spec((1, GROUP_W)), _const_spec((1, GROUP_W)),
                  _const_spec((1, GROUP_W)), _const_spec((GROUP_W, GROUP_W)),
                  _const_spec((GROUP_W, GROUP_W)), _const_spec((1, GROUP_W))],
        out_specs=(cur(GROUP_W), cur(GROUP_W)),
        out_shape=(jax.ShapeDtypeStruct((batch, seq, GROUP_W), BF16),
                   jax.ShapeDtypeStruct((batch, seq, GROUP_W), BF16)),
        scratch_shapes=[pltpu.VMEM((HALO + tile, GROUP_W), F32)] * 4,
        compiler_params=_cparams(("parallel", "parallel")),
        name="local_mixers",
    )(uc3, uc3, up3, up3, dw, dwb, lng, lnb, pw, poolw, pscale)
    return yc.reshape(batch * seq, GROUP_W), yp.reshape(batch * seq, GROUP_W)


def _compress_kernel(kr_ref, posa_ref, posb_ref, w1a_ref, w1b_ref, w2_ref, kv_ref, kvt_ref, bbuf, *, nc):
    kr = kr_ref[...]
    a = _dot((kr + posa_ref[...]).astype(BF16), w1a_ref[...])
    b = _dot((kr + posb_ref[...]).astype(BF16), w1b_ref[...])
    bbuf[0:nc, :] = b
    bbuf[nc:nc + 8, :] = jnp.zeros((8, 256), F32)
    hdn = a + bbuf[pl.ds(1, nc), :]
    hdn = hdn * jax.nn.sigmoid(hdn)
    kv = _dot(hdn.astype(BF16), w2_ref[...])
    kv_ref[...] = kv.astype(BF16)
    kvt_ref[...] = kv.T.astype(BF16)


def _compress(kcvc, posa, posb, w1a, w1b, w2bd, *, batch, seq):
    nc = seq // CMP_STRIDE
    kr = kcvc.reshape(batch, nc, CMP_STRIDE * 128)
    return pl.pallas_call(
        functools.partial(_compress_kernel, nc=nc),
        grid=(batch,),
        in_specs=[pl.BlockSpec((None, nc, CMP_STRIDE * 128), lambda b: (b, 0, 0)),
                  _const_spec(posa.shape), _const_spec(posb.shape), _const_spec(w1a.shape),
                  _const_spec(w1b.shape), _const_spec(w2bd.shape)],
        out_specs=(pl.BlockSpec((None, nc, 128), lambda b: (b, 0, 0)),
                   pl.BlockSpec((None, 128, nc), lambda b: (b, 0, 0))),
        out_shape=(jax.ShapeDtypeStruct((batch, nc, 128), BF16),
                   jax.ShapeDtypeStruct((batch, 128, nc), BF16)),
        scratch_shapes=[pltpu.VMEM((nc + 8, 256), F32)],
        compiler_params=_cparams(("parallel",)),
        name="nsa_compress",
    )(kr, posa, posb, w1a, w1b, w2bd)


def _flash_step(s, valid, vt, m_ref, l_ref, acc_ref):
    if valid is not None:
        s = jnp.where(valid, s, NEG_INF)
    m_old = m_ref[...]
    m_new = jnp.maximum(m_old, jnp.max(s, axis=0, keepdims=True))
    p = jnp.exp(s - m_new)
    if valid is not None:
        p = jnp.where(valid, p, 0.0)
    alpha = jnp.exp(m_old - m_new)
    l_ref[...] = alpha * l_ref[...] + jnp.sum(p, axis=0, keepdims=True)
    acc_ref[...] = alpha * acc_ref[...] + _dot(vt, p.astype(BF16))
    m_ref[...] = m_new


def _nsa_kernel(kvc_ref, kvct_ref, ksw_ref, vswt_ref, qt_ref, gt_ref, ovl_ref, bd_ref, bd1_ref, bc_ref,
                y_ref, s_ref, imp_ref, sel_ref, m_ref, l_ref, acc_ref, oc_ref, os_ref, *, nc, ns, n_top):
    qb = pl.program_id(1)
    hq = NSA_HEADS * QB
    qt = jnp.concatenate([qt_ref[h * NSA_HD:(h + 1) * NSA_HD, :] for h in range(NSA_HEADS)], axis=1)
    zer = jnp.zeros_like(qt)
    qts = jnp.concatenate([qt, zer], axis=0)
    qtw = jnp.concatenate([zer, qt], axis=0)
    lane_q = lax.broadcasted_iota(jnp.int32, (1, hq), 1) % QB

    pad = 16
    s_ref[0:pad, :] = jnp.zeros((pad, hq), F32)
    s_ref[pad + nc:pad + nc + pad, :] = jnp.zeros((pad, hq), F32)
    s_ref[pad:pad + nc, :] = _dot(kvc_ref[...], qts)
    w0 = pl.multiple_of(qb * 8, 8)
    s_ref[pl.ds(w0, 32), :] = s_ref[pl.ds(w0, 32), :] + bc_ref[...]
    sc = s_ref[pad:pad + nc, :]
    cidx = lax.broadcasted_iota(jnp.int32, (nc, hq), 0)
    tq = qb * QB + lane_q
    valid_c = (cidx * CMP_STRIDE + (CMP_LEN - 1) <= tq) & (cidx < nc - 1)
    sc = jnp.where(valid_c, sc, NEG_INF)
    mc = jnp.max(sc, axis=0, keepdims=True)
    pc = jnp.where(valid_c, jnp.exp(sc - mc), 0.0)
    lc = jnp.sum(pc, axis=0, keepdims=True)
    pc = pc * jnp.where(lc > 0.0, 1.0 / lc, 0.0)
    oc_ref[...] = _dot(kvct_ref[NSA_HD:2 * NSA_HD, :], pc.astype(BF16))

    psum = pc[:, 0:QB]
    for h in range(1, NSA_HEADS):
        psum = psum + pc[:, h * QB:(h + 1) * QB]
    p_hi = psum.astype(BF16)
    p_lo = (psum - p_hi.astype(F32)).astype(BF16)
    imp = _dot(ovl_ref[...], p_hi) + _dot(ovl_ref[...], p_lo)
    jidx = lax.broadcasted_iota(jnp.int32, (ns, QB), 0)
    iq = lax.broadcasted_iota(jnp.int32, (ns, QB), 1)
    jc = 2 * qb + (iq >= SLC_LEN).astype(jnp.int32)
    forced = (jidx == 0) | (jidx == jc) | (jidx == jc - 1)
    imp = jnp.where(forced, FORCE_SCORE, imp)
    imp = jnp.where(jidx <= jc, imp, NEG_INF)
    imp_ref[...] = imp

    def rank_body(jp, cnt):
        row = imp_ref[pl.ds(jp, 1), :]
        beats = (row > imp) | ((row == imp) & (jidx > jp))
        return cnt + beats.astype(jnp.int32)

    cnt = lax.fori_loop(0, jnp.minimum(2 * qb + 2, ns), rank_body, jnp.zeros((ns, QB), jnp.int32))
    sel_ref[...] = jnp.where((cnt < n_top) & (imp > 0.5 * NEG_INF), 1.0, 0.0)

    kl = lax.broadcasted_iota(jnp.int32, (QB, hq), 0)
    causal = kl <= lane_q
    anti = kl > lane_q

    def init_state():
        m_ref[...] = jnp.full((1, hq), NEG_INF, F32)
        l_ref[...] = jnp.zeros((1, hq), F32)
        acc_ref[...] = jnp.zeros((NSA_HD, hq), F32)

    def sel_mask(kt):
        m0 = jnp.broadcast_to(sel_ref[pl.ds(2 * kt, 1), :], (SLC_LEN, QB))
        m1 = jnp.broadcast_to(sel_ref[pl.ds(2 * kt + 1, 1), :], (SLC_LEN, QB))
        mt = jnp.concatenate([m0, m1], axis=0)
        return jnp.concatenate([mt] * NSA_HEADS, axis=1) > 0.5

    def ktile(kt):
        return ksw_ref[pl.ds(pl.multiple_of(kt * QB, QB), QB), :]

    init_state()

    def far_body(kt, carry):
        _flash_step(_dot(ktile(kt), qts), sel_mask(kt), vswt_ref[kt, 0:NSA_HD, :], m_ref, l_ref, acc_ref)
        return carry

    lax.fori_loop(0, jnp.maximum(qb - 1, 0), far_body, 0)

    @pl.when(qb >= 1)
    def _():
        kt = qb - 1
        _flash_step(_dot(ktile(kt), qts) + bd1_ref[...], sel_mask(kt), vswt_ref[kt, 0:NSA_HD, :],
                    m_ref, l_ref, acc_ref)

    _flash_step(_dot(ktile(qb), qts) + bd_ref[...], sel_mask(qb) & causal, vswt_ref[qb, 0:NSA_HD, :],
                m_ref, l_ref, acc_ref)
    os_ref[...] = acc_ref[...] * (1.0 / l_ref[...])

    init_state()
    n_w = WINDOW // QB
    for back in range(n_w, 0, -1):
        @pl.when(qb >= back)
        def _(back=back):
            kt = qb - back
            s = _dot(ktile(kt), qtw)
            if back == 1:
                s = s + bd1_ref[...]
            _flash_step(s, anti if back == n_w else None, vswt_ref[kt, NSA_HD:2 * NSA_HD, :],
                        m_ref, l_ref, acc_ref)
    _flash_step(_dot(ktile(qb), qtw) + bd_ref[...], causal, vswt_ref[qb, NSA_HD:2 * NSA_HD, :],
                m_ref, l_ref, acc_ref)
    ow = acc_ref[...] * (1.0 / l_ref[...])

    gates = jax.nn.sigmoid(gt_ref[...])
    oc = oc_ref[...]
    osel = os_ref[...]
    parts = []
    for h in range(NSA_HEADS):
        sl = slice(h * QB, (h + 1) * QB)
        parts.append(gates[3 * h:3 * h + 1, :] * oc[:, sl] + gates[3 * h + 1:3 * h + 2, :] * osel[:, sl]
                     + gates[3 * h + 2:3 * h + 3, :] * ow[:, sl])
    y_ref[...] = jnp.concatenate(parts, axis=0).T.astype(BF16)


def _nsa(kvc, kvct, ksw, vswt, qt, gt, ovl_t, bd, bd1, bc, *, batch, seq):
    nc = seq // CMP_STRIDE
    ns = seq // SLC_LEN
    nqb = seq // QB
    hq = NSA_HEADS * QB
    ksw3 = ksw.reshape(batch, seq, 128)
    vswt4 = vswt.reshape(batch, nqb, 128, LANES)
    return pl.pallas_call(
        functools.partial(_nsa_kernel, nc=nc, ns=ns, n_top=min(N_SELECT, ns)),
        grid=(batch, nqb),
        in_specs=[pl.BlockSpec((None, nc, 128), lambda b, q: (b, 0, 0)),
                  pl.BlockSpec((None, 128, nc), lambda b, q: (b, 0, 0)),
                  pl.BlockSpec((None, seq, 128), lambda b, q: (b, 0, 0)),
                  pl.BlockSpec((None, nqb, 128, LANES), lambda b, q: (b, 0, 0, 0)),
                  pl.BlockSpec((256, QB), lambda b, q: (0, b * nqb + q)),
                  pl.BlockSpec((16, QB), lambda b, q: (0, b * nqb + q)),
                  _const_spec(ovl_t.shape), _const_spec(bd.shape), _const_spec(bd1.shape),
                  _const_spec(bc.shape)],
        out_specs=pl.BlockSpec((QB, 256), lambda b, q: (b * nqb + q, 0)),
        out_shape=jax.ShapeDtypeStruct((batch * seq, 256), BF16),
        scratch_shapes=[pltpu.VMEM((nc + 32, hq), F32),
                        pltpu.VMEM((ns, QB), F32),
                        pltpu.VMEM((ns, QB), F32),
                        pltpu.VMEM((1, hq), F32), pltpu.VMEM((1, hq), F32), pltpu.VMEM((NSA_HD, hq), F32),
                        pltpu.VMEM((NSA_HD, hq), F32), pltpu.VMEM((NSA_HD, hq), F32)],
        compiler_params=_cparams(("parallel", "parallel")),
        name="nsa_attention",
    )(kvc, kvct, ksw3, vswt4, qt, gt, ovl_t, bd, bd1, bc)


def _mla_prep_kernel(u_ref, qg_ref, kg_ref, wqt_ref, wqpt_ref, wk_ref, wvt_ref, cosp_ref, sinp_ref,
                     cost_ref, sint_ref, k_ref, qt_ref, vt_ref):
    scale = (MLA_NOPE + MLA_ROPE) ** -0.5
    cqn = _rms(u_ref[:, 0:Q_LORA], qg_ref[...]).astype(BF16)
    ckvn = _rms(u_ref[:, Q_LORA:Q_LORA + KV_LORA], kg_ref[...]).astype(BF16)
    qta = _dot_nt(wqt_ref[...], cqn)
    qtp = _dot_nt(wqpt_ref[...], cqn)
    cos_t = cost_ref[...]
    sin_t = sint_ref[...]
    for h in range(MLA_HEADS):
        r0 = h * 128
        qt_ref[r0:r0 + MLA_NOPE, :] = (qta[r0:r0 + MLA_NOPE, :] * scale).astype(BF16)
        rope = qta[r0 + MLA_NOPE:r0 + MLA_NOPE + MLA_ROPE, :] * cos_t + qtp[h * MLA_ROPE:(h + 1) * MLA_ROPE, :] * sin_t
        qt_ref[r0 + MLA_NOPE:r0 + MLA_NOPE + MLA_ROPE, :] = (rope * scale).astype(BF16)
        qt_ref[r0 + MLA_NOPE + MLA_ROPE:r0 + 128, :] = jnp.zeros((128 - MLA_NOPE - MLA_ROPE, qta.shape[1]), BF16)
    krope = u_ref[:, 384:512] * cosp_ref[...] + u_ref[:, 512:640] * sinp_ref[...]
    knope = _dot(ckvn, wk_ref[...])
    for h in range(MLA_HEADS):
        k_ref[:, h * 128:(h + 1) * 128] = (knope[:, h * 128:(h + 1) * 128] + krope).astype(BF16)
    vt_ref[...] = _dot_nt(wvt_ref[...], ckvn).astype(BF16)


def _mla_prep(mla_in, qg, kg, wqt, wqpt, wk, wvt, cosp, sinp, cost, sint, *, batch, seq, tile=256):
    u3 = mla_in.reshape(batch, seq, 640)
    nt = seq // tile
    return pl.pallas_call(
        _mla_prep_kernel,
        grid=(batch, nt),
        in_specs=[pl.BlockSpec((None, tile, 640), lambda b, t: (b, t, 0)),
                  _const_spec((1, Q_LORA)), _const_spec((1, KV_LORA)),
                  _const_spec(wqt.shape), _const_spec(wqpt.shape), _const_spec(wk.shape), _const_spec(wvt.shape),
                  pl.BlockSpec((tile, 128), lambda b, t: (t, 0)), pl.BlockSpec((tile, 128), lambda b, t: (t, 0)),
                  pl.BlockSpec((MLA_ROPE, tile), lambda b, t: (0, t)), pl.BlockSpec((MLA_ROPE, tile), lambda b, t: (0, t))],
        out_specs=(pl.BlockSpec((None, tile, 512), lambda b, t: (b, t, 0)),
                   pl.BlockSpec((None, 512, tile), lambda b, t: (b, 0, t)),
                   pl.BlockSpec((None, None, 256, tile), lambda b, t: (b, t, 0, 0))),
        out_shape=(jax.ShapeDtypeStruct((batch, seq, 512), BF16),
                   jax.ShapeDtypeStruct((batch, 512, seq), BF16),
                   jax.ShapeDtypeStruct((batch, nt, 256, tile), BF16)),
        compiler_params=_cparams(("parallel", "parallel")),
        name="mla_prep",
    )(u3, qg, kg, wqt, wqpt, wk, wvt, cosp, sinp, cost, sint)


def _mla_attn_kernel(k_ref, vt_ref, qt_ref, o_ref, m_ref, l_ref, acc_ref, *, tile):
    qi = pl.program_id(1)
    kl = lax.broadcasted_iota(jnp.int32, (tile, tile), 0)
    iq = lax.broadcasted_iota(jnp.int32, (tile, tile), 1)
    causal = kl <= iq
    m_ref[...] = jnp.full(m_ref.shape, NEG_INF, F32)
    l_ref[...] = jnp.zeros(l_ref.shape, F32)
    acc_ref[...] = jnp.zeros(acc_ref.shape, F32)

    def attend(kt, n_sub, mask):
        heads = range(MLA_HEADS)
        scores = []
        for h in heads:
            hs = slice(h * 128, (h + 1) * 128)
            k = k_ref[pl.ds(pl.multiple_of(kt * tile, tile), n_sub * tile), hs]
            s = _dot(k, qt_ref[hs, :])
            if mask is not None:
                s = jnp.where(mask, s, NEG_INF)
            scores.append(s)
        probs, alphas = [], []
        for h in heads:
            m_old = m_ref[h:h + 1, :]
            m_new = jnp.maximum(m_old, jnp.max(scores[h], axis=0, keepdims=True))
            p = jnp.exp(scores[h] - m_new)
            alpha = jnp.exp(m_old - m_new)
            l_ref[h:h + 1, :] = alpha * l_ref[h:h + 1, :] + jnp.sum(p, axis=0, keepdims=True)
            m_ref[h:h + 1, :] = m_new
            probs.append(p.astype(BF16))
            alphas.append(alpha)
        for h in heads:
            vt = jnp.concatenate([vt_ref[kt + j, h * MLA_V:(h + 1) * MLA_V, :] for j in range(n_sub)], axis=1)
            acc_ref[h] = alphas[h] * acc_ref[h] + _dot(vt, probs[h])

    def pair_body(kp, carry):
        attend(2 * kp, 2, None)
        return carry

    lax.fori_loop(0, qi // 2, pair_body, 0)

    @pl.when(qi % 2 == 1)
    def _():
        attend(qi - 1, 1, None)

    attend(qi, 1, causal)
    outs = [acc_ref[h] * (1.0 / l_ref[h:h + 1, :]) for h in range(MLA_HEADS)]
    o_ref[...] = jnp.concatenate(outs, axis=0).T.astype(BF16)


def _mla_attn(k, qt, vt, *, batch, seq, tile=256):
    nt = seq // tile
    y = pl.pallas_call(
        functools.partial(_mla_attn_kernel, tile=tile),
        grid=(batch, nt),
        in_specs=[pl.BlockSpec((None, seq, 512), lambda b, q: (b, 0, 0)),
                  pl.BlockSpec((None, nt, 256, tile), lambda b, q: (b, 0, 0, 0)),
                  pl.BlockSpec((None, 512, tile), lambda b, q: (b, 0, q))],
        out_specs=pl.BlockSpec((None, tile, 256), lambda b, q: (b, q, 0)),
        out_shape=jax.ShapeDtypeStruct((batch, seq, 256), BF16),
        scratch_shapes=[pltpu.VMEM((8, tile), F32), pltpu.VMEM((8, tile), F32),
                        pltpu.VMEM((MLA_HEADS, MLA_V, tile), F32)],
        compiler_params=_cparams(("parallel", "parallel")),
        name="mla_attention",
    )(k, vt, qt)
    return y.reshape(batch * seq, 256)


def _mlp_kernel(h_ref, yc_ref, yn_ref, ym_ref, yp_ref, wo_ref, g_ref, w1_ref, w2_ref, fg_ref, o_ref, *, final, chunk):
    y = jnp.concatenate([yc_ref[...], yn_ref[...], ym_ref[...], yp_ref[...]], axis=1)
    h2 = h_ref[...] + _dot(y, wo_ref[...])
    xn = _rms(h2, g_ref[...]).astype(BF16)
    acc = h2
    for c in range(D_FF // chunk):
        z = _dot(xn, w1_ref[:, c * chunk:(c + 1) * chunk])
        a = jnp.square(jnp.maximum(z, 0.0)).astype(BF16)
        acc = acc + _dot(a, w2_ref[c * chunk:(c + 1) * chunk, :])
    if final:
        acc = _rms(acc, fg_ref[...])
    o_ref[...] = acc


def _out_mlp(h, yc, yn, ym, yp, wo, g, w1, w2, fg, *, final, tile=512, chunk=1024):
    n = h.shape[0]
    row = lambda w: pl.BlockSpec((tile, w), lambda i: (i, 0))
    once = lambda shape: pl.BlockSpec(shape, lambda i: (0,) * len(shape), pipeline_mode=pl.Buffered(1))
    return pl.pallas_call(
        functools.partial(_mlp_kernel, final=final, chunk=chunk),
        grid=(n // tile,),
        in_specs=[row(D_MODEL), row(256), row(256), row(256), row(256),
                  once(wo.shape), _const_spec((1, D_MODEL)), once(w1.shape), once(w2.shape),
                  _const_spec((1, D_MODEL))],
        out_specs=row(D_MODEL),
        out_shape=jax.ShapeDtypeStruct((n, D_MODEL), F32),
        compiler_params=_cparams(("parallel",)),
        name="out_mlp",
    )(h, yc, yn, ym, yp, wo, g, w1, w2, fg)


def _bucket_np(n):
    n = np.maximum(n, 0)
    max_exact = N_BUCKETS // 2
    nf = np.maximum(n, 1).astype(np.float32)
    large = max_exact + (np.log(nf / np.float32(max_exact)) / np.float32(math.log(MAX_DIST / max_exact))
                         * np.float32(N_BUCKETS - max_exact)).astype(np.int32)
    large = np.minimum(large, N_BUCKETS - 1)
    return np.where(n < max_exact, n, large).astype(np.int32)


def _bias_tiles(rel_table):
    shifted = rel_table - rel_table[N_BUCKETS - 1][None, :]
    kl = np.arange(QB)[:, None]
    iq = np.arange(QB)[None, :]

    def tile(dist):
        t = jnp.take(shifted, jnp.asarray(_bucket_np(dist)), axis=0)
        return jnp.transpose(t, (0, 2, 1)).reshape(dist.shape[0], NSA_HEADS * QB).astype(F32)

    bd = tile(iq - kl)
    bd1 = tile(iq + QB - kl)
    cc = np.arange(32)[:, None]
    bc = tile(iq - CMP_STRIDE * (cc - 16) - (CMP_LEN - 1))
    return bd, bd1, bc


def _overlap_t(seq):
    nc = seq // CMP_STRIDE
    ns = seq // SLC_LEN
    c_lo = np.arange(nc)[None, :] * CMP_STRIDE
    s_lo = np.arange(ns)[:, None] * SLC_LEN
    ov = np.clip(np.minimum(c_lo + CMP_LEN, s_lo + SLC_LEN) - np.maximum(c_lo, s_lo), 0, None) / CMP_LEN
    ov[:, nc - 1] = 0.0
    return jnp.asarray(ov, BF16)


def _rope_tables(seq):
    pos = jnp.arange(seq, dtype=F32)
    inv_freq = ROPE_THETA ** (-jnp.arange(0, MLA_ROPE, 2, dtype=F32) / MLA_ROPE)
    ang = pos[:, None] * inv_freq[None, :]
    cos = jnp.concatenate([jnp.cos(ang)] * 2, axis=1)
    sin = jnp.concatenate([jnp.sin(ang)] * 2, axis=1)
    place = lambda a: jnp.pad(a, ((0, 0), (MLA_NOPE, 128 - MLA_NOPE - MLA_ROPE)))
    return place(cos), place(sin), cos.T, sin.T


def _rot_cols(w):
    half = MLA_ROPE // 2
    return jnp.concatenate([-w[..., half:], w[..., :half]], axis=-1)


def _layer_weights(w_in, conv_dw, cmp_pos, cmp_w1, cmp_w2, w_uq, w_ukv, pool_w):
    kv = lambda i: w_in[:, _C_KV + 64 * i:_C_KV + 64 * (i + 1)]
    kr = w_in[:, _C_KR:_C_KR + MLA_ROPE]
    place = lambda a: jnp.pad(a, ((0, 0), (MLA_NOPE, 128 - MLA_NOPE - MLA_ROPE)))
    wrow = jnp.concatenate([
        w_in[:, _C_CONV:_C_CONV + 512], w_in[:, _C_POOL:_C_POOL + 256], kv(0), kv(1), kv(2), kv(4),
        w_in[:, _C_CQ:_C_CQ + Q_LORA], w_in[:, _C_CKV:_C_CKV + KV_LORA], place(kr), place(_rot_cols(kr))],
        axis=1).astype(BF16)
    wt = jnp.concatenate([
        w_in[:, _C_Q:_C_Q + 256].T * (NSA_HD ** -0.5), kv(3).T, kv(5).T,
        jnp.pad(w_in[:, _C_G:_C_G + 12].T, ((0, 4), (0, 0)))], axis=0).astype(BF16)

    dw = jnp.pad(conv_dw, ((0, 1), (0, 0)))

    w1r = cmp_w1.reshape(2, CMP_LEN, NSA_HD, 128)
    zk = jnp.zeros_like(w1r[0])
    w1kv = jnp.concatenate([jnp.concatenate([w1r[0], zk], axis=-1),
                            jnp.concatenate([zk, w1r[1]], axis=-1)], axis=1)
    w1a = w1kv[:16].reshape(16 * 128, 256).astype(BF16)
    w1b = w1kv[16:].reshape(16 * 128, 256).astype(BF16)
    pos_kv = jnp.concatenate([cmp_pos[0], cmp_pos[1]], axis=-1)
    posa = pos_kv[:16].reshape(1, 16 * 128)
    posb = pos_kv[16:].reshape(1, 16 * 128)
    z2 = jnp.zeros_like(cmp_w2[0])
    w2bd = jnp.concatenate([jnp.concatenate([cmp_w2[0], z2], axis=1),
                            jnp.concatenate([z2, cmp_w2[1]], axis=1)], axis=0).astype(BF16)

    wq = w_uq.reshape(Q_LORA, MLA_HEADS, MLA_NOPE + MLA_ROPE)
    wqt = jnp.pad(wq, ((0, 0), (0, 0), (0, 128 - MLA_NOPE - MLA_ROPE))).reshape(Q_LORA, 512).T.astype(BF16)
    wqpt = _rot_cols(wq[:, :, MLA_NOPE:]).reshape(Q_LORA, MLA_HEADS * MLA_ROPE).T.astype(BF16)
    wkv = w_ukv.reshape(KV_LORA, MLA_HEADS, MLA_NOPE + MLA_V)
    wk = jnp.pad(wkv[:, :, :MLA_NOPE], ((0, 0), (0, 0), (0, 128 - MLA_NOPE))).reshape(KV_LORA, 512).astype(BF16)
    wvt = wkv[:, :, MLA_NOPE:].reshape(KV_LORA, MLA_HEADS * MLA_V).T.astype(BF16)

    poolw = jax.scipy.linalg.block_diag(*[pool_w[g] for g in range(len(POOL_WINDOWS))]).astype(BF16)
    return dict(wrow=wrow, wt=wt, dw=dw, w1a=w1a, w1b=w1b, posa=posa, posb=posb, w2bd=w2bd,
                wqt=wqt, wqpt=wqpt, wk=wk, wvt=wvt, poolw=poolw)


def kernel(x, w_in, w_out, ln_mix_g, ln_mlp_g, conv_dw, conv_dw_b, conv_ln_g, conv_ln_b, conv_pw,
           nsa_cmp_pos, nsa_cmp_w1, nsa_cmp_w2, mla_q_norm_g, mla_w_uq, mla_kv_norm_g, mla_w_ukv,
           pool_w, pool_scale, mlp_w1, mlp_w2, rel_bias_table, final_norm_g):
    batch, seq, _ = x.shape
    depth = w_in.shape[0]
    bd, bd1, bc = _bias_tiles(rel_bias_table)
    ovl_t = _overlap_t(seq)
    cosp, sinp, cost, sint = _rope_tables(seq)
    r1 = lambda v: v.reshape(1, -1)
    h = x.reshape(batch * seq, D_MODEL)
    for l in range(depth):
        w = _layer_weights(w_in[l], conv_dw[l], nsa_cmp_pos[l], nsa_cmp_w1[l], nsa_cmp_w2[l],
                           mla_w_uq[l], mla_w_ukv[l], pool_w[l])
        uc, up, kcvc, ksw, mla_in, qt, vswt, gt = _proj(h, r1(ln_mix_g[l]), w["wrow"], w["wt"])
        yc, yp = _local(uc, up, w["dw"], r1(conv_dw_b[l]), r1(conv_ln_g[l]), r1(conv_ln_b[l]),
                        conv_pw[l].astype(BF16), w["poolw"], r1(pool_scale[l]), batch=batch, seq=seq)
        kvc, kvct = _compress(kcvc, w["posa"], w["posb"], w["w1a"], w["w1b"], w["w2bd"], batch=batch, seq=seq)
        yn = _nsa(kvc, kvct, ksw, vswt, qt, gt, ovl_t, bd, bd1, bc, batch=batch, seq=seq)
        km, qtm, vtm = _mla_prep(mla_in, r1(mla_q_norm_g[l]), r1(mla_kv_norm_g[l]), w["wqt"], w["wqpt"],
                                 w["wk"], w["wvt"], cosp, sinp, cost, sint, batch=batch, seq=seq)
        ym = _mla_attn(km, qtm, vtm, batch=batch, seq=seq)
        h = _out_mlp(h, yc, yn, ym, yp, w_out[l].astype(BF16), r1(ln_mlp_g[l]), mlp_w1[l].astype(BF16),
                     mlp_w2[l].astype(BF16), r1(final_norm_g), final=(l == depth - 1))
    return h.reshape(batch, seq, D_MODEL)
```

```python
import functools
import math

import numpy as np
import jax
import jax.numpy as jnp
from jax import lax
from jax.experimental import pallas as pl
from jax.experimental.pallas import tpu as pltpu

F32 = jnp.float32
BF16 = jnp.bfloat16

D_MODEL = 1024
GROUP_W = 256
CONV_K = 31
NSA_HEADS = 4
NSA_HD = 64
CMP_LEN = 32
CMP_STRIDE = 16
SLC_LEN = 64
N_SELECT = 16
WINDOW = 512
MLA_HEADS = 4
MLA_NOPE = 64
MLA_ROPE = 32
MLA_V = 64
Q_LORA = 256
KV_LORA = 128
ROPE_THETA = 10000.0
POOL_WINDOWS = (2, 4, 8, 16)
D_FF = 4 * D_MODEL
N_BUCKETS = 32
MAX_DIST = 128
EPS = 1e-6
NEG_INF = -1e30
FORCE_SCORE = 1e9

LANES = 128
QB = 128
HALO = 32
FAR_TILE = 512
FAR_BLOCKS = FAR_TILE // SLC_LEN
VT_ROWS = 80
VMEM_LIMIT = 56 * 1024 * 1024

_C_CONV, _C_Q, _C_KV, _C_G, _C_CQ, _C_CKV, _C_KR, _C_POOL = 0, 512, 768, 1152, 1164, 1420, 1548, 1580


def _cparams(sem):
    return pltpu.CompilerParams(dimension_semantics=sem, vmem_limit_bytes=VMEM_LIMIT)


def _rms(x, g):
    ms = jnp.mean(x * x, axis=-1, keepdims=True)
    return x * lax.rsqrt(ms + EPS) * g


def _dot(a, b):
    return jnp.dot(a, b, preferred_element_type=F32)


def _dot_nt(a, b):
    return lax.dot_general(a, b, (((1,), (1,)), ((), ())), preferred_element_type=F32)


def _const_spec(shape):
    nd = len(shape)
    return pl.BlockSpec(shape, lambda *_: (0,) * nd)


def _proj_kernel(x_ref, g_ref, wrow_ref, wt_ref, conv_ref, pool_ref, kcvc_ref, ks_ref, kw_ref, mla_ref,
                 qt_ref, vt_ref, gt_ref, *, tile):
    xn = _rms(x_ref[...], g_ref[...]).astype(BF16)
    conv_ref[...] = _dot(xn, wrow_ref[:, 0:512])
    pool_ref[...] = _dot(xn, wrow_ref[:, 512:768])
    kcvc_ref[...] = _dot(xn, wrow_ref[:, 768:896])
    tok = pl.program_id(0) * tile + lax.broadcasted_iota(jnp.int32, (tile, LANES), 0)
    lane = lax.broadcasted_iota(jnp.int32, (tile, LANES), 1)
    onehot = (lane - NSA_HD == (tok // SLC_LEN) % FAR_BLOCKS).astype(F32)
    ks_ref[...] = (_dot(xn, wrow_ref[:, 896:1024]) + onehot).astype(BF16)
    kw_ref[...] = _dot(xn, wrow_ref[:, 1024:1152]).astype(BF16)
    mla_ref[...] = _dot(xn, wrow_ref[:, 1152:1792])
    qt_ref[...] = _dot_nt(wt_ref[0:256, :], xn).astype(BF16)
    vs = _dot_nt(wt_ref[256:384, :], xn).astype(BF16)
    ones = jnp.ones((VT_ROWS - NSA_HD, LANES), BF16)
    for j in range(tile // LANES):
        cols = slice(j * LANES, (j + 1) * LANES)
        vt_ref[j, 0:NSA_HD, :] = vs[0:NSA_HD, cols]
        vt_ref[j, NSA_HD:VT_ROWS, :] = ones
        vt_ref[j, VT_ROWS:VT_ROWS + NSA_HD, :] = vs[NSA_HD:2 * NSA_HD, cols]
        vt_ref[j, VT_ROWS + NSA_HD:2 * VT_ROWS, :] = ones
    gt_ref[...] = _dot_nt(wt_ref[384:400, :], xn)


def _proj(h, g, wrow, wt, *, tile=512):
    n = h.shape[0]
    grid = (n // tile,)
    outs = (
        jax.ShapeDtypeStruct((n, 512), F32),
        jax.ShapeDtypeStruct((n, 256), F32),
        jax.ShapeDtypeStruct((n, 128), F32),
        jax.ShapeDtypeStruct((n, 128), BF16),
        jax.ShapeDtypeStruct((n, 128), BF16),
        jax.ShapeDtypeStruct((n, 640), F32),
        jax.ShapeDtypeStruct((256, n), BF16),
        jax.ShapeDtypeStruct((n // LANES, 2 * VT_ROWS, LANES), BF16),
        jax.ShapeDtypeStruct((16, n), F32),
    )
    row = lambda w: pl.BlockSpec((tile, w), lambda i: (i, 0))
    return pl.pallas_call(
        functools.partial(_proj_kernel, tile=tile),
        grid=grid,
        in_specs=[row(D_MODEL), _const_spec((1, D_MODEL)), _const_spec(wrow.shape), _const_spec(wt.shape)],
        out_specs=(row(512), row(256), row(128), row(128), row(128), row(640),
                   pl.BlockSpec((256, tile), lambda i: (0, i)),
                   pl.BlockSpec((tile // LANES, 2 * VT_ROWS, LANES), lambda i: (i, 0, 0)),
                   pl.BlockSpec((16, tile), lambda i: (0, i))),
        out_shape=outs,
        compiler_params=_cparams(("parallel",)),
        name="in_proj",
    )(h, g, wrow, wt)


def _local_kernel(uc_ref, uch_ref, up_ref, uph_ref, dw_ref, dwb_ref, lng_ref, lnb_ref, pw_ref,
                  poolw_ref, pscale_ref, yc_ref, yp_ref, hbuf, ubuf, sa, sb, *, tile, rows):
    t = pl.program_id(1)
    keep = jnp.where(t > 0, 1.0, 0.0).astype(F32)

    def glu(u):
        return u[:, 0:GROUP_W] * jax.nn.sigmoid(u[:, GROUP_W:2 * GROUP_W])

    hbuf[0:HALO, :] = glu(uch_ref[...]) * keep
    hbuf[HALO:HALO + tile, :] = glu(uc_ref[...])
    dwb = dwb_ref[...]
    for c in range(tile // rows):
        acc = jnp.zeros((rows, GROUP_W), F32) + dwb
        for k in range(CONV_K):
            acc = acc + dw_ref[k:k + 1, :] * hbuf[pl.ds(c * rows + HALO - (CONV_K - 1) + k, rows), :]
        mu = jnp.mean(acc, axis=-1, keepdims=True)
        cen = acc - mu
        var = jnp.mean(cen * cen, axis=-1, keepdims=True)
        y = cen * lax.rsqrt(var + EPS) * lng_ref[...] + lnb_ref[...]
        y = y * jax.nn.sigmoid(y)
        yc_ref[c * rows:(c + 1) * rows, :] = _dot(y.astype(BF16), pw_ref[...]).astype(BF16)

    ext = HALO + tile
    ubuf[0:HALO, :] = uph_ref[...] * keep
    ubuf[HALO:ext, :] = up_ref[...]
    sa[pl.ds(8, ext - 8), :] = ubuf[pl.ds(8, ext - 8), :] + ubuf[pl.ds(7, ext - 8), :]
    sb[pl.ds(16, ext - 16), :] = sa[pl.ds(16, ext - 16), :] + sa[pl.ds(14, ext - 16), :]
    s2 = sa[pl.ds(HALO, tile), :]
    s4 = sb[pl.ds(HALO, tile), :]
    sa[pl.ds(24, ext - 24), :] = sb[pl.ds(24, ext - 24), :] + sb[pl.ds(20, ext - 24), :]
    s8 = sa[pl.ds(HALO, tile), :]
    s16 = s8 + sa[pl.ds(HALO - 8, tile), :]
    lane = lax.broadcasted_iota(jnp.int32, (tile, GROUP_W), 1)
    grp = lane // (GROUP_W // len(POOL_WINDOWS))
    wsum = jnp.where(grp == 0, s2, jnp.where(grp == 1, s4, jnp.where(grp == 2, s8, s16)))
    width = jnp.where(grp == 0, 2, jnp.where(grp == 1, 4, jnp.where(grp == 2, 8, 16)))
    pos = t * tile + lax.broadcasted_iota(jnp.int32, (tile, GROUP_W), 0)
    cnt = jnp.minimum(width, pos + 1).astype(F32)
    d = wsum / cnt - up_ref[...]
    yp_ref[...] = (_dot(d.astype(BF16), poolw_ref[...]) * pscale_ref[...]).astype(BF16)


def _local(uc, up, dw, dwb, lng, lnb, pw, poolw, pscale, *, batch, seq, tile=512, rows=128):
    uc3 = uc.reshape(batch, seq, 512)
    up3 = up.reshape(batch, seq, GROUP_W)
    hpt = tile // HALO
    cur = lambda w: pl.BlockSpec((None, tile, w), lambda b, t: (b, t, 0))
    halo = lambda w: pl.BlockSpec((None, HALO, w), lambda b, t: (b, jnp.maximum(t * hpt - 1, 0), 0))
    yc, yp = pl.pallas_call(
        functools.partial(_local_kernel, tile=tile, rows=rows),
        grid=(batch, seq // tile),
        in_specs=[cur(512), halo(512), cur(GROUP_W), halo(GROUP_W),
                  _const_spec(dw.shape), _const_spec((1, GROUP_W)), _const_spec((1, GROUP_W)),
                  _const_spec((1, GROUP_W)), _const_spec((GROUP_W, GROUP_W)),
                  _const_spec((GROUP_W, GROUP_W)), _const_spec((1, GROUP_W))],
        out_specs=(cur(GROUP_W), cur(GROUP_W)),
        out_shape=(jax.ShapeDtypeStruct((batch, seq, GROUP_W), BF16),
                   jax.ShapeDtypeStruct((batch, seq, GROUP_W), BF16)),
        scratch_shapes=[pltpu.VMEM((HALO + tile, GROUP_W), F32)] * 4,
        compiler_params=_cparams(("parallel", "parallel")),
        name="local_mixers",
    )(uc3, uc3, up3, up3, dw, dwb, lng, lnb, pw, poolw, pscale)
    return yc.reshape(batch * seq, GROUP_W), yp.reshape(batch * seq, GROUP_W)


def _compress_kernel(kr_ref, posa_ref, posb_ref, w1a_ref, w1b_ref, w2_ref, kv_ref, kvt_ref, bbuf, *, nc):
    kr = kr_ref[...]
    a = _dot((kr + posa_ref[...]).astype(BF16), w1a_ref[...])
    b = _dot((kr + posb_ref[...]).astype(BF16), w1b_ref[...])
    bbuf[0:nc, :] = b
    bbuf[nc:nc + 8, :] = jnp.zeros((8, 256), F32)
    hdn = a + bbuf[pl.ds(1, nc), :]
    hdn = hdn * jax.nn.sigmoid(hdn)
    kv = _dot(hdn.astype(BF16), w2_ref[...])
    kv_ref[...] = kv.astype(BF16)
    kvt_ref[...] = kv.T.astype(BF16)


def _compress(kcvc, posa, posb, w1a, w1b, w2bd, *, batch, seq):
    nc = seq // CMP_STRIDE
    kr = kcvc.reshape(batch, nc, CMP_STRIDE * 128)
    return pl.pallas_call(
        functools.partial(_compress_kernel, nc=nc),
        grid=(batch,),
        in_specs=[pl.BlockSpec((None, nc, CMP_STRIDE * 128), lambda b: (b, 0, 0)),
                  _const_spec(posa.shape), _const_spec(posb.shape), _const_spec(w1a.shape),
                  _const_spec(w1b.shape), _const_spec(w2bd.shape)],
        out_specs=(pl.BlockSpec((None, nc, 128), lambda b: (b, 0, 0)),
                   pl.BlockSpec((None, 128, nc), lambda b: (b, 0, 0))),
        out_shape=(jax.ShapeDtypeStruct((batch, nc, 128), BF16),
                   jax.ShapeDtypeStruct((batch, 128, nc), BF16)),
        scratch_shapes=[pltpu.VMEM((nc + 8, 256), F32)],
        compiler_params=_cparams(("parallel",)),
        name="nsa_compress",
    )(kr, posa, posb, w1a, w1b, w2bd)


def _nsa_kernel(kvc_ref, kvct_ref, ks_ref, kw_ref, vt_ref, qt_ref, gt_ref, ovl_ref, bd_ref, bd1_ref, bc_ref,
                y_ref, s_ref, key_ref, tie_ref, sel_ref, mb_ref, m_ref, acc_ref, oc_ref, sa_ref, sb_ref, cma_ref, cmb_ref,
                *, nc, ns, n_top):
    qb = pl.program_id(1)
    hq = NSA_HEADS * QB
    qt = jnp.concatenate([qt_ref[h * NSA_HD:(h + 1) * NSA_HD, :] for h in range(NSA_HEADS)], axis=1)
    qts = jnp.concatenate([qt, jnp.zeros_like(qt)], axis=0)
    lane_q = lax.broadcasted_iota(jnp.int32, (1, hq), 1) % QB

    pad = 16
    s_ref[0:pad, :] = jnp.zeros((pad, hq), F32)
    s_ref[pad + nc:pad + nc + pad, :] = jnp.zeros((pad, hq), F32)
    s_ref[pad:pad + nc, :] = _dot(kvc_ref[...], qts)
    w0 = pl.multiple_of(qb * 8, 8)
    s_ref[pl.ds(w0, 32), :] = s_ref[pl.ds(w0, 32), :] + bc_ref[...]
    sc = s_ref[pad:pad + nc, :]
    cidx = lax.broadcasted_iota(jnp.int32, (nc, hq), 0)
    tq = qb * QB + lane_q
    valid_c = (cidx * CMP_STRIDE + (CMP_LEN - 1) <= tq) & (cidx < nc - 1)
    sc = jnp.where(valid_c, sc, NEG_INF)
    mc = jnp.max(sc, axis=0, keepdims=True)
    pc = jnp.where(valid_c, jnp.exp(sc - mc), 0.0)
    lc = jnp.sum(pc, axis=0, keepdims=True)
    pc = pc * jnp.where(lc > 0.0, 1.0 / lc, 0.0)
    oc_ref[...] = _dot(kvct_ref[NSA_HD:2 * NSA_HD, :], pc.astype(BF16))

    psum = pc[:, 0:QB]
    for h in range(1, NSA_HEADS):
        psum = psum + pc[:, h * QB:(h + 1) * QB]
    p_hi = psum.astype(BF16)
    p_lo = (psum - p_hi.astype(F32)).astype(BF16)
    imp = _dot(ovl_ref[...], p_hi) + _dot(ovl_ref[...], p_lo)
    jidx = lax.broadcasted_iota(jnp.int32, (ns, QB), 0)
    iq = lax.broadcasted_iota(jnp.int32, (ns, QB), 1)
    jc = 2 * qb + (iq >= SLC_LEN).astype(jnp.int32)
    forced = (jidx == 0) | (jidx == jc) | (jidx == jc - 1)
    imp = jnp.where(forced, FORCE_SCORE, imp)
    imp = jnp.where(jidx <= jc, imp, NEG_INF)
    key = pltpu.bitcast(imp, jnp.int32)
    key_ref[...] = key
    tie_ref[...] = jnp.zeros((ns, QB), jnp.int32)
    sub = lax.broadcasted_iota(jnp.int32, (8, QB), 0)
    n_vreg = ns // 8

    def rank_body(g, cnts):
        base = pl.multiple_of(g * 8, 8)
        rows = key_ref[pl.ds(base, 8), :]
        out = []
        for tv in range(n_vreg):
            target = key[tv * 8:(tv + 1) * 8, :] - jnp.where(tv > g, 1, 0)
            c = cnts[tv]
            for r in range(8):
                c = c + (rows[r:r + 1, :] > target).astype(jnp.int32)
            out.append(c)
        ties = jnp.zeros((8, QB), jnp.int32)
        for r in range(8):
            ties = ties + ((rows[r:r + 1, :] == rows) & (sub > r)).astype(jnp.int32)
        tie_ref[pl.ds(base, 8), :] = ties
        return tuple(out)

    n_groups = jnp.minimum((2 * qb + 9) // 8, n_vreg)
    cnts = lax.fori_loop(0, n_groups, rank_body, tuple(jnp.zeros((8, QB), jnp.int32) for _ in range(n_vreg)))
    cnt = jnp.concatenate(cnts, axis=0) + tie_ref[...]
    chosen = (cnt < n_top) & (imp > 0.5 * NEG_INF)
    sel_ref[...] = jnp.where(chosen, 1.0, 0.0)
    mb_ref[...] = jnp.where(chosen & (jidx < 2 * (qb - 1)), 0.0, NEG_INF)

    m_ref[...] = jnp.full((1, hq), NEG_INF, F32)
    acc_ref[...] = jnp.zeros((VT_ROWS, hq), F32)
    n_far = (jnp.maximum(qb - 1, 0) + 3) // 4
    zpad = jnp.zeros((QB - NSA_HD - 2 * FAR_BLOCKS, hq), BF16)

    def q_aug(t):
        rows = mb_ref[pl.ds(pl.multiple_of(t * FAR_BLOCKS, FAR_BLOCKS), FAR_BLOCKS), :]
        rows = jnp.concatenate([rows] * NSA_HEADS, axis=1)
        rows = jnp.concatenate([rows, jnp.zeros_like(rows)], axis=0).astype(BF16)
        return jnp.concatenate([qt, rows, zpad], axis=0)

    def produce(t, s_x, cm_x):
        s = _dot(ks_ref[pl.ds(pl.multiple_of(t * FAR_TILE, FAR_TILE), FAR_TILE), :], q_aug(t))
        s_x[...] = s
        cm_x[...] = jnp.max(s, axis=0, keepdims=True)

    def consume(t, s_x, cm_x):
        m_old = m_ref[...]
        m_new = jnp.maximum(m_old, cm_x[...])
        p = jnp.exp(s_x[...] - m_new).astype(BF16)
        sub = FAR_TILE // QB
        vt = jnp.concatenate([vt_ref[sub * t + j, 0:VT_ROWS, :] for j in range(sub)], axis=1)
        acc_ref[...] = jnp.exp(m_old - m_new) * acc_ref[...] + _dot(vt, p)
        m_ref[...] = m_new

    @pl.when(n_far > 0)
    def _():
        produce(0, sa_ref, cma_ref)

    def pair_body(j, carry):
        produce(2 * j + 1, sb_ref, cmb_ref)
        consume(2 * j, sa_ref, cma_ref)
        produce(jnp.minimum(2 * j + 2, n_far - 1), sa_ref, cma_ref)
        consume(2 * j + 1, sb_ref, cmb_ref)
        return carry

    lax.fori_loop(0, n_far // 2, pair_body, 0)

    @pl.when(n_far % 2 == 1)
    def _():
        consume(n_far - 1, sa_ref, cma_ref)

    kl = lax.broadcasted_iota(jnp.int32, (QB, hq), 0)
    causal = kl <= lane_q
    anti = kl > lane_q
    n_w = WINDOW // QB
    kts = [jnp.maximum(qb - back, 0) for back in range(n_w + 1)]
    live = [qb >= back for back in range(n_w + 1)]

    def krows(ref, kt):
        return ref[pl.ds(pl.multiple_of(kt * QB, QB), QB), :]

    def sel_mask(kt):
        m0 = jnp.broadcast_to(sel_ref[pl.ds(2 * kt, 1), :], (SLC_LEN, QB))
        m1 = jnp.broadcast_to(sel_ref[pl.ds(2 * kt + 1, 1), :], (SLC_LEN, QB))
        mt = jnp.concatenate([m0, m1], axis=0)
        return jnp.concatenate([mt] * NSA_HEADS, axis=1) > 0.5

    bd = bd_ref[...]
    bd1 = bd1_ref[...]
    s_sel = [_dot(krows(ks_ref, qb), qts) + bd, _dot(krows(ks_ref, kts[1]), qts) + bd1]
    s_win = [_dot(krows(kw_ref, kts[back]), qts) for back in range(n_w + 1)]
    s_win[0] = s_win[0] + bd
    s_win[1] = s_win[1] + bd1
    s_sel[0] = jnp.where(sel_mask(qb) & causal, s_sel[0], NEG_INF)
    s_sel[1] = jnp.where(sel_mask(kts[1]) & live[1], s_sel[1], NEG_INF)
    s_win[0] = jnp.where(causal, s_win[0], NEG_INF)
    for back in range(1, n_w):
        s_win[back] = jnp.where(live[back], s_win[back], NEG_INF)
    s_win[n_w] = jnp.where(anti & live[n_w], s_win[n_w], NEG_INF)

    m_old = m_ref[...]
    m_new = m_old
    for s in s_sel:
        m_new = jnp.maximum(m_new, jnp.max(s, axis=0, keepdims=True))
    acc = jnp.exp(m_old - m_new) * acc_ref[...]
    for back, s in enumerate(s_sel):
        acc = acc + _dot(vt_ref[kts[back], 0:VT_ROWS, :], jnp.exp(s - m_new).astype(BF16))
    osel = acc[0:NSA_HD, :] * (1.0 / acc[NSA_HD:NSA_HD + 1, :])

    m_w = jnp.max(s_win[0], axis=0, keepdims=True)
    for s in s_win[1:]:
        m_w = jnp.maximum(m_w, jnp.max(s, axis=0, keepdims=True))
    acc = jnp.zeros((VT_ROWS, hq), F32)
    for back, s in enumerate(s_win):
        acc = acc + _dot(vt_ref[kts[back], VT_ROWS:2 * VT_ROWS, :], jnp.exp(s - m_w).astype(BF16))
    ow = acc[0:NSA_HD, :] * (1.0 / acc[NSA_HD:NSA_HD + 1, :])

    gates = jax.nn.sigmoid(gt_ref[...])
    oc = oc_ref[...]
    parts = []
    for h in range(NSA_HEADS):
        sl = slice(h * QB, (h + 1) * QB)
        parts.append(gates[3 * h:3 * h + 1, :] * oc[:, sl] + gates[3 * h + 1:3 * h + 2, :] * osel[:, sl]
                     + gates[3 * h + 2:3 * h + 3, :] * ow[:, sl])
    y_ref[...] = jnp.concatenate(parts, axis=0).T.astype(BF16)


def _nsa(kvc, kvct, ks, kw, vt, qt, gt, ovl_t, bd, bd1, bc, *, batch, seq):
    nc = seq // CMP_STRIDE
    ns = seq // SLC_LEN
    nqb = seq // QB
    hq = NSA_HEADS * QB
    ks3 = ks.reshape(batch, seq, 128)
    kw3 = kw.reshape(batch, seq, 128)
    vt4 = vt.reshape(batch, nqb, 2 * VT_ROWS, LANES)
    return pl.pallas_call(
        functools.partial(_nsa_kernel, nc=nc, ns=ns, n_top=min(N_SELECT, ns)),
        grid=(batch, nqb),
        in_specs=[pl.BlockSpec((None, nc, 128), lambda b, q: (b, 0, 0)),
                  pl.BlockSpec((None, 128, nc), lambda b, q: (b, 0, 0)),
                  pl.BlockSpec((None, seq, 128), lambda b, q: (b, 0, 0)),
                  pl.BlockSpec((None, seq, 128), lambda b, q: (b, 0, 0)),
                  pl.BlockSpec((None, nqb, 2 * VT_ROWS, LANES), lambda b, q: (b, 0, 0, 0)),
                  pl.BlockSpec((256, QB), lambda b, q: (0, b * nqb + q)),
                  pl.BlockSpec((16, QB), lambda b, q: (0, b * nqb + q)),
                  _const_spec(ovl_t.shape), _const_spec(bd.shape), _const_spec(bd1.shape),
                  _const_spec(bc.shape)],
        out_specs=pl.BlockSpec((QB, 256), lambda b, q: (b * nqb + q, 0)),
        out_shape=jax.ShapeDtypeStruct((batch * seq, 256), BF16),
        scratch_shapes=[pltpu.VMEM((nc + 32, hq), F32),
                        pltpu.VMEM((ns, QB), jnp.int32),
                        pltpu.VMEM((ns, QB), jnp.int32),
                        pltpu.VMEM((ns, QB), F32),
                        pltpu.VMEM((ns, QB), F32),
                        pltpu.VMEM((1, hq), F32),
                        pltpu.VMEM((VT_ROWS, hq), F32),
                        pltpu.VMEM((NSA_HD, hq), F32),
                        pltpu.VMEM((FAR_TILE, hq), F32), pltpu.VMEM((FAR_TILE, hq), F32),
                        pltpu.VMEM((1, hq), F32), pltpu.VMEM((1, hq), F32)],
        compiler_params=_cparams(("parallel", "parallel")),
        name="nsa_attention",
    )(kvc, kvct, ks3, kw3, vt4, qt, gt, ovl_t, bd, bd1, bc)


def _mla_prep_kernel(u_ref, qg_ref, kg_ref, wqt_ref, wqpt_ref, wk_ref, wvt_ref, cosp_ref, sinp_ref,
                     cost_ref, sint_ref, k_ref, qt_ref, vt_ref):
    scale = (MLA_NOPE + MLA_ROPE) ** -0.5
    cqn = _rms(u_ref[:, 0:Q_LORA], qg_ref[...]).astype(BF16)
    ckvn = _rms(u_ref[:, Q_LORA:Q_LORA + KV_LORA], kg_ref[...]).astype(BF16)
    qta = _dot_nt(wqt_ref[...], cqn)
    qtp = _dot_nt(wqpt_ref[...], cqn)
    cos_t = cost_ref[...]
    sin_t = sint_ref[...]
    for h in range(MLA_HEADS):
        r0 = h * 128
        qt_ref[r0:r0 + MLA_NOPE, :] = (qta[r0:r0 + MLA_NOPE, :] * scale).astype(BF16)
        rope = qta[r0 + MLA_NOPE:r0 + MLA_NOPE + MLA_ROPE, :] * cos_t + qtp[h * MLA_ROPE:(h + 1) * MLA_ROPE, :] * sin_t
        qt_ref[r0 + MLA_NOPE:r0 + MLA_NOPE + MLA_ROPE, :] = (rope * scale).astype(BF16)
        qt_ref[r0 + MLA_NOPE + MLA_ROPE:r0 + 128, :] = jnp.zeros((128 - MLA_NOPE - MLA_ROPE, qta.shape[1]), BF16)
    krope = u_ref[:, 384:512] * cosp_ref[...] + u_ref[:, 512:640] * sinp_ref[...]
    knope = _dot(ckvn, wk_ref[...])
    for h in range(MLA_HEADS):
        k_ref[:, h * 128:(h + 1) * 128] = (knope[:, h * 128:(h + 1) * 128] + krope).astype(BF16)
    vt = _dot_nt(wvt_ref[...], ckvn).astype(BF16)
    for h in range(MLA_HEADS):
        vt_ref[h * VT_ROWS:h * VT_ROWS + MLA_V, :] = vt[h * MLA_V:(h + 1) * MLA_V, :]
        vt_ref[h * VT_ROWS + MLA_V:(h + 1) * VT_ROWS, :] = jnp.ones((VT_ROWS - MLA_V, vt.shape[1]), BF16)


def _mla_prep(mla_in, qg, kg, wqt, wqpt, wk, wvt, cosp, sinp, cost, sint, *, batch, seq, tile=256):
    u3 = mla_in.reshape(batch, seq, 640)
    nt = seq // tile
    return pl.pallas_call(
        _mla_prep_kernel,
        grid=(batch, nt),
        in_specs=[pl.BlockSpec((None, tile, 640), lambda b, t: (b, t, 0)),
                  _const_spec((1, Q_LORA)), _const_spec((1, KV_LORA)),
                  _const_spec(wqt.shape), _const_spec(wqpt.shape), _const_spec(wk.shape), _const_spec(wvt.shape),
                  pl.BlockSpec((tile, 128), lambda b, t: (t, 0)), pl.BlockSpec((tile, 128), lambda b, t: (t, 0)),
                  pl.BlockSpec((MLA_ROPE, tile), lambda b, t: (0, t)), pl.BlockSpec((MLA_ROPE, tile), lambda b, t: (0, t))],
        out_specs=(pl.BlockSpec((None, tile, 512), lambda b, t: (b, t, 0)),
                   pl.BlockSpec((None, 512, tile), lambda b, t: (b, 0, t)),
                   pl.BlockSpec((None, None, MLA_HEADS * VT_ROWS, tile), lambda b, t: (b, t, 0, 0))),
        out_shape=(jax.ShapeDtypeStruct((batch, seq, 512), BF16),
                   jax.ShapeDtypeStruct((batch, 512, seq), BF16),
                   jax.ShapeDtypeStruct((batch, nt, MLA_HEADS * VT_ROWS, tile), BF16)),
        compiler_params=_cparams(("parallel", "parallel")),
        name="mla_prep",
    )(u3, qg, kg, wqt, wqpt, wk, wvt, cosp, sinp, cost, sint)


def _mla_attn_kernel(k_ref, vt_ref, qt_ref, o_ref, m_ref, acc_ref, sa_ref, sb_ref, cma_ref, cmb_ref, *, tile):
    qi = pl.program_id(1)
    heads = range(MLA_HEADS)
    m_ref[...] = jnp.full(m_ref.shape, NEG_INF, F32)
    acc_ref[...] = jnp.zeros(acc_ref.shape, F32)

    def scores(t, h):
        hs = slice(h * 128, (h + 1) * 128)
        return _dot(k_ref[pl.ds(pl.multiple_of(t * tile, tile), tile), hs], qt_ref[hs, :])

    def produce(t, s_x, cm_x):
        for h in heads:
            s = scores(t, h)
            s_x[h] = s
            cm_x[h:h + 1, :] = jnp.max(s, axis=0, keepdims=True)

    def consume(t, tiles, maxima):
        probs, alphas = [], []
        for h in heads:
            m_old = m_ref[h:h + 1, :]
            m_new = jnp.maximum(m_old, maxima[h])
            probs.append(jnp.exp(tiles[h] - m_new).astype(BF16))
            alphas.append(jnp.exp(m_old - m_new))
            m_ref[h:h + 1, :] = m_new
        for h in heads:
            acc_ref[h] = alphas[h] * acc_ref[h] + _dot(vt_ref[t, h * VT_ROWS:(h + 1) * VT_ROWS, :], probs[h])

    def consume_from(t, s_x, cm_x):
        consume(t, [s_x[h] for h in heads], [cm_x[h:h + 1, :] for h in heads])

    @pl.when(qi > 0)
    def _():
        produce(0, sa_ref, cma_ref)

    def pair_body(j, carry):
        produce(2 * j + 1, sb_ref, cmb_ref)
        consume_from(2 * j, sa_ref, cma_ref)
        produce(jnp.minimum(2 * j + 2, qi - 1), sa_ref, cma_ref)
        consume_from(2 * j + 1, sb_ref, cmb_ref)
        return carry

    lax.fori_loop(0, qi // 2, pair_body, 0)

    @pl.when(qi % 2 == 1)
    def _():
        consume_from(qi - 1, sa_ref, cma_ref)

    kl = lax.broadcasted_iota(jnp.int32, (tile, tile), 0)
    iq = lax.broadcasted_iota(jnp.int32, (tile, tile), 1)
    diag = [jnp.where(kl <= iq, scores(qi, h), NEG_INF) for h in heads]
    consume(qi, diag, [jnp.max(s, axis=0, keepdims=True) for s in diag])
    outs = [acc_ref[h, 0:MLA_V, :] * (1.0 / acc_ref[h, MLA_V:MLA_V + 1, :]) for h in heads]
    o_ref[...] = jnp.concatenate(outs, axis=0).T.astype(BF16)


def _mla_attn(k, qt, vt, *, batch, seq, tile=256):
    nt = seq // tile
    y = pl.pallas_call(
        functools.partial(_mla_attn_kernel, tile=tile),
        grid=(batch, nt),
        in_specs=[pl.BlockSpec((None, seq, 512), lambda b, q: (b, 0, 0)),
                  pl.BlockSpec((None, nt, MLA_HEADS * VT_ROWS, tile), lambda b, q: (b, 0, 0, 0)),
                  pl.BlockSpec((None, 512, tile), lambda b, q: (b, 0, q))],
        out_specs=pl.BlockSpec((None, tile, 256), lambda b, q: (b, q, 0)),
        out_shape=jax.ShapeDtypeStruct((batch, seq, 256), BF16),
        scratch_shapes=[pltpu.VMEM((8, tile), F32),
                        pltpu.VMEM((MLA_HEADS, VT_ROWS, tile), F32),
                        pltpu.VMEM((MLA_HEADS, tile, tile), F32), pltpu.VMEM((MLA_HEADS, tile, tile), F32),
                        pltpu.VMEM((8, tile), F32), pltpu.VMEM((8, tile), F32)],
        compiler_params=_cparams(("parallel", "parallel")),
        name="mla_attention",
    )(k, vt, qt)
    return y.reshape(batch * seq, 256)


def _mlp_kernel(h_ref, yc_ref, yn_ref, ym_ref, yp_ref, wo_ref, g_ref, w1_ref, w2_ref, fg_ref, o_ref, *, final, chunk):
    y = jnp.concatenate([yc_ref[...], yn_ref[...], ym_ref[...], yp_ref[...]], axis=1)
    h2 = h_ref[...] + _dot(y, wo_ref[...])
    xn = _rms(h2, g_ref[...]).astype(BF16)
    acc = h2
    for c in range(D_FF // chunk):
        z = _dot(xn, w1_ref[:, c * chunk:(c + 1) * chunk])
        a = jnp.square(jnp.maximum(z, 0.0)).astype(BF16)
        acc = acc + _dot(a, w2_ref[c * chunk:(c + 1) * chunk, :])
    if final:
        acc = _rms(acc, fg_ref[...])
    o_ref[...] = acc


def _out_mlp(h, yc, yn, ym, yp, wo, g, w1, w2, fg, *, final, tile=512, chunk=1024):
    n = h.shape[0]
    row = lambda w: pl.BlockSpec((tile, w), lambda i: (i, 0))
    once = lambda shape: pl.BlockSpec(shape, lambda i: (0,) * len(shape), pipeline_mode=pl.Buffered(1))
    return pl.pallas_call(
        functools.partial(_mlp_kernel, final=final, chunk=chunk),
        grid=(n // tile,),
        in_specs=[row(D_MODEL), row(256), row(256), row(256), row(256),
                  once(wo.shape), _const_spec((1, D_MODEL)), once(w1.shape), once(w2.shape),
                  _const_spec((1, D_MODEL))],
        out_specs=row(D_MODEL),
        out_shape=jax.ShapeDtypeStruct((n, D_MODEL), F32),
        compiler_params=_cparams(("parallel",)),
        name="out_mlp",
    )(h, yc, yn, ym, yp, wo, g, w1, w2, fg)


def _bucket_np(n):
    n = np.maximum(n, 0)
    max_exact = N_BUCKETS // 2
    nf = np.maximum(n, 1).astype(np.float32)
    large = max_exact + (np.log(nf / np.float32(max_exact)) / np.float32(math.log(MAX_DIST / max_exact))
                         * np.float32(N_BUCKETS - max_exact)).astype(np.int32)
    large = np.minimum(large, N_BUCKETS - 1)
    return np.where(n < max_exact, n, large).astype(np.int32)


def _bias_tiles(rel_table):
    shifted = rel_table - rel_table[N_BUCKETS - 1][None, :]
    kl = np.arange(QB)[:, None]
    iq = np.arange(QB)[None, :]

    def tile(dist):
        t = jnp.take(shifted, jnp.asarray(_bucket_np(dist)), axis=0)
        return jnp.transpose(t, (0, 2, 1)).reshape(dist.shape[0], NSA_HEADS * QB).astype(F32)

    bd = tile(iq - kl)
    bd1 = tile(iq + QB - kl)
    cc = np.arange(32)[:, None]
    bc = tile(iq - CMP_STRIDE * (cc - 16) - (CMP_LEN - 1))
    return bd, bd1, bc


def _overlap_t(seq):
    nc = seq // CMP_STRIDE
    ns = seq // SLC_LEN
    c_lo = np.arange(nc)[None, :] * CMP_STRIDE
    s_lo = np.arange(ns)[:, None] * SLC_LEN
    ov = np.clip(np.minimum(c_lo + CMP_LEN, s_lo + SLC_LEN) - np.maximum(c_lo, s_lo), 0, None) / CMP_LEN
    ov[:, nc - 1] = 0.0
    return jnp.asarray(ov, BF16)


def _rope_tables(seq):
    pos = jnp.arange(seq, dtype=F32)
    inv_freq = ROPE_THETA ** (-jnp.arange(0, MLA_ROPE, 2, dtype=F32) / MLA_ROPE)
    ang = pos[:, None] * inv_freq[None, :]
    cos = jnp.concatenate([jnp.cos(ang)] * 2, axis=1)
    sin = jnp.concatenate([jnp.sin(ang)] * 2, axis=1)
    place = lambda a: jnp.pad(a, ((0, 0), (MLA_NOPE, 128 - MLA_NOPE - MLA_ROPE)))
    return place(cos), place(sin), cos.T, sin.T


def _rot_cols(w):
    half = MLA_ROPE // 2
    return jnp.concatenate([-w[..., half:], w[..., :half]], axis=-1)


def _layer_weights(w_in, conv_dw, cmp_pos, cmp_w1, cmp_w2, w_uq, w_ukv, pool_w):
    kv = lambda i: w_in[:, _C_KV + 64 * i:_C_KV + 64 * (i + 1)]
    kr = w_in[:, _C_KR:_C_KR + MLA_ROPE]
    place = lambda a: jnp.pad(a, ((0, 0), (MLA_NOPE, 128 - MLA_NOPE - MLA_ROPE)))
    wrow = jnp.concatenate([
        w_in[:, _C_CONV:_C_CONV + 512], w_in[:, _C_POOL:_C_POOL + 256], kv(0), kv(1),
        jnp.pad(kv(2), ((0, 0), (0, 64))), jnp.pad(kv(4), ((0, 0), (0, 64))),
        w_in[:, _C_CQ:_C_CQ + Q_LORA], w_in[:, _C_CKV:_C_CKV + KV_LORA], place(kr), place(_rot_cols(kr))],
        axis=1).astype(BF16)
    wt = jnp.concatenate([
        w_in[:, _C_Q:_C_Q + 256].T * (NSA_HD ** -0.5), kv(3).T, kv(5).T,
        jnp.pad(w_in[:, _C_G:_C_G + 12].T, ((0, 4), (0, 0)))], axis=0).astype(BF16)

    dw = jnp.pad(conv_dw, ((0, 1), (0, 0)))

    w1r = cmp_w1.reshape(2, CMP_LEN, NSA_HD, 128)
    zk = jnp.zeros_like(w1r[0])
    w1kv = jnp.concatenate([jnp.concatenate([w1r[0], zk], axis=-1),
                            jnp.concatenate([zk, w1r[1]], axis=-1)], axis=1)
    w1a = w1kv[:16].reshape(16 * 128, 256).astype(BF16)
    w1b = w1kv[16:].reshape(16 * 128, 256).astype(BF16)
    pos_kv = jnp.concatenate([cmp_pos[0], cmp_pos[1]], axis=-1)
    posa = pos_kv[:16].reshape(1, 16 * 128)
    posb = pos_kv[16:].reshape(1, 16 * 128)
    z2 = jnp.zeros_like(cmp_w2[0])
    w2bd = jnp.concatenate([jnp.concatenate([cmp_w2[0], z2], axis=1),
                            jnp.concatenate([z2, cmp_w2[1]], axis=1)], axis=0).astype(BF16)

    wq = w_uq.reshape(Q_LORA, MLA_HEADS, MLA_NOPE + MLA_ROPE)
    wqt = jnp.pad(wq, ((0, 0), (0, 0), (0, 128 - MLA_NOPE - MLA_ROPE))).reshape(Q_LORA, 512).T.astype(BF16)
    wqpt = _rot_cols(wq[:, :, MLA_NOPE:]).reshape(Q_LORA, MLA_HEADS * MLA_ROPE).T.astype(BF16)
    wkv = w_ukv.reshape(KV_LORA, MLA_HEADS, MLA_NOPE + MLA_V)
    wk = jnp.pad(wkv[:, :, :MLA_NOPE], ((0, 0), (0, 0), (0, 128 - MLA_NOPE))).reshape(KV_LORA, 512).astype(BF16)
    wvt = wkv[:, :, MLA_NOPE:].reshape(KV_LORA, MLA_HEADS * MLA_V).T.astype(BF16)

    poolw = jax.scipy.linalg.block_diag(*[pool_w[g] for g in range(len(POOL_WINDOWS))]).astype(BF16)
    return dict(wrow=wrow, wt=wt, dw=dw, w1a=w1a, w1b=w1b, posa=posa, posb=posb, w2bd=w2bd,
                wqt=wqt, wqpt=wqpt, wk=wk, wvt=wvt, poolw=poolw)


def kernel(x, w_in, w_out, ln_mix_g, ln_mlp_g, conv_dw, conv_dw_b, conv_ln_g, conv_ln_b, conv_pw,
           nsa_cmp_pos, nsa_cmp_w1, nsa_cmp_w2, mla_q_norm_g, mla_w_uq, mla_kv_norm_g, mla_w_ukv,
           pool_w, pool_scale, mlp_w1, mlp_w2, rel_bias_table, final_norm_g):
    batch, seq, _ = x.shape
    depth = w_in.shape[0]
    bd, bd1, bc = _bias_tiles(rel_bias_table)
    ovl_t = _overlap_t(seq)
    cosp, sinp, cost, sint = _rope_tables(seq)
    r1 = lambda v: v.reshape(1, -1)
    h = x.reshape(batch * seq, D_MODEL)
    for l in range(depth):
        w = _layer_weights(w_in[l], conv_dw[l], nsa_cmp_pos[l], nsa_cmp_w1[l], nsa_cmp_w2[l],
                           mla_w_uq[l], mla_w_ukv[l], pool_w[l])
        uc, up, kcvc, ks, kw, mla_in, qt, vt, gt = _proj(h, r1(ln_mix_g[l]), w["wrow"], w["wt"])
        yc, yp = _local(uc, up, w["dw"], r1(conv_dw_b[l]), r1(conv_ln_g[l]), r1(conv_ln_b[l]),
                        conv_pw[l].astype(BF16), w["poolw"], r1(pool_scale[l]), batch=batch, seq=seq)
        kvc, kvct = _compress(kcvc, w["posa"], w["posb"], w["w1a"], w["w1b"], w["w2bd"], batch=batch, seq=seq)
        yn = _nsa(kvc, kvct, ks, kw, vt, qt, gt, ovl_t, bd, bd1, bc, batch=batch, seq=seq)
        km, qtm, vtm = _mla_prep(mla_in, r1(mla_q_norm_g[l]), r1(mla_kv_norm_g[l]), w["wqt"], w["wqpt"],
                                 w["wk"], w["wvt"], cosp, sinp, cost, sint, batch=batch, seq=seq)
        ym = _mla_attn(km, qtm, vtm, batch=batch, seq=seq)
        h = _out_mlp(h, yc, yn, ym, yp, w_out[l].astype(BF16), r1(ln_mlp_g[l]), mlp_w1[l].astype(BF16),
                     mlp_w2[l].astype(BF16), r1(final_norm_g), final=(l == depth - 1))
    return h.reshape(batch, seq, D_MODEL)
```

```python
import functools
import math

import numpy as np
import jax
import jax.numpy as jnp
from jax import lax
from jax.experimental import pallas as pl
from jax.experimental.pallas import tpu as pltpu

F32 = jnp.float32
BF16 = jnp.bfloat16

D_MODEL = 1024
GROUP_W = 256
CONV_K = 31
NSA_HEADS = 4
NSA_HD = 64
CMP_LEN = 32
CMP_STRIDE = 16
SLC_LEN = 64
N_SELECT = 16
WINDOW = 512
MLA_HEADS = 4
MLA_NOPE = 64
MLA_ROPE = 32
MLA_V = 64
Q_LORA = 256
KV_LORA = 128
ROPE_THETA = 10000.0
POOL_WINDOWS = (2, 4, 8, 16)
D_FF = 4 * D_MODEL
N_BUCKETS = 32
MAX_DIST = 128
EPS = 1e-6
NEG_INF = -1e30
FORCE_SCORE = 1e9

LANES = 128
SUBLANES = 8
QB = 128
HALO = 32
FAR_TILE = 512
FAR_BLOCKS = FAR_TILE // SLC_LEN
VT_ROWS = 80
MLA_TILE = 512
VMEM_LIMIT = 56 * 1024 * 1024

_C_CONV, _C_Q, _C_KV, _C_G, _C_CQ, _C_CKV, _C_KR, _C_POOL = 0, 512, 768, 1152, 1164, 1420, 1548, 1580


def _cparams(sem):
    return pltpu.CompilerParams(dimension_semantics=sem, vmem_limit_bytes=VMEM_LIMIT)


def _rms(x, g):
    ms = jnp.mean(x * x, axis=-1, keepdims=True)
    return x * lax.rsqrt(ms + EPS) * g


def _dot(a, b):
    return jnp.dot(a, b, preferred_element_type=F32)


def _dot_nt(a, b):
    return lax.dot_general(a, b, (((1,), (1,)), ((), ())), preferred_element_type=F32)


def _const_spec(shape):
    nd = len(shape)
    return pl.BlockSpec(shape, lambda *_: (0,) * nd)


def _proj_kernel(x_ref, g_ref, wrow_ref, wt_ref, conv_ref, pool_ref, kcvc_ref, ks_ref, kw_ref, mla_ref,
                 qt_ref, vt_ref, gt_ref, *, tile, sub):
    ones = jnp.ones((VT_ROWS - NSA_HD, LANES), BF16)
    lane = lax.broadcasted_iota(jnp.int32, (sub, LANES), 1)
    for r in range(tile // sub):
        rows = slice(r * sub, (r + 1) * sub)
        xn = _rms(x_ref[rows, :], g_ref[...]).astype(BF16)
        conv_ref[rows, :] = _dot(xn, wrow_ref[:, 0:512])
        pool_ref[rows, :] = _dot(xn, wrow_ref[:, 512:768])
        kk = _dot(xn, wrow_ref[:, 768:1024])
        kcvc_ref[rows, :] = kk[:, 0:LANES]
        tok = pl.program_id(0) * tile + r * sub + lax.broadcasted_iota(jnp.int32, (sub, LANES), 0)
        onehot = (lane - NSA_HD == (tok // SLC_LEN) % FAR_BLOCKS).astype(F32)
        ks_ref[rows, :] = (kk[:, LANES:2 * LANES] + onehot).astype(BF16)
        km = _dot(xn, wrow_ref[:, 1024:1792])
        kw_ref[rows, :] = km[:, 0:LANES].astype(BF16)
        mla_ref[rows, :] = km[:, LANES:]
        qt_ref[:, rows] = _dot_nt(wt_ref[0:256, :], xn).astype(BF16)
        vs = _dot_nt(wt_ref[256:384, :], xn).astype(BF16)
        for j in range(sub // LANES):
            cols = slice(j * LANES, (j + 1) * LANES)
            jt = r * (sub // LANES) + j
            vt_ref[jt, 0:NSA_HD, :] = vs[0:NSA_HD, cols]
            vt_ref[jt, NSA_HD:VT_ROWS, :] = ones
            vt_ref[jt, VT_ROWS:VT_ROWS + NSA_HD, :] = vs[NSA_HD:2 * NSA_HD, cols]
            vt_ref[jt, VT_ROWS + NSA_HD:2 * VT_ROWS, :] = ones
        gt_ref[:, rows] = _dot_nt(wt_ref[384:400, :], xn)


def _proj(h, g, wrow, wt, *, tile=1024, sub=512):
    n = h.shape[0]
    grid = (n // tile,)
    outs = (
        jax.ShapeDtypeStruct((n, 512), F32),
        jax.ShapeDtypeStruct((n, 256), F32),
        jax.ShapeDtypeStruct((n, 128), F32),
        jax.ShapeDtypeStruct((n, 128), BF16),
        jax.ShapeDtypeStruct((n, 128), BF16),
        jax.ShapeDtypeStruct((n, 640), F32),
        jax.ShapeDtypeStruct((256, n), BF16),
        jax.ShapeDtypeStruct((n // LANES, 2 * VT_ROWS, LANES), BF16),
        jax.ShapeDtypeStruct((16, n), F32),
    )
    row = lambda w: pl.BlockSpec((tile, w), lambda i: (i, 0))
    return pl.pallas_call(
        functools.partial(_proj_kernel, tile=tile, sub=sub),
        grid=grid,
        in_specs=[row(D_MODEL), _const_spec((1, D_MODEL)), _const_spec(wrow.shape), _const_spec(wt.shape)],
        out_specs=(row(512), row(256), row(128), row(128), row(128), row(640),
                   pl.BlockSpec((256, tile), lambda i: (0, i)),
                   pl.BlockSpec((tile // LANES, 2 * VT_ROWS, LANES), lambda i: (i, 0, 0)),
                   pl.BlockSpec((16, tile), lambda i: (0, i))),
        out_shape=outs,
        compiler_params=_cparams(("parallel",)),
        name="in_proj",
    )(h, g, wrow, wt)


def _local_kernel(uc_ref, uch_ref, up_ref, uph_ref, dw_ref, dwb_ref, lng_ref, lnb_ref, pw_ref,
                  poolw_ref, pscale_ref, yc_ref, yp_ref, hbuf, ubuf, sa, sb, hshift, *, tile, rows):
    t = pl.program_id(1)
    keep = jnp.where(t > 0, 1.0, 0.0).astype(F32)

    def glu(u):
        return u[:, 0:GROUP_W] * jax.nn.sigmoid(u[:, GROUP_W:2 * GROUP_W])

    hbuf[0:HALO, :] = glu(uch_ref[...]) * keep
    hbuf[HALO:HALO + tile, :] = glu(uc_ref[...])
    span = HALO + tile - SUBLANES
    for j in range(1, SUBLANES):
        hshift[j - 1, 0:span, :] = hbuf[pl.ds(j, span), :]
    dwb = dwb_ref[...]
    for c in range(tile // rows):
        acc = jnp.zeros((rows, GROUP_W), F32) + dwb
        for k in range(CONV_K):
            off = c * rows + HALO - (CONV_K - 1) + k
            phase, base = off % SUBLANES, off - off % SUBLANES
            tap = hbuf[pl.ds(base, rows), :] if phase == 0 else hshift[phase - 1, pl.ds(base, rows), :]
            acc = acc + dw_ref[k:k + 1, :] * tap
        mu = jnp.mean(acc, axis=-1, keepdims=True)
        cen = acc - mu
        var = jnp.mean(cen * cen, axis=-1, keepdims=True)
        y = cen * lax.rsqrt(var + EPS) * lng_ref[...] + lnb_ref[...]
        y = y * jax.nn.sigmoid(y)
        yc_ref[c * rows:(c + 1) * rows, :] = _dot(y.astype(BF16), pw_ref[...]).astype(BF16)

    ext = HALO + tile
    ubuf[0:HALO, :] = uph_ref[...] * keep
    ubuf[HALO:ext, :] = up_ref[...]
    sa[pl.ds(8, ext - 8), :] = ubuf[pl.ds(8, ext - 8), :] + ubuf[pl.ds(7, ext - 8), :]
    sb[pl.ds(16, ext - 16), :] = sa[pl.ds(16, ext - 16), :] + sa[pl.ds(14, ext - 16), :]
    s2 = sa[pl.ds(HALO, tile), :]
    s4 = sb[pl.ds(HALO, tile), :]
    sa[pl.ds(24, ext - 24), :] = sb[pl.ds(24, ext - 24), :] + sb[pl.ds(20, ext - 24), :]
    s8 = sa[pl.ds(HALO, tile), :]
    s16 = s8 + sa[pl.ds(HALO - 8, tile), :]
    lane = lax.broadcasted_iota(jnp.int32, (tile, GROUP_W), 1)
    grp = lane // (GROUP_W // len(POOL_WINDOWS))
    wsum = jnp.where(grp == 0, s2, jnp.where(grp == 1, s4, jnp.where(grp == 2, s8, s16)))
    width = jnp.where(grp == 0, 2, jnp.where(grp == 1, 4, jnp.where(grp == 2, 8, 16)))
    pos = t * tile + lax.broadcasted_iota(jnp.int32, (tile, GROUP_W), 0)
    cnt = jnp.minimum(width, pos + 1).astype(F32)
    d = wsum / cnt - up_ref[...]
    yp_ref[...] = (_dot(d.astype(BF16), poolw_ref[...]) * pscale_ref[...]).astype(BF16)


def _local(uc, up, dw, dwb, lng, lnb, pw, poolw, pscale, *, batch, seq, tile=512, rows=128):
    uc3 = uc.reshape(batch, seq, 512)
    up3 = up.reshape(batch, seq, GROUP_W)
    hpt = tile // HALO
    cur = lambda w: pl.BlockSpec((None, tile, w), lambda b, t: (b, t, 0))
    halo = lambda w: pl.BlockSpec((None, HALO, w), lambda b, t: (b, jnp.maximum(t * hpt - 1, 0), 0))
    yc, yp = pl.pallas_call(
        functools.partial(_local_kernel, tile=tile, rows=rows),
        grid=(batch, seq // tile),
        in_specs=[cur(512), halo(512), cur(GROUP_W), halo(GROUP_W),
                  _const_spec(dw.shape), _const_spec((1, GROUP_W)), _const_spec((1, GROUP_W)),
                  _const_spec((1, GROUP_W)), _const_spec((GROUP_W, GROUP_W)),
                  _const_spec((GROUP_W, GROUP_W)), _const_spec((1, GROUP_W))],
        out_specs=(cur(GROUP_W), cur(GROUP_W)),
        out_shape=(jax.ShapeDtypeStruct((batch, seq, GROUP_W), BF16),
                   jax.ShapeDtypeStruct((batch, seq, GROUP_W), BF16)),
        scratch_shapes=[pltpu.VMEM((HALO + tile, GROUP_W), F32)] * 4
        + [pltpu.VMEM((SUBLANES - 1, HALO + tile, GROUP_W), F32)],
        compiler_params=_cparams(("parallel", "parallel")),
        name="local_mixers",
    )(uc3, uc3, up3, up3, dw, dwb, lng, lnb, pw, poolw, pscale)
    return yc.reshape(batch * seq, GROUP_W), yp.reshape(batch * seq, GROUP_W)


def _compress_kernel(kr_ref, posa_ref, posb_ref, w1a_ref, w1b_ref, w2_ref, kv_ref, kvt_ref, bbuf, *, nc):
    kr = kr_ref[...]
    a = _dot((kr + posa_ref[...]).astype(BF16), w1a_ref[...])
    b = _dot((kr + posb_ref[...]).astype(BF16), w1b_ref[...])
    bbuf[0:nc, :] = b
    bbuf[nc:nc + 8, :] = jnp.zeros((8, 256), F32)
    hdn = a + bbuf[pl.ds(1, nc), :]
    hdn = hdn * jax.nn.sigmoid(hdn)
    kv = _dot(hdn.astype(BF16), w2_ref[...])
    kv_ref[...] = kv.astype(BF16)
    kvt_ref[...] = kv.T.astype(BF16)


def _compress(kcvc, posa, posb, w1a, w1b, w2bd, *, batch, seq):
    nc = seq // CMP_STRIDE
    kr = kcvc.reshape(batch, nc, CMP_STRIDE * 128)
    return pl.pallas_call(
        functools.partial(_compress_kernel, nc=nc),
        grid=(batch,),
        in_specs=[pl.BlockSpec((None, nc, CMP_STRIDE * 128), lambda b: (b, 0, 0)),
                  _const_spec(posa.shape), _const_spec(posb.shape), _const_spec(w1a.shape),
                  _const_spec(w1b.shape), _const_spec(w2bd.shape)],
        out_specs=(pl.BlockSpec((None, nc, 128), lambda b: (b, 0, 0)),
                   pl.BlockSpec((None, 128, nc), lambda b: (b, 0, 0))),
        out_shape=(jax.ShapeDtypeStruct((batch, nc, 128), BF16),
                   jax.ShapeDtypeStruct((batch, 128, nc), BF16)),
        scratch_shapes=[pltpu.VMEM((nc + 8, 256), F32)],
        compiler_params=_cparams(("parallel",)),
        name="nsa_compress",
    )(kr, posa, posb, w1a, w1b, w2bd)


def _nsa_kernel(kvc_ref, kvct_ref, ks_ref, kw_ref, vt_ref, qt_ref, gt_ref, ovl_ref, bd_ref, bd1_ref, bc_ref,
                y_ref, s_ref, key_ref, tie_ref, sel_ref, mb_ref, m_ref, acc_ref, oc_ref, ow_ref, sa_ref, sb_ref, cma_ref, cmb_ref,
                *, nc, ns, n_top):
    qb = pl.program_id(1)
    hq = NSA_HEADS * QB
    qt = jnp.concatenate([qt_ref[h * NSA_HD:(h + 1) * NSA_HD, :] for h in range(NSA_HEADS)], axis=1)
    qts = jnp.concatenate([qt, jnp.zeros_like(qt)], axis=0)
    lane_q = lax.broadcasted_iota(jnp.int32, (1, hq), 1) % QB

    pad = 16
    s_ref[0:pad, :] = jnp.zeros((pad, hq), F32)
    s_ref[pad + nc:pad + nc + pad, :] = jnp.zeros((pad, hq), F32)
    s_ref[pad:pad + nc, :] = _dot(kvc_ref[...], qts)
    w0 = pl.multiple_of(qb * 8, 8)
    s_ref[pl.ds(w0, 32), :] = s_ref[pl.ds(w0, 32), :] + bc_ref[...]
    sc = s_ref[pad:pad + nc, :]
    cidx = lax.broadcasted_iota(jnp.int32, (nc, hq), 0)
    tq = qb * QB + lane_q
    last_c = jnp.minimum(lax.shift_right_arithmetic(tq - (CMP_LEN - 1), 4), nc - 2)
    sc = jnp.where(cidx <= last_c, sc, NEG_INF)
    mc = jnp.max(sc, axis=0, keepdims=True)
    pc = jnp.exp(sc - mc)
    lc = jnp.sum(pc, axis=0, keepdims=True)
    pc = pc * jnp.where(last_c >= 0, 1.0 / lc, 0.0)
    oc_ref[...] = _dot(kvct_ref[NSA_HD:2 * NSA_HD, :], pc.astype(BF16))

    kl = lax.broadcasted_iota(jnp.int32, (QB, hq), 0)
    causal = kl <= lane_q
    anti = kl > lane_q
    n_w = WINDOW // QB
    kts = [jnp.maximum(qb - back, 0) for back in range(n_w + 1)]
    live = [qb >= back for back in range(n_w + 1)]

    def krows(ref, kt):
        return ref[pl.ds(pl.multiple_of(kt * QB, QB), QB), :]

    s_win = [_dot(krows(kw_ref, kts[back]), qts) for back in range(n_w + 1)]
    s_win[0] = jnp.where(causal, s_win[0] + bd_ref[...], NEG_INF)
    s_win[1] = jnp.where(live[1], s_win[1] + bd1_ref[...], NEG_INF)
    for back in range(2, n_w):
        s_win[back] = jnp.where(live[back], s_win[back], NEG_INF)
    s_win[n_w] = jnp.where(anti & live[n_w], s_win[n_w], NEG_INF)
    m_w = jnp.max(s_win[0], axis=0, keepdims=True)
    for s in s_win[1:]:
        m_w = jnp.maximum(m_w, jnp.max(s, axis=0, keepdims=True))
    acc_w = jnp.zeros((VT_ROWS, hq), F32)
    for back, s in enumerate(s_win):
        acc_w = acc_w + _dot(vt_ref[kts[back], VT_ROWS:2 * VT_ROWS, :], jnp.exp(s - m_w).astype(BF16))
    ow_ref[...] = acc_w[0:NSA_HD, :] * (1.0 / acc_w[NSA_HD:NSA_HD + 1, :])

    psum = pc[:, 0:QB]
    for h in range(1, NSA_HEADS):
        psum = psum + pc[:, h * QB:(h + 1) * QB]
    p_hi = psum.astype(BF16)
    p_lo = (psum - p_hi.astype(F32)).astype(BF16)
    imp = _dot(ovl_ref[...], p_hi) + _dot(ovl_ref[...], p_lo)
    jidx = lax.broadcasted_iota(jnp.int32, (ns, QB), 0)
    iq = lax.broadcasted_iota(jnp.int32, (ns, QB), 1)
    jc = 2 * qb + (iq >= SLC_LEN).astype(jnp.int32)
    forced = (jidx == 0) | (jidx == jc) | (jidx == jc - 1)
    imp = jnp.where(forced, FORCE_SCORE, imp)
    imp = jnp.where(jidx <= jc, imp, NEG_INF)
    key = pltpu.bitcast(imp, jnp.int32)
    key_ref[...] = key
    tie_ref[...] = jnp.zeros((ns, QB), jnp.int32)
    sub = lax.broadcasted_iota(jnp.int32, (8, QB), 0)
    n_vreg = ns // 8

    def rank_body(g, cnts):
        base = pl.multiple_of(g * 8, 8)
        rows = key_ref[pl.ds(base, 8), :]
        out = []
        for tv in range(n_vreg):
            target = key[tv * 8:(tv + 1) * 8, :] - jnp.where(tv > g, 1, 0)
            c = cnts[tv]
            for r in range(8):
                c = c + (rows[r:r + 1, :] > target).astype(jnp.int32)
            out.append(c)
        ties = jnp.zeros((8, QB), jnp.int32)
        for r in range(8):
            ties = ties + ((rows[r:r + 1, :] == rows) & (sub > r)).astype(jnp.int32)
        tie_ref[pl.ds(base, 8), :] = ties
        return tuple(out)

    n_groups = jnp.minimum((2 * qb + 9) // 8, n_vreg)
    cnts = lax.fori_loop(0, n_groups, rank_body, tuple(jnp.zeros((8, QB), jnp.int32) for _ in range(n_vreg)))
    cnt = jnp.concatenate(cnts, axis=0) + tie_ref[...]
    chosen = (cnt < n_top) & (imp > 0.5 * NEG_INF)
    sel_ref[...] = jnp.where(chosen, 1.0, 0.0)
    mb_ref[...] = jnp.where(chosen & (jidx < 2 * (qb - 1)), 0.0, NEG_INF)

    m_ref[...] = jnp.full((1, hq), NEG_INF, F32)
    acc_ref[...] = jnp.zeros((VT_ROWS, hq), F32)
    n_far = (jnp.maximum(qb - 1, 0) + 3) // 4
    zpad = jnp.zeros((QB - NSA_HD - 2 * FAR_BLOCKS, hq), BF16)

    def q_aug(t):
        rows = mb_ref[pl.ds(pl.multiple_of(t * FAR_BLOCKS, FAR_BLOCKS), FAR_BLOCKS), :]
        rows = jnp.concatenate([rows] * NSA_HEADS, axis=1)
        rows = jnp.concatenate([rows, jnp.zeros_like(rows)], axis=0).astype(BF16)
        return jnp.concatenate([qt, rows, zpad], axis=0)

    def produce(t, s_x, cm_x):
        s = _dot(ks_ref[pl.ds(pl.multiple_of(t * FAR_TILE, FAR_TILE), FAR_TILE), :], q_aug(t))
        s_x[...] = s
        cm_x[...] = jnp.max(s, axis=0, keepdims=True)

    def soften(s_x, cm_x):
        m_old = m_ref[...]
        m_new = jnp.maximum(m_old, cm_x[...])
        m_ref[...] = m_new
        return jnp.exp(s_x[...] - m_new).astype(BF16), jnp.exp(m_old - m_new)

    def accumulate(t, p, alpha):
        sub = FAR_TILE // QB
        vt = jnp.concatenate([vt_ref[sub * t + j, 0:VT_ROWS, :] for j in range(sub)], axis=1)
        acc_ref[...] = alpha * acc_ref[...] + _dot(vt, p)

    def consume(t, s_x, cm_x):
        accumulate(t, *soften(s_x, cm_x))

    produce(0, sa_ref, cma_ref)

    def pair_body(j, carry):
        produce(2 * j + 1, sb_ref, cmb_ref)
        consume(2 * j, sa_ref, cma_ref)
        produce(jnp.minimum(2 * j + 2, n_far - 1), sa_ref, cma_ref)
        consume(2 * j + 1, sb_ref, cmb_ref)
        return carry

    lax.fori_loop(0, n_far // 2, pair_body, 0)

    @pl.when(n_far % 2 == 1)
    def _():
        consume(n_far - 1, sa_ref, cma_ref)

    def sel_mask(kt):
        m0 = jnp.broadcast_to(sel_ref[pl.ds(2 * kt, 1), :], (SLC_LEN, QB))
        m1 = jnp.broadcast_to(sel_ref[pl.ds(2 * kt + 1, 1), :], (SLC_LEN, QB))
        mt = jnp.concatenate([m0, m1], axis=0)
        return jnp.concatenate([mt] * NSA_HEADS, axis=1) > 0.5

    s_sel = [_dot(krows(ks_ref, qb), qts) + bd_ref[...], _dot(krows(ks_ref, kts[1]), qts) + bd1_ref[...]]
    s_sel[0] = jnp.where(sel_mask(qb) & causal, s_sel[0], NEG_INF)
    s_sel[1] = jnp.where(sel_mask(kts[1]) & live[1], s_sel[1], NEG_INF)
    m_old = m_ref[...]
    m_new = m_old
    for s in s_sel:
        m_new = jnp.maximum(m_new, jnp.max(s, axis=0, keepdims=True))
    acc = jnp.exp(m_old - m_new) * acc_ref[...]
    for back, s in enumerate(s_sel):
        acc = acc + _dot(vt_ref[kts[back], 0:VT_ROWS, :], jnp.exp(s - m_new).astype(BF16))
    osel = acc[0:NSA_HD, :] * (1.0 / acc[NSA_HD:NSA_HD + 1, :])

    gates = jax.nn.sigmoid(gt_ref[...])
    oc = oc_ref[...]
    ow = ow_ref[...]
    parts = []
    for h in range(NSA_HEADS):
        sl = slice(h * QB, (h + 1) * QB)
        parts.append(gates[3 * h:3 * h + 1, :] * oc[:, sl] + gates[3 * h + 1:3 * h + 2, :] * osel[:, sl]
                     + gates[3 * h + 2:3 * h + 3, :] * ow[:, sl])
    y_ref[...] = jnp.concatenate(parts, axis=0).T.astype(BF16)


def _nsa(kvc, kvct, ks, kw, vt, qt, gt, ovl_t, bd, bd1, bc, *, batch, seq):
    nc = seq // CMP_STRIDE
    ns = seq // SLC_LEN
    nqb = seq // QB
    hq = NSA_HEADS * QB
    ks3 = ks.reshape(batch, seq, 128)
    kw3 = kw.reshape(batch, seq, 128)
    vt4 = vt.reshape(batch, nqb, 2 * VT_ROWS, LANES)
    return pl.pallas_call(
        functools.partial(_nsa_kernel, nc=nc, ns=ns, n_top=min(N_SELECT, ns)),
        grid=(batch, nqb),
        in_specs=[pl.BlockSpec((None, nc, 128), lambda b, q: (b, 0, 0)),
                  pl.BlockSpec((None, 128, nc), lambda b, q: (b, 0, 0)),
                  pl.BlockSpec((None, seq, 128), lambda b, q: (b, 0, 0)),
                  pl.BlockSpec((None, seq, 128), lambda b, q: (b, 0, 0)),
                  pl.BlockSpec((None, nqb, 2 * VT_ROWS, LANES), lambda b, q: (b, 0, 0, 0)),
                  pl.BlockSpec((256, QB), lambda b, q: (0, b * nqb + q)),
                  pl.BlockSpec((16, QB), lambda b, q: (0, b * nqb + q)),
                  _const_spec(ovl_t.shape), _const_spec(bd.shape), _const_spec(bd1.shape),
                  _const_spec(bc.shape)],
        out_specs=pl.BlockSpec((QB, 256), lambda b, q: (b * nqb + q, 0)),
        out_shape=jax.ShapeDtypeStruct((batch * seq, 256), BF16),
        scratch_shapes=[pltpu.VMEM((nc + 32, hq), F32),
                        pltpu.VMEM((ns, QB), jnp.int32),
                        pltpu.VMEM((ns, QB), jnp.int32),
                        pltpu.VMEM((ns, QB), F32),
                        pltpu.VMEM((ns, QB), F32),
                        pltpu.VMEM((1, hq), F32),
                        pltpu.VMEM((VT_ROWS, hq), F32),
                        pltpu.VMEM((NSA_HD, hq), F32),
                        pltpu.VMEM((NSA_HD, hq), F32),
                        pltpu.VMEM((FAR_TILE, hq), F32), pltpu.VMEM((FAR_TILE, hq), F32),
                        pltpu.VMEM((1, hq), F32), pltpu.VMEM((1, hq), F32)],
        compiler_params=_cparams(("parallel", "parallel")),
        name="nsa_attention",
    )(kvc, kvct, ks3, kw3, vt4, qt, gt, ovl_t, bd, bd1, bc)


def _mla_prep_kernel(u_ref, qg_ref, kg_ref, wqt_ref, wqpt_ref, wk_ref, wvt_ref, cosp_ref, sinp_ref,
                     cost_ref, sint_ref, k_ref, qt_ref, vt_ref):
    scale = (MLA_NOPE + MLA_ROPE) ** -0.5
    cqn = _rms(u_ref[:, 0:Q_LORA], qg_ref[...]).astype(BF16)
    ckvn = _rms(u_ref[:, Q_LORA:Q_LORA + KV_LORA], kg_ref[...]).astype(BF16)
    qta = _dot_nt(wqt_ref[...], cqn)
    qtp = _dot_nt(wqpt_ref[...], cqn)
    cos_t = cost_ref[...]
    sin_t = sint_ref[...]
    for h in range(MLA_HEADS):
        r0 = h * 128
        qt_ref[r0:r0 + MLA_NOPE, :] = (qta[r0:r0 + MLA_NOPE, :] * scale).astype(BF16)
        rope = qta[r0 + MLA_NOPE:r0 + MLA_NOPE + MLA_ROPE, :] * cos_t + qtp[h * MLA_ROPE:(h + 1) * MLA_ROPE, :] * sin_t
        qt_ref[r0 + MLA_NOPE:r0 + MLA_NOPE + MLA_ROPE, :] = (rope * scale).astype(BF16)
        qt_ref[r0 + MLA_NOPE + MLA_ROPE:r0 + 128, :] = jnp.zeros((128 - MLA_NOPE - MLA_ROPE, qta.shape[1]), BF16)
    krope = u_ref[:, 384:512] * cosp_ref[...] + u_ref[:, 512:640] * sinp_ref[...]
    knope = _dot(ckvn, wk_ref[...])
    for h in range(MLA_HEADS):
        k_ref[:, h * 128:(h + 1) * 128] = (knope[:, h * 128:(h + 1) * 128] + krope).astype(BF16)
    vt = _dot_nt(wvt_ref[...], ckvn).astype(BF16)
    for h in range(MLA_HEADS):
        vt_ref[h * VT_ROWS:h * VT_ROWS + MLA_V, :] = vt[h * MLA_V:(h + 1) * MLA_V, :]
        vt_ref[h * VT_ROWS + MLA_V:(h + 1) * VT_ROWS, :] = jnp.ones((VT_ROWS - MLA_V, vt.shape[1]), BF16)


def _mla_prep(mla_in, qg, kg, wqt, wqpt, wk, wvt, cosp, sinp, cost, sint, *, batch, seq, tile=MLA_TILE):
    u3 = mla_in.reshape(batch, seq, 640)
    nt = seq // tile
    return pl.pallas_call(
        _mla_prep_kernel,
        grid=(batch, nt),
        in_specs=[pl.BlockSpec((None, tile, 640), lambda b, t: (b, t, 0)),
                  _const_spec((1, Q_LORA)), _const_spec((1, KV_LORA)),
                  _const_spec(wqt.shape), _const_spec(wqpt.shape), _const_spec(wk.shape), _const_spec(wvt.shape),
                  pl.BlockSpec((tile, 128), lambda b, t: (t, 0)), pl.BlockSpec((tile, 128), lambda b, t: (t, 0)),
                  pl.BlockSpec((MLA_ROPE, tile), lambda b, t: (0, t)), pl.BlockSpec((MLA_ROPE, tile), lambda b, t: (0, t))],
        out_specs=(pl.BlockSpec((None, tile, 512), lambda b, t: (b, t, 0)),
                   pl.BlockSpec((None, 512, tile), lambda b, t: (b, 0, t)),
                   pl.BlockSpec((None, None, MLA_HEADS * VT_ROWS, tile), lambda b, t: (b, t, 0, 0))),
        out_shape=(jax.ShapeDtypeStruct((batch, seq, 512), BF16),
                   jax.ShapeDtypeStruct((batch, 512, seq), BF16),
                   jax.ShapeDtypeStruct((batch, nt, MLA_HEADS * VT_ROWS, tile), BF16)),
        compiler_params=_cparams(("parallel", "parallel")),
        name="mla_prep",
    )(u3, qg, kg, wqt, wqpt, wk, wvt, cosp, sinp, cost, sint)


def _mla_attn_kernel(k_ref, vt_ref, qt_ref, o_ref, m_ref, acc_ref, sa_ref, sb_ref, cma_ref, cmb_ref, *, tile):
    qi = pl.program_id(1)
    heads = range(MLA_HEADS)
    m_ref[...] = jnp.full(m_ref.shape, NEG_INF, F32)
    acc_ref[...] = jnp.zeros(acc_ref.shape, F32)

    def scores(t, h):
        hs = slice(h * 128, (h + 1) * 128)
        return _dot(k_ref[pl.ds(pl.multiple_of(t * tile, tile), tile), hs], qt_ref[hs, :])

    def produce(t, s_x, cm_x):
        for h in heads:
            s = scores(t, h)
            s_x[h] = s
            cm_x[h:h + 1, :] = jnp.max(s, axis=0, keepdims=True)

    def consume(t, tiles, maxima):
        probs, alphas = [], []
        for h in heads:
            m_old = m_ref[h:h + 1, :]
            m_new = jnp.maximum(m_old, maxima[h])
            probs.append(jnp.exp(tiles[h] - m_new).astype(BF16))
            alphas.append(jnp.exp(m_old - m_new))
            m_ref[h:h + 1, :] = m_new
        for h in heads:
            acc_ref[h] = alphas[h] * acc_ref[h] + _dot(vt_ref[t, h * VT_ROWS:(h + 1) * VT_ROWS, :], probs[h])

    def consume_from(t, s_x, cm_x):
        consume(t, [s_x[h] for h in heads], [cm_x[h:h + 1, :] for h in heads])

    @pl.when(qi > 0)
    def _():
        produce(0, sa_ref, cma_ref)

    def pair_body(j, carry):
        produce(2 * j + 1, sb_ref, cmb_ref)
        consume_from(2 * j, sa_ref, cma_ref)
        produce(jnp.minimum(2 * j + 2, qi - 1), sa_ref, cma_ref)
        consume_from(2 * j + 1, sb_ref, cmb_ref)
        return carry

    lax.fori_loop(0, qi // 2, pair_body, 0)

    @pl.when(qi % 2 == 1)
    def _():
        consume_from(qi - 1, sa_ref, cma_ref)

    kl = lax.broadcasted_iota(jnp.int32, (tile, tile), 0)
    iq = lax.broadcasted_iota(jnp.int32, (tile, tile), 1)
    diag = [jnp.where(kl <= iq, scores(qi, h), NEG_INF) for h in heads]
    consume(qi, diag, [jnp.max(s, axis=0, keepdims=True) for s in diag])
    outs = [acc_ref[h, 0:MLA_V, :] * (1.0 / acc_ref[h, MLA_V:MLA_V + 1, :]) for h in heads]
    o_ref[...] = jnp.concatenate(outs, axis=0).T.astype(BF16)


def _mla_attn(k, qt, vt, *, batch, seq, tile=MLA_TILE):
    nt = seq // tile
    y = pl.pallas_call(
        functools.partial(_mla_attn_kernel, tile=tile),
        grid=(batch, nt),
        in_specs=[pl.BlockSpec((None, seq, 512), lambda b, q: (b, 0, 0)),
                  pl.BlockSpec((None, nt, MLA_HEADS * VT_ROWS, tile), lambda b, q: (b, 0, 0, 0)),
                  pl.BlockSpec((None, 512, tile), lambda b, q: (b, 0, q))],
        out_specs=pl.BlockSpec((None, tile, 256), lambda b, q: (b, q, 0)),
        out_shape=jax.ShapeDtypeStruct((batch, seq, 256), BF16),
        scratch_shapes=[pltpu.VMEM((8, tile), F32),
                        pltpu.VMEM((MLA_HEADS, VT_ROWS, tile), F32),
                        pltpu.VMEM((MLA_HEADS, tile, tile), F32), pltpu.VMEM((MLA_HEADS, tile, tile), F32),
                        pltpu.VMEM((8, tile), F32), pltpu.VMEM((8, tile), F32)],
        compiler_params=_cparams(("parallel", "parallel")),
        name="mla_attention",
    )(k, vt, qt)
    return y.reshape(batch * seq, 256)


def _mlp_kernel(h_ref, yc_ref, yn_ref, ym_ref, yp_ref, wo_ref, g_ref, w1_ref, w2_ref, fg_ref, o_ref, *, final, chunk):
    y = jnp.concatenate([yc_ref[...], yn_ref[...], ym_ref[...], yp_ref[...]], axis=1)
    h2 = h_ref[...] + _dot(y, wo_ref[...])
    xn = _rms(h2, g_ref[...]).astype(BF16)
    acc = h2
    for c in range(D_FF // chunk):
        z = _dot(xn, w1_ref[:, c * chunk:(c + 1) * chunk])
        a = jnp.square(jnp.maximum(z, 0.0)).astype(BF16)
        acc = acc + _dot(a, w2_ref[c * chunk:(c + 1) * chunk, :])
    if final:
        acc = _rms(acc, fg_ref[...])
    o_ref[...] = acc


def _out_mlp(h, yc, yn, ym, yp, wo, g, w1, w2, fg, *, final, tile=512, chunk=1024):
    n = h.shape[0]
    row = lambda w: pl.BlockSpec((tile, w), lambda i: (i, 0))
    once = lambda shape: pl.BlockSpec(shape, lambda i: (0,) * len(shape), pipeline_mode=pl.Buffered(1))
    return pl.pallas_call(
        functools.partial(_mlp_kernel, final=final, chunk=chunk),
        grid=(n // tile,),
        in_specs=[row(D_MODEL), row(256), row(256), row(256), row(256),
                  once(wo.shape), _const_spec((1, D_MODEL)), once(w1.shape), once(w2.shape),
                  _const_spec((1, D_MODEL))],
        out_specs=row(D_MODEL),
        out_shape=jax.ShapeDtypeStruct((n, D_MODEL), F32),
        compiler_params=_cparams(("parallel",)),
        name="out_mlp",
    )(h, yc, yn, ym, yp, wo, g, w1, w2, fg)


def _bucket_np(n):
    n = np.maximum(n, 0)
    max_exact = N_BUCKETS // 2
    nf = np.maximum(n, 1).astype(np.float32)
    large = max_exact + (np.log(nf / np.float32(max_exact)) / np.float32(math.log(MAX_DIST / max_exact))
                         * np.float32(N_BUCKETS - max_exact)).astype(np.int32)
    large = np.minimum(large, N_BUCKETS - 1)
    return np.where(n < max_exact, n, large).astype(np.int32)


def _bias_tiles(rel_table):
    shifted = rel_table - rel_table[N_BUCKETS - 1][None, :]
    kl = np.arange(QB)[:, None]
    iq = np.arange(QB)[None, :]

    def tile(dist):
        onehot = (_bucket_np(dist)[..., None] == np.arange(N_BUCKETS)).astype(np.float32)
        t = jnp.einsum("kib,bh->khi", jnp.asarray(onehot), shifted, precision=lax.Precision.HIGHEST)
        return t.reshape(dist.shape[0], NSA_HEADS * QB)

    bd = tile(iq - kl)
    bd1 = tile(iq + QB - kl)
    cc = np.arange(32)[:, None]
    bc = tile(iq - CMP_STRIDE * (cc - 16) - (CMP_LEN - 1))
    return bd, bd1, bc


def _overlap_t(seq):
    nc = seq // CMP_STRIDE
    ns = seq // SLC_LEN
    c_lo = np.arange(nc)[None, :] * CMP_STRIDE
    s_lo = np.arange(ns)[:, None] * SLC_LEN
    ov = np.clip(np.minimum(c_lo + CMP_LEN, s_lo + SLC_LEN) - np.maximum(c_lo, s_lo), 0, None) / CMP_LEN
    ov[:, nc - 1] = 0.0
    return jnp.asarray(ov, BF16)


def _rope_tables(seq):
    pos = jnp.arange(seq, dtype=F32)
    inv_freq = ROPE_THETA ** (-jnp.arange(0, MLA_ROPE, 2, dtype=F32) / MLA_ROPE)
    ang = pos[:, None] * inv_freq[None, :]
    cos = jnp.concatenate([jnp.cos(ang)] * 2, axis=1)
    sin = jnp.concatenate([jnp.sin(ang)] * 2, axis=1)
    place = lambda a: jnp.pad(a, ((0, 0), (MLA_NOPE, 128 - MLA_NOPE - MLA_ROPE)))
    return place(cos), place(sin), cos.T, sin.T


def _rot_cols(w):
    half = MLA_ROPE // 2
    return jnp.concatenate([-w[..., half:], w[..., :half]], axis=-1)


def _layer_weights(w_in, conv_dw, cmp_pos, cmp_w1, cmp_w2, w_uq, w_ukv, pool_w):
    kv = lambda i: w_in[:, _C_KV + 64 * i:_C_KV + 64 * (i + 1)]
    kr = w_in[:, _C_KR:_C_KR + MLA_ROPE]
    place = lambda a: jnp.pad(a, ((0, 0), (MLA_NOPE, 128 - MLA_NOPE - MLA_ROPE)))
    wrow = jnp.concatenate([
        w_in[:, _C_CONV:_C_CONV + 512], w_in[:, _C_POOL:_C_POOL + 256], kv(0), kv(1),
        jnp.pad(kv(2), ((0, 0), (0, 64))), jnp.pad(kv(4), ((0, 0), (0, 64))),
        w_in[:, _C_CQ:_C_CQ + Q_LORA], w_in[:, _C_CKV:_C_CKV + KV_LORA], place(kr), place(_rot_cols(kr))],
        axis=1).astype(BF16)
    wt = jnp.concatenate([
        w_in[:, _C_Q:_C_Q + 256].T * (NSA_HD ** -0.5), kv(3).T, kv(5).T,
        jnp.pad(w_in[:, _C_G:_C_G + 12].T, ((0, 4), (0, 0)))], axis=0).astype(BF16)

    dw = jnp.pad(conv_dw, ((0, 1), (0, 0)))

    w1r = cmp_w1.reshape(2, CMP_LEN, NSA_HD, 128)
    zk = jnp.zeros_like(w1r[0])
    w1kv = jnp.concatenate([jnp.concatenate([w1r[0], zk], axis=-1),
                            jnp.concatenate([zk, w1r[1]], axis=-1)], axis=1)
    w1a = w1kv[:16].reshape(16 * 128, 256).astype(BF16)
    w1b = w1kv[16:].reshape(16 * 128, 256).astype(BF16)
    pos_kv = jnp.concatenate([cmp_pos[0], cmp_pos[1]], axis=-1)
    posa = pos_kv[:16].reshape(1, 16 * 128)
    posb = pos_kv[16:].reshape(1, 16 * 128)
    z2 = jnp.zeros_like(cmp_w2[0])
    w2bd = jnp.concatenate([jnp.concatenate([cmp_w2[0], z2], axis=1),
                            jnp.concatenate([z2, cmp_w2[1]], axis=1)], axis=0).astype(BF16)

    wq = w_uq.reshape(Q_LORA, MLA_HEADS, MLA_NOPE + MLA_ROPE)
    wqt = jnp.pad(wq, ((0, 0), (0, 0), (0, 128 - MLA_NOPE - MLA_ROPE))).reshape(Q_LORA, 512).T.astype(BF16)
    wqpt = _rot_cols(wq[:, :, MLA_NOPE:]).reshape(Q_LORA, MLA_HEADS * MLA_ROPE).T.astype(BF16)
    wkv = w_ukv.reshape(KV_LORA, MLA_HEADS, MLA_NOPE + MLA_V)
    wk = jnp.pad(wkv[:, :, :MLA_NOPE], ((0, 0), (0, 0), (0, 128 - MLA_NOPE))).reshape(KV_LORA, 512).astype(BF16)
    wvt = wkv[:, :, MLA_NOPE:].reshape(KV_LORA, MLA_HEADS * MLA_V).T.astype(BF16)

    poolw = jax.scipy.linalg.block_diag(*[pool_w[g] for g in range(len(POOL_WINDOWS))]).astype(BF16)
    return dict(wrow=wrow, wt=wt, dw=dw, w1a=w1a, w1b=w1b, posa=posa, posb=posb, w2bd=w2bd,
                wqt=wqt, wqpt=wqpt, wk=wk, wvt=wvt, poolw=poolw)


def kernel(x, w_in, w_out, ln_mix_g, ln_mlp_g, conv_dw, conv_dw_b, conv_ln_g, conv_ln_b, conv_pw,
           nsa_cmp_pos, nsa_cmp_w1, nsa_cmp_w2, mla_q_norm_g, mla_w_uq, mla_kv_norm_g, mla_w_ukv,
           pool_w, pool_scale, mlp_w1, mlp_w2, rel_bias_table, final_norm_g):
    batch, seq, _ = x.shape
    depth = w_in.shape[0]
    bd, bd1, bc = _bias_tiles(rel_bias_table)
    ovl_t = _overlap_t(seq)
    cosp, sinp, cost, sint = _rope_tables(seq)
    r1 = lambda v: v.reshape(1, -1)
    h = x.reshape(batch * seq, D_MODEL)
    for l in range(depth):
        w = _layer_weights(w_in[l], conv_dw[l], nsa_cmp_pos[l], nsa_cmp_w1[l], nsa_cmp_w2[l],
                           mla_w_uq[l], mla_w_ukv[l], pool_w[l])
        uc, up, kcvc, ks, kw, mla_in, qt, vt, gt = _proj(h, r1(ln_mix_g[l]), w["wrow"], w["wt"])
        yc, yp = _local(uc, up, w["dw"], r1(conv_dw_b[l]), r1(conv_ln_g[l]), r1(conv_ln_b[l]),
                        conv_pw[l].astype(BF16), w["poolw"], r1(pool_scale[l]), batch=batch, seq=seq)
        kvc, kvct = _compress(kcvc, w["posa"], w["posb"], w["w1a"], w["w1b"], w["w2bd"], batch=batch, seq=seq)
        yn = _nsa(kvc, kvct, ks, kw, vt, qt, gt, ovl_t, bd, bd1, bc, batch=batch, seq=seq)
        km, qtm, vtm = _mla_prep(mla_in, r1(mla_q_norm_g[l]), r1(mla_kv_norm_g[l]), w["wqt"], w["wqpt"],
                                 w["wk"], w["wvt"], cosp, sinp, cost, sint, batch=batch, seq=seq)
        ym = _mla_attn(km, qtm, vtm, batch=batch, seq=seq)
        h = _out_mlp(h, yc, yn, ym, yp, w_out[l].astype(BF16), r1(ln_mlp_g[l]), mlp_w1[l].astype(BF16),
                     mlp_w2[l].astype(BF16), r1(final_norm_g), final=(l == depth - 1))
    return h.reshape(batch, seq, D_MODEL)
```

```python
import functools
import math

import numpy as np
import jax
import jax.numpy as jnp
from jax import lax
from jax.experimental import pallas as pl
from jax.experimental.pallas import tpu as pltpu

F32 = jnp.float32
BF16 = jnp.bfloat16

D_MODEL = 1024
GROUP_W = 256
CONV_K = 31
NSA_HEADS = 4
NSA_HD = 64
CMP_LEN = 32
CMP_STRIDE = 16
SLC_LEN = 64
N_SELECT = 16
WINDOW = 512
MLA_HEADS = 4
MLA_NOPE = 64
MLA_ROPE = 32
MLA_V = 64
Q_LORA = 256
KV_LORA = 128
ROPE_THETA = 10000.0
POOL_WINDOWS = (2, 4, 8, 16)
D_FF = 4 * D_MODEL
N_BUCKETS = 32
MAX_DIST = 128
EPS = 1e-6
NEG_INF = -1e30
FORCE_SCORE = 1e9
LOG2E = math.log2(math.e)

LANES = 128
SUBLANES = 8
QB = 128
HALO = 32
FAR_TILE = 512
FAR_BLOCKS = FAR_TILE // SLC_LEN
MASK_ROWS = 16
VT_ROWS = 80
MLA_TILE = 512
VMEM_LIMIT = 56 * 1024 * 1024

_C_CONV, _C_Q, _C_KV, _C_G, _C_CQ, _C_CKV, _C_KR, _C_POOL = 0, 512, 768, 1152, 1164, 1420, 1548, 1580


def _cparams(sem):
    return pltpu.CompilerParams(dimension_semantics=sem, vmem_limit_bytes=VMEM_LIMIT)


def _rms(x, g):
    ms = jnp.mean(x * x, axis=-1, keepdims=True)
    return x * lax.rsqrt(ms + EPS) * g


def _dot(a, b):
    return jnp.dot(a, b, preferred_element_type=F32)


def _dot_nt(a, b):
    return lax.dot_general(a, b, (((1,), (1,)), ((), ())), preferred_element_type=F32)


def _const_spec(shape):
    nd = len(shape)
    return pl.BlockSpec(shape, lambda *_: (0,) * nd)


def _proj_kernel(x_ref, g_ref, wrow_ref, wt_ref, conv_ref, pool_ref, kcvc_ref, ks_ref, kw_ref, mla_ref,
                 qt_ref, vt_ref, gt_ref, *, tile, sub):
    ones = jnp.ones((VT_ROWS - NSA_HD, LANES), BF16)
    lane = lax.broadcasted_iota(jnp.int32, (sub, LANES), 1)
    for r in range(tile // sub):
        rows = slice(r * sub, (r + 1) * sub)
        xn = _rms(x_ref[rows, :], g_ref[...]).astype(BF16)
        conv_ref[rows, :] = _dot(xn, wrow_ref[:, 0:512])
        pool_ref[rows, :] = _dot(xn, wrow_ref[:, 512:768])
        kk = _dot(xn, wrow_ref[:, 768:1024])
        kcvc_ref[rows, :] = kk[:, 0:LANES]
        tok = pl.program_id(0) * tile + r * sub + lax.broadcasted_iota(jnp.int32, (sub, LANES), 0)
        onehot = (lane - NSA_HD == (tok // SLC_LEN) % FAR_BLOCKS).astype(F32)
        ks_ref[rows, :] = (kk[:, LANES:2 * LANES] + onehot).astype(BF16)
        km = _dot(xn, wrow_ref[:, 1024:1792])
        kw_ref[rows, :] = km[:, 0:LANES].astype(BF16)
        mla_ref[rows, :] = km[:, LANES:]
        qt_ref[:, rows] = _dot_nt(wt_ref[0:256, :], xn).astype(BF16)
        vs = _dot_nt(wt_ref[256:384, :], xn).astype(BF16)
        for j in range(sub // LANES):
            cols = slice(j * LANES, (j + 1) * LANES)
            jt = r * (sub // LANES) + j
            vt_ref[jt, 0:NSA_HD, :] = vs[0:NSA_HD, cols]
            vt_ref[jt, NSA_HD:VT_ROWS, :] = ones
            vt_ref[jt, VT_ROWS:VT_ROWS + NSA_HD, :] = vs[NSA_HD:2 * NSA_HD, cols]
            vt_ref[jt, VT_ROWS + NSA_HD:2 * VT_ROWS, :] = ones
        gt_ref[:, rows] = _dot_nt(wt_ref[384:400, :], xn)


def _proj(h, g, wrow, wt, *, tile=1024, sub=512):
    n = h.shape[0]
    grid = (n // tile,)
    outs = (
        jax.ShapeDtypeStruct((n, 512), F32),
        jax.ShapeDtypeStruct((n, 256), F32),
        jax.ShapeDtypeStruct((n, 128), F32),
        jax.ShapeDtypeStruct((n, 128), BF16),
        jax.ShapeDtypeStruct((n, 128), BF16),
        jax.ShapeDtypeStruct((n, 640), F32),
        jax.ShapeDtypeStruct((256, n), BF16),
        jax.ShapeDtypeStruct((n // LANES, 2 * VT_ROWS, LANES), BF16),
        jax.ShapeDtypeStruct((16, n), F32),
    )
    row = lambda w: pl.BlockSpec((tile, w), lambda i: (i, 0))
    return pl.pallas_call(
        functools.partial(_proj_kernel, tile=tile, sub=sub),
        grid=grid,
        in_specs=[row(D_MODEL), _const_spec((1, D_MODEL)), _const_spec(wrow.shape), _const_spec(wt.shape)],
        out_specs=(row(512), row(256), row(128), row(128), row(128), row(640),
                   pl.BlockSpec((256, tile), lambda i: (0, i)),
                   pl.BlockSpec((tile // LANES, 2 * VT_ROWS, LANES), lambda i: (i, 0, 0)),
                   pl.BlockSpec((16, tile), lambda i: (0, i))),
        out_shape=outs,
        compiler_params=_cparams(("parallel",)),
        name="in_proj",
    )(h, g, wrow, wt)


def _local_kernel(uc_ref, uch_ref, up_ref, uph_ref, dw_ref, dwb_ref, lng_ref, lnb_ref, pw_ref,
                  poolw_ref, pscale_ref, yc_ref, yp_ref, hbuf, ubuf, sa, sb, hshift, *, tile, rows):
    t = pl.program_id(1)
    keep = jnp.where(t > 0, 1.0, 0.0).astype(F32)

    def glu(u):
        return u[:, 0:GROUP_W] * jax.nn.sigmoid(u[:, GROUP_W:2 * GROUP_W])

    hbuf[0:HALO, :] = glu(uch_ref[...]) * keep
    hbuf[HALO:HALO + tile, :] = glu(uc_ref[...])
    span = HALO + tile - SUBLANES
    for j in range(1, SUBLANES):
        hshift[j - 1, 0:span, :] = hbuf[pl.ds(j, span), :]
    dwb = dwb_ref[...]
    for c in range(tile // rows):
        acc = jnp.zeros((rows, GROUP_W), F32) + dwb
        for k in range(CONV_K):
            off = c * rows + HALO - (CONV_K - 1) + k
            phase, base = off % SUBLANES, off - off % SUBLANES
            tap = hbuf[pl.ds(base, rows), :] if phase == 0 else hshift[phase - 1, pl.ds(base, rows), :]
            acc = acc + dw_ref[k:k + 1, :] * tap
        mu = jnp.mean(acc, axis=-1, keepdims=True)
        cen = acc - mu
        var = jnp.mean(cen * cen, axis=-1, keepdims=True)
        y = cen * lax.rsqrt(var + EPS) * lng_ref[...] + lnb_ref[...]
        y = y * jax.nn.sigmoid(y)
        yc_ref[c * rows:(c + 1) * rows, :] = _dot(y.astype(BF16), pw_ref[...]).astype(BF16)

    ext = HALO + tile
    ubuf[0:HALO, :] = uph_ref[...] * keep
    ubuf[HALO:ext, :] = up_ref[...]
    sa[pl.ds(8, ext - 8), :] = ubuf[pl.ds(8, ext - 8), :] + ubuf[pl.ds(7, ext - 8), :]
    sb[pl.ds(16, ext - 16), :] = sa[pl.ds(16, ext - 16), :] + sa[pl.ds(14, ext - 16), :]
    s2 = sa[pl.ds(HALO, tile), :]
    s4 = sb[pl.ds(HALO, tile), :]
    sa[pl.ds(24, ext - 24), :] = sb[pl.ds(24, ext - 24), :] + sb[pl.ds(20, ext - 24), :]
    s8 = sa[pl.ds(HALO, tile), :]
    s16 = s8 + sa[pl.ds(HALO - 8, tile), :]
    lane = lax.broadcasted_iota(jnp.int32, (tile, GROUP_W), 1)
    grp = lane // (GROUP_W // len(POOL_WINDOWS))
    wsum = jnp.where(grp == 0, s2, jnp.where(grp == 1, s4, jnp.where(grp == 2, s8, s16)))
    width = jnp.where(grp == 0, 2, jnp.where(grp == 1, 4, jnp.where(grp == 2, 8, 16)))
    pos = t * tile + lax.broadcasted_iota(jnp.int32, (tile, GROUP_W), 0)
    cnt = jnp.minimum(width, pos + 1).astype(F32)
    d = wsum / cnt - up_ref[...]
    yp_ref[...] = (_dot(d.astype(BF16), poolw_ref[...]) * pscale_ref[...]).astype(BF16)


def _local(uc, up, dw, dwb, lng, lnb, pw, poolw, pscale, *, batch, seq, tile=512, rows=128):
    uc3 = uc.reshape(batch, seq, 512)
    up3 = up.reshape(batch, seq, GROUP_W)
    hpt = tile // HALO
    cur = lambda w: pl.BlockSpec((None, tile, w), lambda b, t: (b, t, 0))
    halo = lambda w: pl.BlockSpec((None, HALO, w), lambda b, t: (b, jnp.maximum(t * hpt - 1, 0), 0))
    yc, yp = pl.pallas_call(
        functools.partial(_local_kernel, tile=tile, rows=rows),
        grid=(batch, seq // tile),
        in_specs=[cur(512), halo(512), cur(GROUP_W), halo(GROUP_W),
                  _const_spec(dw.shape), _const_spec((1, GROUP_W)), _const_spec((1, GROUP_W)),
                  _const_spec((1, GROUP_W)), _const_spec((GROUP_W, GROUP_W)),
                  _const_spec((GROUP_W, GROUP_W)), _const_spec((1, GROUP_W))],
        out_specs=(cur(GROUP_W), cur(GROUP_W)),
        out_shape=(jax.ShapeDtypeStruct((batch, seq, GROUP_W), BF16),
                   jax.ShapeDtypeStruct((batch, seq, GROUP_W), BF16)),
        scratch_shapes=[pltpu.VMEM((HALO + tile, GROUP_W), F32)] * 4
        + [pltpu.VMEM((SUBLANES - 1, HALO + tile, GROUP_W), F32)],
        compiler_params=_cparams(("parallel", "parallel")),
        name="local_mixers",
    )(uc3, uc3, up3, up3, dw, dwb, lng, lnb, pw, poolw, pscale)
    return yc.reshape(batch * seq, GROUP_W), yp.reshape(batch * seq, GROUP_W)


def _compress_kernel(kr_ref, posa_ref, posb_ref, w1a_ref, w1b_ref, w2_ref, kv_ref, kvt_ref, bbuf, *, nc):
    kr = kr_ref[...]
    a = _dot((kr + posa_ref[...]).astype(BF16), w1a_ref[...])
    b = _dot((kr + posb_ref[...]).astype(BF16), w1b_ref[...])
    bbuf[0:nc, :] = b
    bbuf[nc:nc + 8, :] = jnp.zeros((8, 256), F32)
    hdn = a + bbuf[pl.ds(1, nc), :]
    hdn = hdn * jax.nn.sigmoid(hdn)
    kv = _dot(hdn.astype(BF16), w2_ref[...])
    kv_ref[...] = kv.astype(BF16)
    kvt_ref[...] = kv.T.astype(BF16)


def _compress(kcvc, posa, posb, w1a, w1b, w2bd, *, batch, seq):
    nc = seq // CMP_STRIDE
    kr = kcvc.reshape(batch, nc, CMP_STRIDE * 128)
    return pl.pallas_call(
        functools.partial(_compress_kernel, nc=nc),
        grid=(batch,),
        in_specs=[pl.BlockSpec((None, nc, CMP_STRIDE * 128), lambda b: (b, 0, 0)),
                  _const_spec(posa.shape), _const_spec(posb.shape), _const_spec(w1a.shape),
                  _const_spec(w1b.shape), _const_spec(w2bd.shape)],
        out_specs=(pl.BlockSpec((None, nc, 128), lambda b: (b, 0, 0)),
                   pl.BlockSpec((None, 128, nc), lambda b: (b, 0, 0))),
        out_shape=(jax.ShapeDtypeStruct((batch, nc, 128), BF16),
                   jax.ShapeDtypeStruct((batch, 128, nc), BF16)),
        scratch_shapes=[pltpu.VMEM((nc + 8, 256), F32)],
        compiler_params=_cparams(("parallel",)),
        name="nsa_compress",
    )(kr, posa, posb, w1a, w1b, w2bd)


def _nsa_kernel(kvc_ref, kvct_ref, ks_ref, kw_ref, vt_ref, qt_ref, gt_ref, ovl_ref, bd_ref, bd1_ref, bc_ref,
                y_ref, s_ref, key_ref, tie_ref, sel_ref, mb_ref, m_ref, acc_ref, mn_ref, accn_ref, oc_ref, ow_ref, sa_ref, sb_ref, cma_ref, cmb_ref,
                *, nc, ns, n_top):
    qb = pl.program_id(1)
    hq = NSA_HEADS * QB
    qt = jnp.concatenate([qt_ref[h * NSA_HD:(h + 1) * NSA_HD, :] for h in range(NSA_HEADS)], axis=1)
    qts = jnp.concatenate([qt, jnp.zeros_like(qt)], axis=0)
    lane_q = lax.broadcasted_iota(jnp.int32, (1, hq), 1) % QB

    tq = qb * QB + lane_q
    pad = 16
    sub8 = lax.broadcasted_iota(jnp.int32, (8, QB), 0)

    def select(nrows, nblk):
        s_ref[0:pad, :] = jnp.zeros((pad, hq), F32)
        s_ref[pad + nrows:pad + nrows + pad, :] = jnp.zeros((pad, hq), F32)
        s_ref[pad:pad + nrows, :] = _dot(kvc_ref[0:nrows, :], qts)
        w0 = pl.multiple_of(qb * 8, 8)
        s_ref[pl.ds(w0, 32), :] = s_ref[pl.ds(w0, 32), :] + bc_ref[...]
        sc = s_ref[pad:pad + nrows, :]
        cidx = lax.broadcasted_iota(jnp.int32, (nrows, hq), 0)
        last_c = jnp.minimum(lax.shift_right_arithmetic(tq - (CMP_LEN - 1), 4), nc - 2)
        sc = jnp.where(cidx <= last_c, sc, NEG_INF)
        mc = jnp.max(sc, axis=0, keepdims=True)
        pc = jnp.exp(sc - mc)
        lc = jnp.sum(pc, axis=0, keepdims=True)
        pc = pc * jnp.where(last_c >= 0, 1.0 / lc, 0.0)
        oc_ref[...] = _dot(kvct_ref[NSA_HD:2 * NSA_HD, 0:nrows], pc.astype(BF16))

        psum = pc[:, 0:QB]
        for h in range(1, NSA_HEADS):
            psum = psum + pc[:, h * QB:(h + 1) * QB]
        p_hi = psum.astype(BF16)
        p_lo = (psum - p_hi.astype(F32)).astype(BF16)
        ovl = ovl_ref[0:nblk, 0:nrows]
        imp = _dot(ovl, p_hi) + _dot(ovl, p_lo)
        jidx = lax.broadcasted_iota(jnp.int32, (nblk, QB), 0)
        iq = lax.broadcasted_iota(jnp.int32, (nblk, QB), 1)
        jc = 2 * qb + (iq >= SLC_LEN).astype(jnp.int32)
        forced = (jidx == 0) | (jidx == jc) | (jidx == jc - 1)
        imp = jnp.where(forced, FORCE_SCORE, imp)
        imp = jnp.where(jidx <= jc, imp, NEG_INF)
        key = pltpu.bitcast(imp, jnp.int32)
        key_ref[0:nblk, :] = key
        tie_ref[0:nblk, :] = jnp.zeros((nblk, QB), jnp.int32)
        n_vreg = nblk // 8

        def rank_body(g, cnts):
            base = pl.multiple_of(g * 8, 8)
            rows = key_ref[pl.ds(base, 8), :]
            out = []
            for tv in range(n_vreg):
                target = key[tv * 8:(tv + 1) * 8, :] - jnp.where(tv > g, 1, 0)
                c = cnts[tv]
                for r in range(8):
                    c = c + (rows[r:r + 1, :] > target).astype(jnp.int32)
                out.append(c)
            ties = jnp.zeros((8, QB), jnp.int32)
            for r in range(8):
                ties = ties + ((rows[r:r + 1, :] == rows) & (sub8 > r)).astype(jnp.int32)
            tie_ref[pl.ds(base, 8), :] = ties
            return tuple(out)

        n_groups = jnp.minimum((2 * qb + 9) // 8, n_vreg)
        cnts = lax.fori_loop(0, n_groups, rank_body, tuple(jnp.zeros((8, QB), jnp.int32) for _ in range(n_vreg)))
        cnt = jnp.concatenate(cnts, axis=0) + tie_ref[0:nblk, :]
        chosen = (cnt < n_top) & (imp > 0.5 * NEG_INF)
        sel_ref[0:nblk, :] = jnp.where(chosen, 1.0, 0.0)
        mb_ref[0:nblk, :] = jnp.where(chosen & (jidx < 2 * (qb - 1)), 0.0, NEG_INF)
        if nblk < ns:
            sel_ref[nblk:ns, :] = jnp.zeros((ns - nblk, QB), F32)
            mb_ref[nblk:ns, :] = jnp.full((ns - nblk, QB), NEG_INF, F32)

    if (nc // 2) % LANES == 0 and (ns // 2) % SUBLANES == 0:
        first_half = qb < nc // (2 * (QB // CMP_STRIDE))
        pl.when(first_half)(lambda: select(nc // 2, ns // 2))
        pl.when(jnp.logical_not(first_half))(lambda: select(nc, ns))
    else:
        select(nc, ns)

    m_ref[...] = jnp.full((1, hq), NEG_INF, F32)
    acc_ref[...] = jnp.zeros((VT_ROWS, hq), F32)
    per_far = FAR_TILE // QB
    n_far = (jnp.maximum(qb - 1, 0) + per_far - 1) // per_far
    zpad = jnp.zeros((QB - NSA_HD - MASK_ROWS, hq), BF16)

    def q_aug(t):
        rows = mb_ref[pl.ds(pl.multiple_of(t * FAR_BLOCKS, FAR_BLOCKS), FAR_BLOCKS), :]
        rows = jnp.concatenate([rows] * NSA_HEADS, axis=1)
        rows = jnp.concatenate([rows, jnp.zeros((MASK_ROWS - FAR_BLOCKS, hq), F32)], axis=0).astype(BF16)
        return jnp.concatenate([qt, rows, zpad], axis=0)

    def produce(t, s_x, cm_x):
        s = _dot(ks_ref[pl.ds(pl.multiple_of(t * FAR_TILE, FAR_TILE), FAR_TILE), :], q_aug(t))
        s_x[...] = s
        cm_x[...] = jnp.max(s, axis=0, keepdims=True)

    def soften(s_x, cm_x):
        m_old = m_ref[...]
        m_new = jnp.maximum(m_old, cm_x[...])
        m_ref[...] = m_new
        return jnp.exp(s_x[...] - m_new).astype(BF16), jnp.exp(m_old - m_new)

    def accumulate(t, p, alpha):
        sub = FAR_TILE // QB
        vt = jnp.concatenate([vt_ref[sub * t + j, 0:VT_ROWS, :] for j in range(sub)], axis=1)
        acc_ref[...] = alpha * acc_ref[...] + _dot(vt, p)

    def consume(t, s_x, cm_x):
        accumulate(t, *soften(s_x, cm_x))

    produce(0, sa_ref, cma_ref)

    kl = lax.broadcasted_iota(jnp.int32, (QB, hq), 0)
    causal = kl <= lane_q
    anti = kl > lane_q
    n_w = WINDOW // QB
    kts = [jnp.maximum(qb - back, 0) for back in range(n_w + 1)]
    live = [qb >= back for back in range(n_w + 1)]

    def krows(ref, kt):
        return ref[pl.ds(pl.multiple_of(kt * QB, QB), QB), :]

    def sel_mask(kt):
        m0 = jnp.broadcast_to(sel_ref[pl.ds(2 * kt, 1), :], (SLC_LEN, QB))
        m1 = jnp.broadcast_to(sel_ref[pl.ds(2 * kt + 1, 1), :], (SLC_LEN, QB))
        mt = jnp.concatenate([m0, m1], axis=0)
        return jnp.concatenate([mt] * NSA_HEADS, axis=1) > 0.5

    def softmax_pv(tiles, row0):
        m = jnp.max(tiles[0], axis=0, keepdims=True)
        for s in tiles[1:]:
            m = jnp.maximum(m, jnp.max(s, axis=0, keepdims=True))
        acc = jnp.zeros((VT_ROWS, hq), F32)
        for back, s in enumerate(tiles):
            acc = acc + _dot(vt_ref[kts[back], row0:row0 + VT_ROWS, :], jnp.exp(s - m).astype(BF16))
        return m, acc

    s_win = [_dot(krows(kw_ref, kts[back]), qts) for back in range(n_w + 1)]
    s_sel = [_dot(krows(ks_ref, kts[back]), qts) for back in range(2)]
    s_win[0] = jnp.where(causal, s_win[0] + bd_ref[...], NEG_INF)
    s_win[1] = jnp.where(live[1], s_win[1] + bd1_ref[...], NEG_INF)
    for back in range(2, n_w):
        s_win[back] = jnp.where(live[back], s_win[back], NEG_INF)
    s_win[n_w] = jnp.where(anti & live[n_w], s_win[n_w], NEG_INF)
    s_sel[0] = jnp.where(sel_mask(qb) & causal, s_sel[0] + bd_ref[...], NEG_INF)
    s_sel[1] = jnp.where(sel_mask(kts[1]) & live[1], s_sel[1] + bd1_ref[...], NEG_INF)
    _, acc_w = softmax_pv(s_win, VT_ROWS)
    ow_ref[...] = acc_w[0:NSA_HD, :] * (1.0 / acc_w[NSA_HD:NSA_HD + 1, :])
    mn_ref[...], accn_ref[...] = softmax_pv(s_sel, 0)

    def pair_body(j, carry):
        produce(2 * j + 1, sb_ref, cmb_ref)
        consume(2 * j, sa_ref, cma_ref)
        produce(jnp.minimum(2 * j + 2, n_far - 1), sa_ref, cma_ref)
        consume(2 * j + 1, sb_ref, cmb_ref)
        return carry

    lax.fori_loop(0, n_far // 2, pair_body, 0)

    @pl.when(n_far % 2 == 1)
    def _():
        consume(n_far - 1, sa_ref, cma_ref)

    m_far = m_ref[...]
    m_near = mn_ref[...]
    m_all = jnp.maximum(m_far, m_near)
    acc = jnp.exp(m_far - m_all) * acc_ref[...] + jnp.exp(m_near - m_all) * accn_ref[...]
    osel = acc[0:NSA_HD, :] * (1.0 / acc[NSA_HD:NSA_HD + 1, :])

    gates = jax.nn.sigmoid(gt_ref[...])
    oc = oc_ref[...]
    ow = ow_ref[...]
    parts = []
    for h in range(NSA_HEADS):
        sl = slice(h * QB, (h + 1) * QB)
        parts.append(gates[3 * h:3 * h + 1, :] * oc[:, sl] + gates[3 * h + 1:3 * h + 2, :] * osel[:, sl]
                     + gates[3 * h + 2:3 * h + 3, :] * ow[:, sl])
    y_ref[...] = jnp.concatenate(parts, axis=0).T.astype(BF16)


def _nsa(kvc, kvct, ks, kw, vt, qt, gt, ovl_t, bd, bd1, bc, *, batch, seq):
    nc = seq // CMP_STRIDE
    ns = seq // SLC_LEN
    nqb = seq // QB
    hq = NSA_HEADS * QB
    ks3 = ks.reshape(batch, seq, 128)
    kw3 = kw.reshape(batch, seq, 128)
    vt4 = vt.reshape(batch, nqb, 2 * VT_ROWS, LANES)
    return pl.pallas_call(
        functools.partial(_nsa_kernel, nc=nc, ns=ns, n_top=min(N_SELECT, ns)),
        grid=(batch, nqb),
        in_specs=[pl.BlockSpec((None, nc, 128), lambda b, q: (b, 0, 0)),
                  pl.BlockSpec((None, 128, nc), lambda b, q: (b, 0, 0)),
                  pl.BlockSpec((None, seq, 128), lambda b, q: (b, 0, 0)),
                  pl.BlockSpec((None, seq, 128), lambda b, q: (b, 0, 0)),
                  pl.BlockSpec((None, nqb, 2 * VT_ROWS, LANES), lambda b, q: (b, 0, 0, 0)),
                  pl.BlockSpec((256, QB), lambda b, q: (0, b * nqb + q)),
                  pl.BlockSpec((16, QB), lambda b, q: (0, b * nqb + q)),
                  _const_spec(ovl_t.shape), _const_spec(bd.shape), _const_spec(bd1.shape),
                  _const_spec(bc.shape)],
        out_specs=pl.BlockSpec((QB, 256), lambda b, q: (b * nqb + q, 0)),
        out_shape=jax.ShapeDtypeStruct((batch * seq, 256), BF16),
        scratch_shapes=[pltpu.VMEM((nc + 32, hq), F32),
                        pltpu.VMEM((ns, QB), jnp.int32),
                        pltpu.VMEM((ns, QB), jnp.int32),
                        pltpu.VMEM((ns, QB), F32),
                        pltpu.VMEM((ns, QB), F32),
                        pltpu.VMEM((1, hq), F32),
                        pltpu.VMEM((VT_ROWS, hq), F32),
                        pltpu.VMEM((1, hq), F32),
                        pltpu.VMEM((VT_ROWS, hq), F32),
                        pltpu.VMEM((NSA_HD, hq), F32),
                        pltpu.VMEM((NSA_HD, hq), F32),
                        *[pltpu.VMEM((FAR_TILE, hq), F32)] * 2,
                        *[pltpu.VMEM((1, hq), F32)] * 2],
        compiler_params=_cparams(("parallel", "parallel")),
        name="nsa_attention",
    )(kvc, kvct, ks3, kw3, vt4, qt, gt, ovl_t, bd, bd1, bc)


def _mla_prep_kernel(u_ref, qg_ref, kg_ref, wqt_ref, wqpt_ref, wk_ref, wvt_ref, cosp_ref, sinp_ref,
                     cost_ref, sint_ref, k_ref, qt_ref, vt_ref):
    scale = (MLA_NOPE + MLA_ROPE) ** -0.5 * LOG2E
    cqn = _rms(u_ref[:, 0:Q_LORA], qg_ref[...]).astype(BF16)
    ckvn = _rms(u_ref[:, Q_LORA:Q_LORA + KV_LORA], kg_ref[...]).astype(BF16)
    qta = _dot_nt(wqt_ref[...], cqn)
    qtp = _dot_nt(wqpt_ref[...], cqn)
    cos_t = cost_ref[...]
    sin_t = sint_ref[...]
    for h in range(MLA_HEADS):
        r0 = h * 128
        qt_ref[r0:r0 + MLA_NOPE, :] = (qta[r0:r0 + MLA_NOPE, :] * scale).astype(BF16)
        rope = qta[r0 + MLA_NOPE:r0 + MLA_NOPE + MLA_ROPE, :] * cos_t + qtp[h * MLA_ROPE:(h + 1) * MLA_ROPE, :] * sin_t
        qt_ref[r0 + MLA_NOPE:r0 + MLA_NOPE + MLA_ROPE, :] = (rope * scale).astype(BF16)
        qt_ref[r0 + MLA_NOPE + MLA_ROPE:r0 + 128, :] = jnp.zeros((128 - MLA_NOPE - MLA_ROPE, qta.shape[1]), BF16)
    krope = u_ref[:, 384:512] * cosp_ref[...] + u_ref[:, 512:640] * sinp_ref[...]
    knope = _dot(ckvn, wk_ref[...])
    for h in range(MLA_HEADS):
        k_ref[:, h * 128:(h + 1) * 128] = (knope[:, h * 128:(h + 1) * 128] + krope).astype(BF16)
    vt = _dot_nt(wvt_ref[...], ckvn).astype(BF16)
    for h in range(MLA_HEADS):
        vt_ref[h * VT_ROWS:h * VT_ROWS + MLA_V, :] = vt[h * MLA_V:(h + 1) * MLA_V, :]
        vt_ref[h * VT_ROWS + MLA_V:(h + 1) * VT_ROWS, :] = jnp.ones((VT_ROWS - MLA_V, vt.shape[1]), BF16)


def _mla_prep(mla_in, qg, kg, wqt, wqpt, wk, wvt, cosp, sinp, cost, sint, *, batch, seq, tile=MLA_TILE):
    u3 = mla_in.reshape(batch, seq, 640)
    nt = seq // tile
    return pl.pallas_call(
        _mla_prep_kernel,
        grid=(batch, nt),
        in_specs=[pl.BlockSpec((None, tile, 640), lambda b, t: (b, t, 0)),
                  _const_spec((1, Q_LORA)), _const_spec((1, KV_LORA)),
                  _const_spec(wqt.shape), _const_spec(wqpt.shape), _const_spec(wk.shape), _const_spec(wvt.shape),
                  pl.BlockSpec((tile, 128), lambda b, t: (t, 0)), pl.BlockSpec((tile, 128), lambda b, t: (t, 0)),
                  pl.BlockSpec((MLA_ROPE, tile), lambda b, t: (0, t)), pl.BlockSpec((MLA_ROPE, tile), lambda b, t: (0, t))],
        out_specs=(pl.BlockSpec((None, tile, 512), lambda b, t: (b, t, 0)),
                   pl.BlockSpec((None, 512, tile), lambda b, t: (b, 0, t)),
                   pl.BlockSpec((None, None, MLA_HEADS * VT_ROWS, tile), lambda b, t: (b, t, 0, 0))),
        out_shape=(jax.ShapeDtypeStruct((batch, seq, 512), BF16),
                   jax.ShapeDtypeStruct((batch, 512, seq), BF16),
                   jax.ShapeDtypeStruct((batch, nt, MLA_HEADS * VT_ROWS, tile), BF16)),
        compiler_params=_cparams(("parallel", "parallel")),
        name="mla_prep",
    )(u3, qg, kg, wqt, wqpt, wk, wvt, cosp, sinp, cost, sint)


def _mla_attn_kernel(k_ref, vt_ref, qt_ref, o_ref, m_ref, acc_ref, sa_ref, sb_ref, cma_ref, cmb_ref, *, tile):
    qi = pl.program_id(1)
    heads = range(MLA_HEADS)
    m_ref[...] = jnp.full(m_ref.shape, NEG_INF, F32)
    acc_ref[...] = jnp.zeros(acc_ref.shape, F32)

    def scores(t, h):
        hs = slice(h * 128, (h + 1) * 128)
        return _dot(k_ref[pl.ds(pl.multiple_of(t * tile, tile), tile), hs], qt_ref[hs, :])

    def produce(t, s_x, cm_x):
        for h in heads:
            s = scores(t, h)
            s_x[h] = s
            cm_x[h:h + 1, :] = jnp.max(s, axis=0, keepdims=True)

    def consume(t, tiles, maxima):
        probs, alphas = [], []
        for h in heads:
            m_old = m_ref[h:h + 1, :]
            m_new = jnp.maximum(m_old, maxima[h])
            probs.append(jnp.exp2(tiles[h] - m_new).astype(BF16))
            alphas.append(jnp.exp2(m_old - m_new))
            m_ref[h:h + 1, :] = m_new
        for h in heads:
            acc_ref[h] = alphas[h] * acc_ref[h] + _dot(vt_ref[t, h * VT_ROWS:(h + 1) * VT_ROWS, :], probs[h])

    def consume_from(t, s_x, cm_x):
        consume(t, [s_x[h] for h in heads], [cm_x[h:h + 1, :] for h in heads])

    @pl.when(qi > 0)
    def _():
        produce(0, sa_ref, cma_ref)

    def pair_body(j, carry):
        produce(2 * j + 1, sb_ref, cmb_ref)
        consume_from(2 * j, sa_ref, cma_ref)
        produce(jnp.minimum(2 * j + 2, qi - 1), sa_ref, cma_ref)
        consume_from(2 * j + 1, sb_ref, cmb_ref)
        return carry

    lax.fori_loop(0, qi // 2, pair_body, 0)

    @pl.when(qi % 2 == 1)
    def _():
        consume_from(qi - 1, sa_ref, cma_ref)

    kl = lax.broadcasted_iota(jnp.int32, (tile, tile), 0)
    iq = lax.broadcasted_iota(jnp.int32, (tile, tile), 1)
    diag = [jnp.where(kl <= iq, scores(qi, h), NEG_INF) for h in heads]
    consume(qi, diag, [jnp.max(s, axis=0, keepdims=True) for s in diag])
    outs = [acc_ref[h, 0:MLA_V, :] * (1.0 / acc_ref[h, MLA_V:MLA_V + 1, :]) for h in heads]
    o_ref[...] = jnp.concatenate(outs, axis=0).T.astype(BF16)


def _mla_attn(k, qt, vt, *, batch, seq, tile=MLA_TILE):
    nt = seq // tile
    y = pl.pallas_call(
        functools.partial(_mla_attn_kernel, tile=tile),
        grid=(batch, nt),
        in_specs=[pl.BlockSpec((None, seq, 512), lambda b, q: (b, 0, 0)),
                  pl.BlockSpec((None, nt, MLA_HEADS * VT_ROWS, tile), lambda b, q: (b, 0, 0, 0)),
                  pl.BlockSpec((None, 512, tile), lambda b, q: (b, 0, q))],
        out_specs=pl.BlockSpec((None, tile, 256), lambda b, q: (b, q, 0)),
        out_shape=jax.ShapeDtypeStruct((batch, seq, 256), BF16),
        scratch_shapes=[pltpu.VMEM((8, tile), F32),
                        pltpu.VMEM((MLA_HEADS, VT_ROWS, tile), F32),
                        *[pltpu.VMEM((MLA_HEADS, tile, tile), F32)] * 2,
                        *[pltpu.VMEM((8, tile), F32)] * 2],
        compiler_params=_cparams(("parallel", "parallel")),
        name="mla_attention",
    )(k, vt, qt)
    return y.reshape(batch * seq, 256)


def _mlp_kernel(h_ref, yc_ref, yn_ref, ym_ref, yp_ref, wo_ref, g_ref, w1_ref, w2_ref, fg_ref, o_ref, *, final, chunk):
    y = jnp.concatenate([yc_ref[...], yn_ref[...], ym_ref[...], yp_ref[...]], axis=1)
    h2 = h_ref[...] + _dot(y, wo_ref[...])
    xn = _rms(h2, g_ref[...]).astype(BF16)
    acc = h2
    for c in range(D_FF // chunk):
        z = _dot(xn, w1_ref[:, c * chunk:(c + 1) * chunk])
        a = jnp.square(jnp.maximum(z, 0.0)).astype(BF16)
        acc = acc + _dot(a, w2_ref[c * chunk:(c + 1) * chunk, :])
    if final:
        acc = _rms(acc, fg_ref[...])
    o_ref[...] = acc


def _out_mlp(h, yc, yn, ym, yp, wo, g, w1, w2, fg, *, final, tile=512, chunk=1024):
    n = h.shape[0]
    row = lambda w: pl.BlockSpec((tile, w), lambda i: (i, 0))
    once = lambda shape: pl.BlockSpec(shape, lambda i: (0,) * len(shape), pipeline_mode=pl.Buffered(1))
    return pl.pallas_call(
        functools.partial(_mlp_kernel, final=final, chunk=chunk),
        grid=(n // tile,),
        in_specs=[row(D_MODEL), row(256), row(256), row(256), row(256),
                  once(wo.shape), _const_spec((1, D_MODEL)), once(w1.shape), once(w2.shape),
                  _const_spec((1, D_MODEL))],
        out_specs=row(D_MODEL),
        out_shape=jax.ShapeDtypeStruct((n, D_MODEL), F32),
        compiler_params=_cparams(("parallel",)),
        name="out_mlp",
    )(h, yc, yn, ym, yp, wo, g, w1, w2, fg)


def _bucket_np(n):
    n = np.maximum(n, 0)
    max_exact = N_BUCKETS // 2
    nf = np.maximum(n, 1).astype(np.float32)
    large = max_exact + (np.log(nf / np.float32(max_exact)) / np.float32(math.log(MAX_DIST / max_exact))
                         * np.float32(N_BUCKETS - max_exact)).astype(np.int32)
    large = np.minimum(large, N_BUCKETS - 1)
    return np.where(n < max_exact, n, large).astype(np.int32)


def _bias_tiles(rel_table):
    shifted = rel_table - rel_table[N_BUCKETS - 1][None, :]
    kl = np.arange(QB)[:, None]
    iq = np.arange(QB)[None, :]

    def tile(dist):
        onehot = (_bucket_np(dist)[..., None] == np.arange(N_BUCKETS)).astype(np.float32)
        t = jnp.einsum("kib,bh->khi", jnp.asarray(onehot), shifted, precision=lax.Precision.HIGHEST)
        return t.reshape(dist.shape[0], NSA_HEADS * QB)

    bd = tile(iq - kl)
    bd1 = tile(iq + QB - kl)
    cc = np.arange(32)[:, None]
    bc = tile(iq - CMP_STRIDE * (cc - 16) - (CMP_LEN - 1))
    return bd, bd1, bc


def _overlap_t(seq):
    nc = seq // CMP_STRIDE
    ns = seq // SLC_LEN
    c_lo = np.arange(nc)[None, :] * CMP_STRIDE
    s_lo = np.arange(ns)[:, None] * SLC_LEN
    ov = np.clip(np.minimum(c_lo + CMP_LEN, s_lo + SLC_LEN) - np.maximum(c_lo, s_lo), 0, None) / CMP_LEN
    ov[:, nc - 1] = 0.0
    return jnp.asarray(ov, BF16)


def _rope_tables(seq):
    pos = jnp.arange(seq, dtype=F32)
    inv_freq = ROPE_THETA ** (-jnp.arange(0, MLA_ROPE, 2, dtype=F32) / MLA_ROPE)
    ang = pos[:, None] * inv_freq[None, :]
    cos = jnp.concatenate([jnp.cos(ang)] * 2, axis=1)
    sin = jnp.concatenate([jnp.sin(ang)] * 2, axis=1)
    place = lambda a: jnp.pad(a, ((0, 0), (MLA_NOPE, 128 - MLA_NOPE - MLA_ROPE)))
    return place(cos), place(sin), cos.T, sin.T


def _rot_cols(w):
    half = MLA_ROPE // 2
    return jnp.concatenate([-w[..., half:], w[..., :half]], axis=-1)


def _layer_weights(w_in, conv_dw, cmp_pos, cmp_w1, cmp_w2, w_uq, w_ukv, pool_w):
    kv = lambda i: w_in[:, _C_KV + 64 * i:_C_KV + 64 * (i + 1)]
    kr = w_in[:, _C_KR:_C_KR + MLA_ROPE]
    place = lambda a: jnp.pad(a, ((0, 0), (MLA_NOPE, 128 - MLA_NOPE - MLA_ROPE)))
    wrow = jnp.concatenate([
        w_in[:, _C_CONV:_C_CONV + 512], w_in[:, _C_POOL:_C_POOL + 256], kv(0), kv(1),
        jnp.pad(kv(2), ((0, 0), (0, 64))), jnp.pad(kv(4), ((0, 0), (0, 64))),
        w_in[:, _C_CQ:_C_CQ + Q_LORA], w_in[:, _C_CKV:_C_CKV + KV_LORA], place(kr), place(_rot_cols(kr))],
        axis=1).astype(BF16)
    wt = jnp.concatenate([
        w_in[:, _C_Q:_C_Q + 256].T * (NSA_HD ** -0.5), kv(3).T, kv(5).T,
        jnp.pad(w_in[:, _C_G:_C_G + 12].T, ((0, 4), (0, 0)))], axis=0).astype(BF16)

    dw = jnp.pad(conv_dw, ((0, 1), (0, 0)))

    w1r = cmp_w1.reshape(2, CMP_LEN, NSA_HD, 128)
    zk = jnp.zeros_like(w1r[0])
    w1kv = jnp.concatenate([jnp.concatenate([w1r[0], zk], axis=-1),
                            jnp.concatenate([zk, w1r[1]], axis=-1)], axis=1)
    w1a = w1kv[:16].reshape(16 * 128, 256).astype(BF16)
    w1b = w1kv[16:].reshape(16 * 128, 256).astype(BF16)
    pos_kv = jnp.concatenate([cmp_pos[0], cmp_pos[1]], axis=-1)
    posa = pos_kv[:16].reshape(1, 16 * 128)
    posb = pos_kv[16:].reshape(1, 16 * 128)
    z2 = jnp.zeros_like(cmp_w2[0])
    w2bd = jnp.concatenate([jnp.concatenate([cmp_w2[0], z2], axis=1),
                            jnp.concatenate([z2, cmp_w2[1]], axis=1)], axis=0).astype(BF16)

    wq = w_uq.reshape(Q_LORA, MLA_HEADS, MLA_NOPE + MLA_ROPE)
    wqt = jnp.pad(wq, ((0, 0), (0, 0), (0, 128 - MLA_NOPE - MLA_ROPE))).reshape(Q_LORA, 512).T.astype(BF16)
    wqpt = _rot_cols(wq[:, :, MLA_NOPE:]).reshape(Q_LORA, MLA_HEADS * MLA_ROPE).T.astype(BF16)
    wkv = w_ukv.reshape(KV_LORA, MLA_HEADS, MLA_NOPE + MLA_V)
    wk = jnp.pad(wkv[:, :, :MLA_NOPE], ((0, 0), (0, 0), (0, 128 - MLA_NOPE))).reshape(KV_LORA, 512).astype(BF16)
    wvt = wkv[:, :, MLA_NOPE:].reshape(KV_LORA, MLA_HEADS * MLA_V).T.astype(BF16)

    poolw = jax.scipy.linalg.block_diag(*[pool_w[g] for g in range(len(POOL_WINDOWS))]).astype(BF16)
    return dict(wrow=wrow, wt=wt, dw=dw, w1a=w1a, w1b=w1b, posa=posa, posb=posb, w2bd=w2bd,
                wqt=wqt, wqpt=wqpt, wk=wk, wvt=wvt, poolw=poolw)


def kernel(x, w_in, w_out, ln_mix_g, ln_mlp_g, conv_dw, conv_dw_b, conv_ln_g, conv_ln_b, conv_pw,
           nsa_cmp_pos, nsa_cmp_w1, nsa_cmp_w2, mla_q_norm_g, mla_w_uq, mla_kv_norm_g, mla_w_ukv,
           pool_w, pool_scale, mlp_w1, mlp_w2, rel_bias_table, final_norm_g):
    batch, seq, _ = x.shape
    depth = w_in.shape[0]
    bd, bd1, bc = _bias_tiles(rel_bias_table)
    ovl_t = _overlap_t(seq)
    cosp, sinp, cost, sint = _rope_tables(seq)
    r1 = lambda v: v.reshape(1, -1)
    h = x.reshape(batch * seq, D_MODEL)
    for l in range(depth):
        w = _layer_weights(w_in[l], conv_dw[l], nsa_cmp_pos[l], nsa_cmp_w1[l], nsa_cmp_w2[l],
                           mla_w_uq[l], mla_w_ukv[l], pool_w[l])
        uc, up, kcvc, ks, kw, mla_in, qt, vt, gt = _proj(h, r1(ln_mix_g[l]), w["wrow"], w["wt"])
        yc, yp = _local(uc, up, w["dw"], r1(conv_dw_b[l]), r1(conv_ln_g[l]), r1(conv_ln_b[l]),
                        conv_pw[l].astype(BF16), w["poolw"], r1(pool_scale[l]), batch=batch, seq=seq)
        kvc, kvct = _compress(kcvc, w["posa"], w["posb"], w["w1a"], w["w1b"], w["w2bd"], batch=batch, seq=seq)
        yn = _nsa(kvc, kvct, ks, kw, vt, qt, gt, ovl_t, bd, bd1, bc, batch=batch, seq=seq)
        km, qtm, vtm = _mla_prep(mla_in, r1(mla_q_norm_g[l]), r1(mla_kv_norm_g[l]), w["wqt"], w["wqpt"],
                                 w["wk"], w["wvt"], cosp, sinp, cost, sint, batch=batch, seq=seq)
        ym = _mla_attn(km, qtm, vtm, batch=batch, seq=seq)
        h = _out_mlp(h, yc, yn, ym, yp, w_out[l].astype(BF16), r1(ln_mlp_g[l]), mlp_w1[l].astype(BF16),
                     mlp_w2[l].astype(BF16), r1(final_norm_g), final=(l == depth - 1))
    return h.reshape(batch, seq, D_MODEL)
```

```python
import functools
import math

import numpy as np
import jax
import jax.numpy as jnp
from jax import lax
from jax.experimental import pallas as pl
from jax.experimental.pallas import tpu as pltpu

F32 = jnp.float32
BF16 = jnp.bfloat16

D_MODEL = 1024
GROUP_W = 256
CONV_K = 31
NSA_HEADS = 4
NSA_HD = 64
CMP_LEN = 32
CMP_STRIDE = 16
SLC_LEN = 64
N_SELECT = 16
WINDOW = 512
MLA_HEADS = 4
MLA_NOPE = 64
MLA_ROPE = 32
MLA_V = 64
Q_LORA = 256
KV_LORA = 128
ROPE_THETA = 10000.0
POOL_WINDOWS = (2, 4, 8, 16)
D_FF = 4 * D_MODEL
N_BUCKETS = 32
MAX_DIST = 128
EPS = 1e-6
NEG_INF = -1e30
FORCE_SCORE = 1e9
LOG2E = math.log2(math.e)

LANES = 128
SUBLANES = 8
QB = 128
HALO = 32
FAR_TILE = 512
FAR_BLOCKS = FAR_TILE // SLC_LEN
MASK_ROWS = 16
VT_ROWS = 80
MLA_TILE = 512
VMEM_LIMIT = 56 * 1024 * 1024

_C_CONV, _C_Q, _C_KV, _C_G, _C_CQ, _C_CKV, _C_KR, _C_POOL = 0, 512, 768, 1152, 1164, 1420, 1548, 1580


def _cparams(sem):
    return pltpu.CompilerParams(dimension_semantics=sem, vmem_limit_bytes=VMEM_LIMIT)


def _rms(x, g):
    ms = jnp.mean(x * x, axis=-1, keepdims=True)
    return x * lax.rsqrt(ms + EPS) * g


def _dot(a, b):
    return jnp.dot(a, b, preferred_element_type=F32)


def _dot_nt(a, b):
    return lax.dot_general(a, b, (((1,), (1,)), ((), ())), preferred_element_type=F32)


def _const_spec(shape):
    nd = len(shape)
    return pl.BlockSpec(shape, lambda *_: (0,) * nd)


def _proj_kernel(x_ref, g_ref, wrow_ref, wt_ref, *rest, tile, sub, steps_per_seq):
    mla = rest[:10]
    loc = rest[10:17]
    yc_ref, yp_ref, kcvc_ref, ks_ref, kw_ref, kmla_ref, qtmla_ref, vtmla_ref, qt_ref, vt_ref, gt_ref = rest[17:28]
    scratch = rest[28:]
    step = pl.program_id(0)
    seq_step = step % steps_per_seq

    @pl.when(step == 0)
    def _():
        scratch[0][sub:sub + HALO, :] = jnp.zeros((HALO, GROUP_W), F32)
        scratch[1][sub:sub + HALO, :] = jnp.zeros((HALO, GROUP_W), F32)

    ones = jnp.ones((VT_ROWS - NSA_HD, LANES), BF16)
    lane = lax.broadcasted_iota(jnp.int32, (sub, LANES), 1)
    for r in range(tile // sub):
        rows = slice(r * sub, (r + 1) * sub)
        xn = _rms(x_ref[rows, :], g_ref[...]).astype(BF16)
        first = (seq_step == 0) if r == 0 else False
        _local_tile(_dot(xn, wrow_ref[:, 0:512]), _dot(xn, wrow_ref[:, 512:768]), first,
                    seq_step * tile + r * sub, loc, rows, yc_ref, yp_ref, scratch)
        kk = _dot(xn, wrow_ref[:, 768:1024])
        kcvc_ref[rows, :] = kk[:, 0:LANES]
        tok = pl.program_id(0) * tile + r * sub + lax.broadcasted_iota(jnp.int32, (sub, LANES), 0)
        onehot = (lane - NSA_HD == (tok // SLC_LEN) % FAR_BLOCKS).astype(F32)
        ks_ref[rows, :] = (kk[:, LANES:2 * LANES] + onehot).astype(BF16)
        km = _dot(xn, wrow_ref[:, 1024:1792])
        kw_ref[rows, :] = km[:, 0:LANES].astype(BF16)
        _mla_prep_tile(km[:, LANES:], mla, rows, kmla_ref, qtmla_ref, vtmla_ref.at[r])
        qt_ref[:, rows] = _dot_nt(wt_ref[0:256, :], xn).astype(BF16)
        vs = _dot_nt(wt_ref[256:384, :], xn).astype(BF16)
        for j in range(sub // LANES):
            cols = slice(j * LANES, (j + 1) * LANES)
            jt = r * (sub // LANES) + j
            vt_ref[jt, 0:NSA_HD, :] = vs[0:NSA_HD, cols]
            vt_ref[jt, NSA_HD:VT_ROWS, :] = ones
            vt_ref[jt, VT_ROWS:VT_ROWS + NSA_HD, :] = vs[NSA_HD:2 * NSA_HD, cols]
            vt_ref[jt, VT_ROWS + NSA_HD:2 * VT_ROWS, :] = ones
        gt_ref[:, rows] = _dot_nt(wt_ref[384:400, :], xn)


def _proj(h, g, wrow, wt, mla_args, loc_args, *, batch, seq, tile=1024):
    qg, kg, wqt, wqpt, wk, wvt, cosp, sinp, cost, sint = mla_args
    dw, dwb, lng, lnb, pw, poolw, pscale = loc_args
    sub = MLA_TILE
    n = h.shape[0]
    spb = seq // tile
    mt = tile // MLA_TILE
    outs = (
        jax.ShapeDtypeStruct((n, GROUP_W), BF16),
        jax.ShapeDtypeStruct((n, GROUP_W), BF16),
        jax.ShapeDtypeStruct((n, 128), F32),
        jax.ShapeDtypeStruct((n, 128), BF16),
        jax.ShapeDtypeStruct((n, 128), BF16),
        jax.ShapeDtypeStruct((n, 512), BF16),
        jax.ShapeDtypeStruct((batch, 512, seq), BF16),
        jax.ShapeDtypeStruct((batch, seq // MLA_TILE, MLA_HEADS * VT_ROWS, MLA_TILE), BF16),
        jax.ShapeDtypeStruct((256, n), BF16),
        jax.ShapeDtypeStruct((n // LANES, 2 * VT_ROWS, LANES), BF16),
        jax.ShapeDtypeStruct((16, n), F32),
    )
    row = lambda w: pl.BlockSpec((tile, w), lambda i: (i, 0))
    return pl.pallas_call(
        functools.partial(_proj_kernel, tile=tile, sub=sub, steps_per_seq=spb),
        grid=(n // tile,),
        in_specs=[row(D_MODEL), _const_spec((1, D_MODEL)), _const_spec(wrow.shape), _const_spec(wt.shape),
                  _const_spec((1, Q_LORA)), _const_spec((1, KV_LORA)),
                  _const_spec(wqt.shape), _const_spec(wqpt.shape), _const_spec(wk.shape), _const_spec(wvt.shape),
                  pl.BlockSpec((tile, 128), lambda i: (i % spb, 0)), pl.BlockSpec((tile, 128), lambda i: (i % spb, 0)),
                  pl.BlockSpec((MLA_ROPE, tile), lambda i: (0, i % spb)),
                  pl.BlockSpec((MLA_ROPE, tile), lambda i: (0, i % spb)),
                  _const_spec(dw.shape), _const_spec((1, GROUP_W)), _const_spec((1, GROUP_W)),
                  _const_spec((1, GROUP_W)), _const_spec((GROUP_W, GROUP_W)),
                  _const_spec((GROUP_W, GROUP_W)), _const_spec((1, GROUP_W))],
        out_specs=(row(GROUP_W), row(GROUP_W), row(128), row(128), row(128), row(512),
                   pl.BlockSpec((None, 512, tile), lambda i: (i // spb, 0, i % spb)),
                   pl.BlockSpec((None, mt, MLA_HEADS * VT_ROWS, MLA_TILE), lambda i: (i // spb, i % spb, 0, 0)),
                   pl.BlockSpec((256, tile), lambda i: (0, i)),
                   pl.BlockSpec((tile // LANES, 2 * VT_ROWS, LANES), lambda i: (i, 0, 0)),
                   pl.BlockSpec((16, tile), lambda i: (0, i))),
        out_shape=outs,
        scratch_shapes=[pltpu.VMEM((HALO + sub, GROUP_W), F32)] * 4
        + [pltpu.VMEM((SUBLANES - 1, HALO + sub, GROUP_W), F32)],
        compiler_params=_cparams(("arbitrary",)),
        name="in_proj",
    )(h, g, wrow, wt, *mla_args, *loc_args)


def _local_tile(uc, up, first, pos0, loc, out_rows, yc_ref, yp_ref, scratch, *, rows=128):
    dw_ref, dwb_ref, lng_ref, lnb_ref, pw_ref, poolw_ref, pscale_ref = loc
    hbuf, ubuf, sa, sb, hshift = scratch
    tile = uc.shape[0]
    hbuf[0:HALO, :] = jnp.where(first, 0.0, hbuf[tile:tile + HALO, :])
    ubuf[0:HALO, :] = jnp.where(first, 0.0, ubuf[tile:tile + HALO, :])
    hbuf[HALO:HALO + tile, :] = uc[:, 0:GROUP_W] * jax.nn.sigmoid(uc[:, GROUP_W:2 * GROUP_W])
    span = HALO + tile - SUBLANES
    for j in range(1, SUBLANES):
        hshift[j - 1, 0:span, :] = hbuf[pl.ds(j, span), :]
    dwb = dwb_ref[...]
    for c in range(tile // rows):
        acc = jnp.zeros((rows, GROUP_W), F32) + dwb
        for k in range(CONV_K):
            off = c * rows + HALO - (CONV_K - 1) + k
            phase, base = off % SUBLANES, off - off % SUBLANES
            tap = hbuf[pl.ds(base, rows), :] if phase == 0 else hshift[phase - 1, pl.ds(base, rows), :]
            acc = acc + dw_ref[k:k + 1, :] * tap
        mu = jnp.mean(acc, axis=-1, keepdims=True)
        cen = acc - mu
        var = jnp.mean(cen * cen, axis=-1, keepdims=True)
        y = cen * lax.rsqrt(var + EPS) * lng_ref[...] + lnb_ref[...]
        y = y * jax.nn.sigmoid(y)
        r0 = out_rows.start + c * rows
        yc_ref[r0:r0 + rows, :] = _dot(y.astype(BF16), pw_ref[...]).astype(BF16)

    ext = HALO + tile
    ubuf[HALO:ext, :] = up
    sa[pl.ds(8, ext - 8), :] = ubuf[pl.ds(8, ext - 8), :] + ubuf[pl.ds(7, ext - 8), :]
    sb[pl.ds(16, ext - 16), :] = sa[pl.ds(16, ext - 16), :] + sa[pl.ds(14, ext - 16), :]
    s2 = sa[pl.ds(HALO, tile), :]
    s4 = sb[pl.ds(HALO, tile), :]
    sa[pl.ds(24, ext - 24), :] = sb[pl.ds(24, ext - 24), :] + sb[pl.ds(20, ext - 24), :]
    s8 = sa[pl.ds(HALO, tile), :]
    s16 = s8 + sa[pl.ds(HALO - 8, tile), :]
    lane = lax.broadcasted_iota(jnp.int32, (tile, GROUP_W), 1)
    grp = lane // (GROUP_W // len(POOL_WINDOWS))
    wsum = jnp.where(grp == 0, s2, jnp.where(grp == 1, s4, jnp.where(grp == 2, s8, s16)))
    width = jnp.where(grp == 0, 2, jnp.where(grp == 1, 4, jnp.where(grp == 2, 8, 16)))
    pos = pos0 + lax.broadcasted_iota(jnp.int32, (tile, GROUP_W), 0)
    cnt = jnp.minimum(width, pos + 1).astype(F32)
    d = wsum / cnt - up
    yp_ref[out_rows, :] = (_dot(d.astype(BF16), poolw_ref[...]) * pscale_ref[...]).astype(BF16)


def _compress_kernel(kr_ref, posa_ref, posb_ref, w1a_ref, w1b_ref, w2_ref, kv_ref, kvt_ref, bbuf, *, nc):
    kr = kr_ref[...]
    a = _dot((kr + posa_ref[...]).astype(BF16), w1a_ref[...])
    b = _dot((kr + posb_ref[...]).astype(BF16), w1b_ref[...])
    bbuf[0:nc, :] = b
    bbuf[nc:nc + 8, :] = jnp.zeros((8, 256), F32)
    hdn = a + bbuf[pl.ds(1, nc), :]
    hdn = hdn * jax.nn.sigmoid(hdn)
    kv = _dot(hdn.astype(BF16), w2_ref[...])
    kv_ref[...] = kv.astype(BF16)
    kvt_ref[...] = kv.T.astype(BF16)


def _compress(kcvc, posa, posb, w1a, w1b, w2bd, *, batch, seq):
    nc = seq // CMP_STRIDE
    kr = kcvc.reshape(batch, nc, CMP_STRIDE * 128)
    return pl.pallas_call(
        functools.partial(_compress_kernel, nc=nc),
        grid=(batch,),
        in_specs=[pl.BlockSpec((None, nc, CMP_STRIDE * 128), lambda b: (b, 0, 0)),
                  _const_spec(posa.shape), _const_spec(posb.shape), _const_spec(w1a.shape),
                  _const_spec(w1b.shape), _const_spec(w2bd.shape)],
        out_specs=(pl.BlockSpec((None, nc, 128), lambda b: (b, 0, 0)),
                   pl.BlockSpec((None, 128, nc), lambda b: (b, 0, 0))),
        out_shape=(jax.ShapeDtypeStruct((batch, nc, 128), BF16),
                   jax.ShapeDtypeStruct((batch, 128, nc), BF16)),
        scratch_shapes=[pltpu.VMEM((nc + 8, 256), F32)],
        compiler_params=_cparams(("parallel",)),
        name="nsa_compress",
    )(kr, posa, posb, w1a, w1b, w2bd)


def _nsa_kernel(kvc_ref, kvct_ref, ks_ref, kw_ref, vt_ref, qt_ref, gt_ref, ovl_ref, bd_ref, bd1_ref, bc_ref,
                y_ref, s_ref, key_ref, tie_ref, sel_ref, mb_ref, m_ref, acc_ref, mn_ref, accn_ref, oc_ref, ow_ref, sa_ref, sb_ref, cma_ref, cmb_ref,
                *, nc, ns, n_top):
    qb = pl.program_id(1)
    hq = NSA_HEADS * QB
    qt = jnp.concatenate([qt_ref[h * NSA_HD:(h + 1) * NSA_HD, :] for h in range(NSA_HEADS)], axis=1)
    qts = jnp.concatenate([qt, jnp.zeros_like(qt)], axis=0)
    lane_q = lax.broadcasted_iota(jnp.int32, (1, hq), 1) % QB

    tq = qb * QB + lane_q
    pad = 16
    sub8 = lax.broadcasted_iota(jnp.int32, (8, QB), 0)

    def select(nrows, nblk):
        s_ref[0:pad, :] = jnp.zeros((pad, hq), F32)
        s_ref[pad + nrows:pad + nrows + pad, :] = jnp.zeros((pad, hq), F32)
        s_ref[pad:pad + nrows, :] = _dot(kvc_ref[0:nrows, :], qts)
        w0 = pl.multiple_of(qb * 8, 8)
        s_ref[pl.ds(w0, 32), :] = s_ref[pl.ds(w0, 32), :] + bc_ref[...]
        sc = s_ref[pad:pad + nrows, :]
        cidx = lax.broadcasted_iota(jnp.int32, (nrows, hq), 0)
        last_c = jnp.minimum(lax.shift_right_arithmetic(tq - (CMP_LEN - 1), 4), nc - 2)
        sc = jnp.where(cidx <= last_c, sc, NEG_INF)
        mc = jnp.max(sc, axis=0, keepdims=True)
        pc = jnp.exp(sc - mc)
        lc = jnp.sum(pc, axis=0, keepdims=True)
        pc = pc * jnp.where(last_c >= 0, 1.0 / lc, 0.0)
        oc_ref[...] = _dot(kvct_ref[NSA_HD:2 * NSA_HD, 0:nrows], pc.astype(BF16))

        psum = pc[:, 0:QB]
        for h in range(1, NSA_HEADS):
            psum = psum + pc[:, h * QB:(h + 1) * QB]
        p_hi = psum.astype(BF16)
        p_lo = (psum - p_hi.astype(F32)).astype(BF16)
        ovl = ovl_ref[0:nblk, 0:nrows]
        imp = _dot(ovl, p_hi) + _dot(ovl, p_lo)
        jidx = lax.broadcasted_iota(jnp.int32, (nblk, QB), 0)
        iq = lax.broadcasted_iota(jnp.int32, (nblk, QB), 1)
        jc = 2 * qb + (iq >= SLC_LEN).astype(jnp.int32)
        forced = (jidx == 0) | (jidx == jc) | (jidx == jc - 1)
        imp = jnp.where(forced, FORCE_SCORE, imp)
        imp = jnp.where(jidx <= jc, imp, NEG_INF)
        key = pltpu.bitcast(imp, jnp.int32)
        key_ref[0:nblk, :] = key
        tie_ref[0:nblk, :] = jnp.zeros((nblk, QB), jnp.int32)
        n_vreg = nblk // 8

        def rank_body(g, cnts):
            base = pl.multiple_of(g * 8, 8)
            rows = key_ref[pl.ds(base, 8), :]
            out = []
            for tv in range(n_vreg):
                target = key[tv * 8:(tv + 1) * 8, :] - jnp.where(tv > g, 1, 0)
                c = cnts[tv]
                for r in range(8):
                    c = c + (rows[r:r + 1, :] > target).astype(jnp.int32)
                out.append(c)
            ties = jnp.zeros((8, QB), jnp.int32)
            for r in range(8):
                ties = ties + ((rows[r:r + 1, :] == rows) & (sub8 > r)).astype(jnp.int32)
            tie_ref[pl.ds(base, 8), :] = ties
            return tuple(out)

        n_groups = jnp.minimum((2 * qb + 9) // 8, n_vreg)
        cnts = lax.fori_loop(0, n_groups, rank_body, tuple(jnp.zeros((8, QB), jnp.int32) for _ in range(n_vreg)))
        cnt = jnp.concatenate(cnts, axis=0) + tie_ref[0:nblk, :]
        chosen = (cnt < n_top) & (imp > 0.5 * NEG_INF)
        sel_ref[0:nblk, :] = jnp.where(chosen, 1.0, 0.0)
        mb_ref[0:nblk, :] = jnp.where(chosen & (jidx < 2 * (qb - 1)), 0.0, NEG_INF)
        if nblk < ns:
            sel_ref[nblk:ns, :] = jnp.zeros((ns - nblk, QB), F32)
            mb_ref[nblk:ns, :] = jnp.full((ns - nblk, QB), NEG_INF, F32)

    if (nc // 2) % LANES == 0 and (ns // 2) % SUBLANES == 0:
        first_half = qb < nc // (2 * (QB // CMP_STRIDE))
        pl.when(first_half)(lambda: select(nc // 2, ns // 2))
        pl.when(jnp.logical_not(first_half))(lambda: select(nc, ns))
    else:
        select(nc, ns)

    m_ref[...] = jnp.full((1, hq), NEG_INF, F32)
    acc_ref[...] = jnp.zeros((VT_ROWS, hq), F32)
    per_far = FAR_TILE // QB
    n_far = (jnp.maximum(qb - 1, 0) + per_far - 1) // per_far
    zpad = jnp.zeros((QB - NSA_HD - MASK_ROWS, hq), BF16)

    def q_aug(t):
        rows = mb_ref[pl.ds(pl.multiple_of(t * FAR_BLOCKS, FAR_BLOCKS), FAR_BLOCKS), :]
        rows = jnp.concatenate([rows] * NSA_HEADS, axis=1)
        rows = jnp.concatenate([rows, jnp.zeros((MASK_ROWS - FAR_BLOCKS, hq), F32)], axis=0).astype(BF16)
        return jnp.concatenate([qt, rows, zpad], axis=0)

    def produce(t, s_x, cm_x):
        s = _dot(ks_ref[pl.ds(pl.multiple_of(t * FAR_TILE, FAR_TILE), FAR_TILE), :], q_aug(t))
        s_x[...] = s
        cm_x[...] = jnp.max(s, axis=0, keepdims=True)

    def soften(s_x, cm_x):
        m_old = m_ref[...]
        m_new = jnp.maximum(m_old, cm_x[...])
        m_ref[...] = m_new
        return jnp.exp(s_x[...] - m_new).astype(BF16), jnp.exp(m_old - m_new)

    def accumulate(t, p, alpha):
        sub = FAR_TILE // QB
        vt = jnp.concatenate([vt_ref[sub * t + j, 0:VT_ROWS, :] for j in range(sub)], axis=1)
        acc_ref[...] = alpha * acc_ref[...] + _dot(vt, p)

    def consume(t, s_x, cm_x):
        accumulate(t, *soften(s_x, cm_x))

    produce(0, sa_ref, cma_ref)

    kl = lax.broadcasted_iota(jnp.int32, (QB, hq), 0)
    causal = kl <= lane_q
    anti = kl > lane_q
    n_w = WINDOW // QB
    kts = [jnp.maximum(qb - back, 0) for back in range(n_w + 1)]
    live = [qb >= back for back in range(n_w + 1)]

    def krows(ref, kt):
        return ref[pl.ds(pl.multiple_of(kt * QB, QB), QB), :]

    def sel_mask(kt):
        m0 = jnp.broadcast_to(sel_ref[pl.ds(2 * kt, 1), :], (SLC_LEN, QB))
        m1 = jnp.broadcast_to(sel_ref[pl.ds(2 * kt + 1, 1), :], (SLC_LEN, QB))
        mt = jnp.concatenate([m0, m1], axis=0)
        return jnp.concatenate([mt] * NSA_HEADS, axis=1) > 0.5

    def softmax_pv(tiles, row0):
        m = jnp.max(tiles[0], axis=0, keepdims=True)
        for s in tiles[1:]:
            m = jnp.maximum(m, jnp.max(s, axis=0, keepdims=True))
        acc = jnp.zeros((VT_ROWS, hq), F32)
        for back, s in enumerate(tiles):
            acc = acc + _dot(vt_ref[kts[back], row0:row0 + VT_ROWS, :], jnp.exp(s - m).astype(BF16))
        return m, acc

    s_win = [_dot(krows(kw_ref, kts[back]), qts) for back in range(n_w + 1)]
    s_sel = [_dot(krows(ks_ref, kts[back]), qts) for back in range(2)]
    s_win[0] = jnp.where(causal, s_win[0] + bd_ref[...], NEG_INF)
    s_win[1] = jnp.where(live[1], s_win[1] + bd1_ref[...], NEG_INF)
    for back in range(2, n_w):
        s_win[back] = jnp.where(live[back], s_win[back], NEG_INF)
    s_win[n_w] = jnp.where(anti & live[n_w], s_win[n_w], NEG_INF)
    s_sel[0] = jnp.where(sel_mask(qb) & causal, s_sel[0] + bd_ref[...], NEG_INF)
    s_sel[1] = jnp.where(sel_mask(kts[1]) & live[1], s_sel[1] + bd1_ref[...], NEG_INF)
    _, acc_w = softmax_pv(s_win, VT_ROWS)
    ow_ref[...] = acc_w[0:NSA_HD, :] * (1.0 / acc_w[NSA_HD:NSA_HD + 1, :])
    mn_ref[...], accn_ref[...] = softmax_pv(s_sel, 0)

    def pair_body(j, carry):
        produce(2 * j + 1, sb_ref, cmb_ref)
        consume(2 * j, sa_ref, cma_ref)
        produce(jnp.minimum(2 * j + 2, n_far - 1), sa_ref, cma_ref)
        consume(2 * j + 1, sb_ref, cmb_ref)
        return carry

    lax.fori_loop(0, n_far // 2, pair_body, 0)

    @pl.when(n_far % 2 == 1)
    def _():
        consume(n_far - 1, sa_ref, cma_ref)

    m_far = m_ref[...]
    m_near = mn_ref[...]
    m_all = jnp.maximum(m_far, m_near)
    acc = jnp.exp(m_far - m_all) * acc_ref[...] + jnp.exp(m_near - m_all) * accn_ref[...]
    osel = acc[0:NSA_HD, :] * (1.0 / acc[NSA_HD:NSA_HD + 1, :])

    gates = jax.nn.sigmoid(gt_ref[...])
    oc = oc_ref[...]
    ow = ow_ref[...]
    parts = []
    for h in range(NSA_HEADS):
        sl = slice(h * QB, (h + 1) * QB)
        parts.append(gates[3 * h:3 * h + 1, :] * oc[:, sl] + gates[3 * h + 1:3 * h + 2, :] * osel[:, sl]
                     + gates[3 * h + 2:3 * h + 3, :] * ow[:, sl])
    y_ref[...] = jnp.concatenate(parts, axis=0).T.astype(BF16)


def _nsa(kvc, kvct, ks, kw, vt, qt, gt, ovl_t, bd, bd1, bc, *, batch, seq):
    nc = seq // CMP_STRIDE
    ns = seq // SLC_LEN
    nqb = seq // QB
    hq = NSA_HEADS * QB
    ks3 = ks.reshape(batch, seq, 128)
    kw3 = kw.reshape(batch, seq, 128)
    vt4 = vt.reshape(batch, nqb, 2 * VT_ROWS, LANES)
    return pl.pallas_call(
        functools.partial(_nsa_kernel, nc=nc, ns=ns, n_top=min(N_SELECT, ns)),
        grid=(batch, nqb),
        in_specs=[pl.BlockSpec((None, nc, 128), lambda b, q: (b, 0, 0)),
                  pl.BlockSpec((None, 128, nc), lambda b, q: (b, 0, 0)),
                  pl.BlockSpec((None, seq, 128), lambda b, q: (b, 0, 0)),
                  pl.BlockSpec((None, seq, 128), lambda b, q: (b, 0, 0)),
                  pl.BlockSpec((None, nqb, 2 * VT_ROWS, LANES), lambda b, q: (b, 0, 0, 0)),
                  pl.BlockSpec((256, QB), lambda b, q: (0, b * nqb + q)),
                  pl.BlockSpec((16, QB), lambda b, q: (0, b * nqb + q)),
                  _const_spec(ovl_t.shape), _const_spec(bd.shape), _const_spec(bd1.shape),
                  _const_spec(bc.shape)],
        out_specs=pl.BlockSpec((QB, 256), lambda b, q: (b * nqb + q, 0)),
        out_shape=jax.ShapeDtypeStruct((batch * seq, 256), BF16),
        scratch_shapes=[pltpu.VMEM((nc + 32, hq), F32),
                        pltpu.VMEM((ns, QB), jnp.int32),
                        pltpu.VMEM((ns, QB), jnp.int32),
                        pltpu.VMEM((ns, QB), F32),
                        pltpu.VMEM((ns, QB), F32),
                        pltpu.VMEM((1, hq), F32),
                        pltpu.VMEM((VT_ROWS, hq), F32),
                        pltpu.VMEM((1, hq), F32),
                        pltpu.VMEM((VT_ROWS, hq), F32),
                        pltpu.VMEM((NSA_HD, hq), F32),
                        pltpu.VMEM((NSA_HD, hq), F32),
                        *[pltpu.VMEM((FAR_TILE, hq), F32)] * 2,
                        *[pltpu.VMEM((1, hq), F32)] * 2],
        compiler_params=_cparams(("parallel", "parallel")),
        name="nsa_attention",
    )(kvc, kvct, ks3, kw3, vt4, qt, gt, ovl_t, bd, bd1, bc)


def _mla_prep_tile(u, mla, rows, k_ref, qt_ref, vt_ref):
    qg_ref, kg_ref, wqt_ref, wqpt_ref, wk_ref, wvt_ref, cosp_ref, sinp_ref, cost_ref, sint_ref = mla
    scale = (MLA_NOPE + MLA_ROPE) ** -0.5 * LOG2E
    cqn = _rms(u[:, 0:Q_LORA], qg_ref[...]).astype(BF16)
    ckvn = _rms(u[:, Q_LORA:Q_LORA + KV_LORA], kg_ref[...]).astype(BF16)
    qta = _dot_nt(wqt_ref[...], cqn)
    qtp = _dot_nt(wqpt_ref[...], cqn)
    cos_t = cost_ref[:, rows]
    sin_t = sint_ref[:, rows]
    n_tok = u.shape[0]
    for h in range(MLA_HEADS):
        r0 = h * 128
        qt_ref[r0:r0 + MLA_NOPE, rows] = (qta[r0:r0 + MLA_NOPE, :] * scale).astype(BF16)
        rope = qta[r0 + MLA_NOPE:r0 + MLA_NOPE + MLA_ROPE, :] * cos_t + qtp[h * MLA_ROPE:(h + 1) * MLA_ROPE, :] * sin_t
        qt_ref[r0 + MLA_NOPE:r0 + MLA_NOPE + MLA_ROPE, rows] = (rope * scale).astype(BF16)
        qt_ref[r0 + MLA_NOPE + MLA_ROPE:r0 + 128, rows] = jnp.zeros((128 - MLA_NOPE - MLA_ROPE, n_tok), BF16)
    krope = u[:, 384:512] * cosp_ref[rows, :] + u[:, 512:640] * sinp_ref[rows, :]
    knope = _dot(ckvn, wk_ref[...])
    for h in range(MLA_HEADS):
        k_ref[rows, h * 128:(h + 1) * 128] = (knope[:, h * 128:(h + 1) * 128] + krope).astype(BF16)
    vt = _dot_nt(wvt_ref[...], ckvn).astype(BF16)
    for h in range(MLA_HEADS):
        vt_ref[h * VT_ROWS:h * VT_ROWS + MLA_V, :] = vt[h * MLA_V:(h + 1) * MLA_V, :]
        vt_ref[h * VT_ROWS + MLA_V:(h + 1) * VT_ROWS, :] = jnp.ones((VT_ROWS - MLA_V, n_tok), BF16)


def _mla_attn_kernel(k_ref, vt_ref, qt_ref, o_ref, m_ref, acc_ref, sa_ref, sb_ref, cma_ref, cmb_ref, *, tile):
    qi = pl.program_id(1)
    heads = range(MLA_HEADS)
    m_ref[...] = jnp.full(m_ref.shape, NEG_INF, F32)
    acc_ref[...] = jnp.zeros(acc_ref.shape, F32)

    def scores(t, h):
        hs = slice(h * 128, (h + 1) * 128)
        return _dot(k_ref[pl.ds(pl.multiple_of(t * tile, tile), tile), hs], qt_ref[hs, :])

    def produce(t, s_x, cm_x):
        for h in heads:
            s = scores(t, h)
            s_x[h] = s
            cm_x[h:h + 1, :] = jnp.max(s, axis=0, keepdims=True)

    def consume(t, tiles, maxima):
        probs, alphas = [], []
        for h in heads:
            m_old = m_ref[h:h + 1, :]
            m_new = jnp.maximum(m_old, maxima[h])
            probs.append(jnp.exp2(tiles[h] - m_new).astype(BF16))
            alphas.append(jnp.exp2(m_old - m_new))
            m_ref[h:h + 1, :] = m_new
        for h in heads:
            acc_ref[h] = alphas[h] * acc_ref[h] + _dot(vt_ref[t, h * VT_ROWS:(h + 1) * VT_ROWS, :], probs[h])

    def consume_from(t, s_x, cm_x):
        consume(t, [s_x[h] for h in heads], [cm_x[h:h + 1, :] for h in heads])

    @pl.when(qi > 0)
    def _():
        produce(0, sa_ref, cma_ref)

    def pair_body(j, carry):
        produce(2 * j + 1, sb_ref, cmb_ref)
        consume_from(2 * j, sa_ref, cma_ref)
        produce(jnp.minimum(2 * j + 2, qi - 1), sa_ref, cma_ref)
        consume_from(2 * j + 1, sb_ref, cmb_ref)
        return carry

    lax.fori_loop(0, qi // 2, pair_body, 0)

    @pl.when(qi % 2 == 1)
    def _():
        consume_from(qi - 1, sa_ref, cma_ref)

    kl = lax.broadcasted_iota(jnp.int32, (tile, tile), 0)
    iq = lax.broadcasted_iota(jnp.int32, (tile, tile), 1)
    diag = [jnp.where(kl <= iq, scores(qi, h), NEG_INF) for h in heads]
    consume(qi, diag, [jnp.max(s, axis=0, keepdims=True) for s in diag])
    outs = [acc_ref[h, 0:MLA_V, :] * (1.0 / acc_ref[h, MLA_V:MLA_V + 1, :]) for h in heads]
    o_ref[...] = jnp.concatenate(outs, axis=0).T.astype(BF16)


def _mla_attn(k, qt, vt, *, batch, seq, tile=MLA_TILE):
    nt = seq // tile
    y = pl.pallas_call(
        functools.partial(_mla_attn_kernel, tile=tile),
        grid=(batch, nt),
        in_specs=[pl.BlockSpec((None, seq, 512), lambda b, q: (b, 0, 0)),
                  pl.BlockSpec((None, nt, MLA_HEADS * VT_ROWS, tile), lambda b, q: (b, 0, 0, 0)),
                  pl.BlockSpec((None, 512, tile), lambda b, q: (b, 0, q))],
        out_specs=pl.BlockSpec((None, tile, 256), lambda b, q: (b, q, 0)),
        out_shape=jax.ShapeDtypeStruct((batch, seq, 256), BF16),
        scratch_shapes=[pltpu.VMEM((8, tile), F32),
                        pltpu.VMEM((MLA_HEADS, VT_ROWS, tile), F32),
                        *[pltpu.VMEM((MLA_HEADS, tile, tile), F32)] * 2,
                        *[pltpu.VMEM((8, tile), F32)] * 2],
        compiler_params=_cparams(("parallel", "parallel")),
        name="mla_attention",
    )(k, vt, qt)
    return y.reshape(batch * seq, 256)


def _mlp_kernel(h_ref, yc_ref, yn_ref, ym_ref, yp_ref, wo_ref, g_ref, w1_ref, w2_ref, fg_ref, o_ref, *, final, chunk):
    y = jnp.concatenate([yc_ref[...], yn_ref[...], ym_ref[...], yp_ref[...]], axis=1)
    h2 = h_ref[...] + _dot(y, wo_ref[...])
    xn = _rms(h2, g_ref[...]).astype(BF16)
    acc = h2
    for c in range(D_FF // chunk):
        z = _dot(xn, w1_ref[:, c * chunk:(c + 1) * chunk])
        a = jnp.square(jnp.maximum(z, 0.0)).astype(BF16)
        acc = acc + _dot(a, w2_ref[c * chunk:(c + 1) * chunk, :])
    if final:
        acc = _rms(acc, fg_ref[...])
    o_ref[...] = acc


def _out_mlp(h, yc, yn, ym, yp, wo, g, w1, w2, fg, *, final, tile=512, chunk=1024):
    n = h.shape[0]
    row = lambda w: pl.BlockSpec((tile, w), lambda i: (i, 0))
    once = lambda shape: pl.BlockSpec(shape, lambda i: (0,) * len(shape), pipeline_mode=pl.Buffered(1))
    return pl.pallas_call(
        functools.partial(_mlp_kernel, final=final, chunk=chunk),
        grid=(n // tile,),
        in_specs=[row(D_MODEL), row(256), row(256), row(256), row(256),
                  once(wo.shape), _const_spec((1, D_MODEL)), once(w1.shape), once(w2.shape),
                  _const_spec((1, D_MODEL))],
        out_specs=row(D_MODEL),
        out_shape=jax.ShapeDtypeStruct((n, D_MODEL), F32),
        compiler_params=_cparams(("parallel",)),
        name="out_mlp",
    )(h, yc, yn, ym, yp, wo, g, w1, w2, fg)


def _bucket_np(n):
    n = np.maximum(n, 0)
    max_exact = N_BUCKETS // 2
    nf = np.maximum(n, 1).astype(np.float32)
    large = max_exact + (np.log(nf / np.float32(max_exact)) / np.float32(math.log(MAX_DIST / max_exact))
                         * np.float32(N_BUCKETS - max_exact)).astype(np.int32)
    large = np.minimum(large, N_BUCKETS - 1)
    return np.where(n < max_exact, n, large).astype(np.int32)


def _bias_tiles(rel_table):
    shifted = rel_table - rel_table[N_BUCKETS - 1][None, :]
    kl = np.arange(QB)[:, None]
    iq = np.arange(QB)[None, :]

    def tile(dist):
        onehot = (_bucket_np(dist)[..., None] == np.arange(N_BUCKETS)).astype(np.float32)
        t = jnp.einsum("kib,bh->khi", jnp.asarray(onehot), shifted, precision=lax.Precision.HIGHEST)
        return t.reshape(dist.shape[0], NSA_HEADS * QB)

    bd = tile(iq - kl)
    bd1 = tile(iq + QB - kl)
    cc = np.arange(32)[:, None]
    bc = tile(iq - CMP_STRIDE * (cc - 16) - (CMP_LEN - 1))
    return bd, bd1, bc


def _overlap_t(seq):
    nc = seq // CMP_STRIDE
    ns = seq // SLC_LEN
    c_lo = np.arange(nc)[None, :] * CMP_STRIDE
    s_lo = np.arange(ns)[:, None] * SLC_LEN
    ov = np.clip(np.minimum(c_lo + CMP_LEN, s_lo + SLC_LEN) - np.maximum(c_lo, s_lo), 0, None) / CMP_LEN
    ov[:, nc - 1] = 0.0
    return jnp.asarray(ov, BF16)


def _rope_tables(seq):
    pos = jnp.arange(seq, dtype=F32)
    inv_freq = ROPE_THETA ** (-jnp.arange(0, MLA_ROPE, 2, dtype=F32) / MLA_ROPE)
    ang = pos[:, None] * inv_freq[None, :]
    cos = jnp.concatenate([jnp.cos(ang)] * 2, axis=1)
    sin = jnp.concatenate([jnp.sin(ang)] * 2, axis=1)
    place = lambda a: jnp.pad(a, ((0, 0), (MLA_NOPE, 128 - MLA_NOPE - MLA_ROPE)))
    return place(cos), place(sin), cos.T, sin.T


def _rot_cols(w):
    half = MLA_ROPE // 2
    return jnp.concatenate([-w[..., half:], w[..., :half]], axis=-1)


def _layer_weights(w_in, conv_dw, cmp_pos, cmp_w1, cmp_w2, w_uq, w_ukv, pool_w):
    kv = lambda i: w_in[:, _C_KV + 64 * i:_C_KV + 64 * (i + 1)]
    kr = w_in[:, _C_KR:_C_KR + MLA_ROPE]
    place = lambda a: jnp.pad(a, ((0, 0), (MLA_NOPE, 128 - MLA_NOPE - MLA_ROPE)))
    wrow = jnp.concatenate([
        w_in[:, _C_CONV:_C_CONV + 512], w_in[:, _C_POOL:_C_POOL + 256], kv(0), kv(1),
        jnp.pad(kv(2), ((0, 0), (0, 64))), jnp.pad(kv(4), ((0, 0), (0, 64))),
        w_in[:, _C_CQ:_C_CQ + Q_LORA], w_in[:, _C_CKV:_C_CKV + KV_LORA], place(kr), place(_rot_cols(kr))],
        axis=1).astype(BF16)
    wt = jnp.concatenate([
        w_in[:, _C_Q:_C_Q + 256].T * (NSA_HD ** -0.5), kv(3).T, kv(5).T,
        jnp.pad(w_in[:, _C_G:_C_G + 12].T, ((0, 4), (0, 0)))], axis=0).astype(BF16)

    dw = jnp.pad(conv_dw, ((0, 1), (0, 0)))

    w1r = cmp_w1.reshape(2, CMP_LEN, NSA_HD, 128)
    zk = jnp.zeros_like(w1r[0])
    w1kv = jnp.concatenate([jnp.concatenate([w1r[0], zk], axis=-1),
                            jnp.concatenate([zk, w1r[1]], axis=-1)], axis=1)
    w1a = w1kv[:16].reshape(16 * 128, 256).astype(BF16)
    w1b = w1kv[16:].reshape(16 * 128, 256).astype(BF16)
    pos_kv = jnp.concatenate([cmp_pos[0], cmp_pos[1]], axis=-1)
    posa = pos_kv[:16].reshape(1, 16 * 128)
    posb = pos_kv[16:].reshape(1, 16 * 128)
    z2 = jnp.zeros_like(cmp_w2[0])
    w2bd = jnp.concatenate([jnp.concatenate([cmp_w2[0], z2], axis=1),
                            jnp.concatenate([z2, cmp_w2[1]], axis=1)], axis=0).astype(BF16)

    wq = w_uq.reshape(Q_LORA, MLA_HEADS, MLA_NOPE + MLA_ROPE)
    wqt = jnp.pad(wq, ((0, 0), (0, 0), (0, 128 - MLA_NOPE - MLA_ROPE))).reshape(Q_LORA, 512).T.astype(BF16)
    wqpt = _rot_cols(wq[:, :, MLA_NOPE:]).reshape(Q_LORA, MLA_HEADS * MLA_ROPE).T.astype(BF16)
    wkv = w_ukv.reshape(KV_LORA, MLA_HEADS, MLA_NOPE + MLA_V)
    wk = jnp.pad(wkv[:, :, :MLA_NOPE], ((0, 0), (0, 0), (0, 128 - MLA_NOPE))).reshape(KV_LORA, 512).astype(BF16)
    wvt = wkv[:, :, MLA_NOPE:].reshape(KV_LORA, MLA_HEADS * MLA_V).T.astype(BF16)

    poolw = jax.scipy.linalg.block_diag(*[pool_w[g] for g in range(len(POOL_WINDOWS))]).astype(BF16)
    return dict(wrow=wrow, wt=wt, dw=dw, w1a=w1a, w1b=w1b, posa=posa, posb=posb, w2bd=w2bd,
                wqt=wqt, wqpt=wqpt, wk=wk, wvt=wvt, poolw=poolw)


def kernel(x, w_in, w_out, ln_mix_g, ln_mlp_g, conv_dw, conv_dw_b, conv_ln_g, conv_ln_b, conv_pw,
           nsa_cmp_pos, nsa_cmp_w1, nsa_cmp_w2, mla_q_norm_g, mla_w_uq, mla_kv_norm_g, mla_w_ukv,
           pool_w, pool_scale, mlp_w1, mlp_w2, rel_bias_table, final_norm_g):
    batch, seq, _ = x.shape
    depth = w_in.shape[0]
    bd, bd1, bc = _bias_tiles(rel_bias_table)
    ovl_t = _overlap_t(seq)
    cosp, sinp, cost, sint = _rope_tables(seq)
    r1 = lambda v: v.reshape(1, -1)
    h = x.reshape(batch * seq, D_MODEL)
    for l in range(depth):
        w = _layer_weights(w_in[l], conv_dw[l], nsa_cmp_pos[l], nsa_cmp_w1[l], nsa_cmp_w2[l],
                           mla_w_uq[l], mla_w_ukv[l], pool_w[l])
        mla_args = (r1(mla_q_norm_g[l]), r1(mla_kv_norm_g[l]), w["wqt"], w["wqpt"], w["wk"], w["wvt"],
                    cosp, sinp, cost, sint)
        loc_args = (w["dw"], r1(conv_dw_b[l]), r1(conv_ln_g[l]), r1(conv_ln_b[l]), conv_pw[l].astype(BF16),
                    w["poolw"], r1(pool_scale[l]))
        yc, yp, kcvc, ks, kw, km, qtm, vtm, qt, vt, gt = _proj(
            h, r1(ln_mix_g[l]), w["wrow"], w["wt"], mla_args, loc_args, batch=batch, seq=seq)
        kvc, kvct = _compress(kcvc, w["posa"], w["posb"], w["w1a"], w["w1b"], w["w2bd"], batch=batch, seq=seq)
        yn = _nsa(kvc, kvct, ks, kw, vt, qt, gt, ovl_t, bd, bd1, bc, batch=batch, seq=seq)
        ym = _mla_attn(km.reshape(batch, seq, 512), qtm, vtm, batch=batch, seq=seq)
        h = _out_mlp(h, yc, yn, ym, yp, w_out[l].astype(BF16), r1(ln_mlp_g[l]), mlp_w1[l].astype(BF16),
                     mlp_w2[l].astype(BF16), r1(final_norm_g), final=(l == depth - 1))
    return h.reshape(batch, seq, D_MODEL)
```

```python
import functools
import math

import numpy as np
import jax
import jax.numpy as jnp
from jax import lax
from jax.experimental import pallas as pl
from jax.experimental.pallas import tpu as pltpu

F32 = jnp.float32
BF16 = jnp.bfloat16

D_MODEL = 1024
GROUP_W = 256
CONV_K = 31
NSA_HEADS = 4
NSA_HD = 64
CMP_LEN = 32
CMP_STRIDE = 16
SLC_LEN = 64
N_SELECT = 16
WINDOW = 512
MLA_HEADS = 4
MLA_NOPE = 64
MLA_ROPE = 32
MLA_V = 64
Q_LORA = 256
KV_LORA = 128
ROPE_THETA = 10000.0
POOL_WINDOWS = (2, 4, 8, 16)
D_FF = 4 * D_MODEL
N_BUCKETS = 32
MAX_DIST = 128
EPS = 1e-6
NEG_INF = -1e30
FORCE_SCORE = 1e9
LOG2E = math.log2(math.e)

LANES = 128
SUBLANES = 8
QB = 128
HALO = 32
FAR_TILE = 512
FAR_BLOCKS = FAR_TILE // SLC_LEN
MASK_ROWS = 16
VT_ROWS = 80
MLA_TILE = 512
VMEM_LIMIT = 56 * 1024 * 1024

_C_CONV, _C_Q, _C_KV, _C_G, _C_CQ, _C_CKV, _C_KR, _C_POOL = 0, 512, 768, 1152, 1164, 1420, 1548, 1580


def _cparams(sem):
    return pltpu.CompilerParams(dimension_semantics=sem, vmem_limit_bytes=VMEM_LIMIT)


def _rms(x, g):
    ms = jnp.mean(x * x, axis=-1, keepdims=True)
    return x * lax.rsqrt(ms + EPS) * g


def _dot(a, b):
    return jnp.dot(a, b, preferred_element_type=F32)


def _dot_nt(a, b):
    return lax.dot_general(a, b, (((1,), (1,)), ((), ())), preferred_element_type=F32)


def _const_spec(shape):
    nd = len(shape)
    return pl.BlockSpec(shape, lambda *_: (0,) * nd)


def _proj_kernel(x_ref, g_ref, wrow_ref, wt_ref, *rest, tile, sub, steps_per_seq):
    mla = rest[:10]
    loc = rest[10:17]
    yc_ref, yp_ref, kcvc_ref, ks_ref, kw_ref, kmla_ref, qtmla_ref, vtmla_ref, qt_ref, vt_ref, gt_ref = rest[17:28]
    scratch = rest[28:]
    step = pl.program_id(0)
    seq_step = step % steps_per_seq

    @pl.when(step == 0)
    def _():
        scratch[0][sub:sub + HALO, :] = jnp.zeros((HALO, GROUP_W), F32)
        scratch[1][sub:sub + HALO, :] = jnp.zeros((HALO, GROUP_W), F32)

    ones = jnp.ones((VT_ROWS - NSA_HD, LANES), BF16)
    lane = lax.broadcasted_iota(jnp.int32, (sub, LANES), 1)
    for r in range(tile // sub):
        rows = slice(r * sub, (r + 1) * sub)
        xn = _rms(x_ref[rows, :], g_ref[...]).astype(BF16)
        first = (seq_step == 0) if r == 0 else False
        _local_tile(_dot(xn, wrow_ref[:, 0:512]), _dot(xn, wrow_ref[:, 512:768]), first,
                    seq_step * tile + r * sub, loc, rows, yc_ref, yp_ref, scratch)
        kk = _dot(xn, wrow_ref[:, 768:1024])
        kcvc_ref[rows, :] = kk[:, 0:LANES]
        tok = pl.program_id(0) * tile + r * sub + lax.broadcasted_iota(jnp.int32, (sub, LANES), 0)
        onehot = (lane - NSA_HD == (tok // SLC_LEN) % FAR_BLOCKS).astype(F32)
        ks_ref[rows, :] = (kk[:, LANES:2 * LANES] + onehot).astype(BF16)
        km = _dot(xn, wrow_ref[:, 1024:1792])
        kw_ref[rows, :] = km[:, 0:LANES].astype(BF16)
        _mla_prep_tile(km[:, LANES:], mla, rows, kmla_ref, qtmla_ref, vtmla_ref.at[r])
        qt_ref[:, rows] = _dot_nt(wt_ref[0:256, :], xn).astype(BF16)
        vs = _dot_nt(wt_ref[256:384, :], xn).astype(BF16)
        for j in range(sub // LANES):
            cols = slice(j * LANES, (j + 1) * LANES)
            jt = r * (sub // LANES) + j
            vt_ref[jt, 0:NSA_HD, :] = vs[0:NSA_HD, cols]
            vt_ref[jt, NSA_HD:VT_ROWS, :] = ones
            vt_ref[jt, VT_ROWS:VT_ROWS + NSA_HD, :] = vs[NSA_HD:2 * NSA_HD, cols]
            vt_ref[jt, VT_ROWS + NSA_HD:2 * VT_ROWS, :] = ones
        gt_ref[:, rows] = _dot_nt(wt_ref[384:400, :], xn)


def _proj(h, g, wrow, wt, mla_args, loc_args, *, batch, seq, tile=1024):
    qg, kg, wqt, wqpt, wk, wvt, cosp, sinp, cost, sint = mla_args
    dw, dwb, lng, lnb, pw, poolw, pscale = loc_args
    sub = MLA_TILE
    n = h.shape[0]
    spb = seq // tile
    mt = tile // MLA_TILE
    outs = (
        jax.ShapeDtypeStruct((n, GROUP_W), BF16),
        jax.ShapeDtypeStruct((n, GROUP_W), BF16),
        jax.ShapeDtypeStruct((n, 128), F32),
        jax.ShapeDtypeStruct((n, 128), BF16),
        jax.ShapeDtypeStruct((n, 128), BF16),
        jax.ShapeDtypeStruct((n, 512), BF16),
        jax.ShapeDtypeStruct((batch, 512, seq), BF16),
        jax.ShapeDtypeStruct((batch, seq // MLA_TILE, MLA_HEADS * VT_ROWS, MLA_TILE), BF16),
        jax.ShapeDtypeStruct((256, n), BF16),
        jax.ShapeDtypeStruct((n // LANES, 2 * VT_ROWS, LANES), BF16),
        jax.ShapeDtypeStruct((16, n), F32),
    )
    row = lambda w: pl.BlockSpec((tile, w), lambda i: (i, 0))
    return pl.pallas_call(
        functools.partial(_proj_kernel, tile=tile, sub=sub, steps_per_seq=spb),
        grid=(n // tile,),
        in_specs=[row(D_MODEL), _const_spec((1, D_MODEL)), _const_spec(wrow.shape), _const_spec(wt.shape),
                  _const_spec((1, Q_LORA)), _const_spec((1, KV_LORA)),
                  _const_spec(wqt.shape), _const_spec(wqpt.shape), _const_spec(wk.shape), _const_spec(wvt.shape),
                  pl.BlockSpec((tile, 128), lambda i: (i % spb, 0)), pl.BlockSpec((tile, 128), lambda i: (i % spb, 0)),
                  pl.BlockSpec((MLA_ROPE, tile), lambda i: (0, i % spb)),
                  pl.BlockSpec((MLA_ROPE, tile), lambda i: (0, i % spb)),
                  _const_spec(dw.shape), _const_spec((1, GROUP_W)), _const_spec((1, GROUP_W)),
                  _const_spec((1, GROUP_W)), _const_spec((GROUP_W, GROUP_W)),
                  _const_spec((GROUP_W, GROUP_W)), _const_spec((1, GROUP_W))],
        out_specs=(row(GROUP_W), row(GROUP_W), row(128), row(128), row(128), row(512),
                   pl.BlockSpec((None, 512, tile), lambda i: (i // spb, 0, i % spb)),
                   pl.BlockSpec((None, mt, MLA_HEADS * VT_ROWS, MLA_TILE), lambda i: (i // spb, i % spb, 0, 0)),
                   pl.BlockSpec((256, tile), lambda i: (0, i)),
                   pl.BlockSpec((tile // LANES, 2 * VT_ROWS, LANES), lambda i: (i, 0, 0)),
                   pl.BlockSpec((16, tile), lambda i: (0, i))),
        out_shape=outs,
        scratch_shapes=[pltpu.VMEM((HALO + sub, GROUP_W), F32)] * 4
        + [pltpu.VMEM((SUBLANES - 1, HALO + sub, GROUP_W), F32)],
        compiler_params=_cparams(("arbitrary",)),
        name="in_proj",
    )(h, g, wrow, wt, *mla_args, *loc_args)


def _local_tile(uc, up, first, pos0, loc, out_rows, yc_ref, yp_ref, scratch, *, rows=128):
    dw_ref, dwb_ref, lng_ref, lnb_ref, pw_ref, poolw_ref, pscale_ref = loc
    hbuf, ubuf, sa, sb, hshift = scratch
    tile = uc.shape[0]
    hbuf[0:HALO, :] = jnp.where(first, 0.0, hbuf[tile:tile + HALO, :])
    ubuf[0:HALO, :] = jnp.where(first, 0.0, ubuf[tile:tile + HALO, :])
    hbuf[HALO:HALO + tile, :] = uc[:, 0:GROUP_W] * jax.nn.sigmoid(uc[:, GROUP_W:2 * GROUP_W])
    span = HALO + tile - SUBLANES
    for j in range(1, SUBLANES):
        hshift[j - 1, 0:span, :] = hbuf[pl.ds(j, span), :]
    dwb = dwb_ref[...]
    for c in range(tile // rows):
        acc = jnp.zeros((rows, GROUP_W), F32) + dwb
        for k in range(CONV_K):
            off = c * rows + HALO - (CONV_K - 1) + k
            phase, base = off % SUBLANES, off - off % SUBLANES
            tap = hbuf[pl.ds(base, rows), :] if phase == 0 else hshift[phase - 1, pl.ds(base, rows), :]
            acc = acc + dw_ref[k:k + 1, :] * tap
        mu = jnp.mean(acc, axis=-1, keepdims=True)
        cen = acc - mu
        var = jnp.mean(cen * cen, axis=-1, keepdims=True)
        y = cen * lax.rsqrt(var + EPS) * lng_ref[...] + lnb_ref[...]
        y = y * jax.nn.sigmoid(y)
        r0 = out_rows.start + c * rows
        yc_ref[r0:r0 + rows, :] = _dot(y.astype(BF16), pw_ref[...]).astype(BF16)

    ext = HALO + tile
    ubuf[HALO:ext, :] = up
    sa[pl.ds(8, ext - 8), :] = ubuf[pl.ds(8, ext - 8), :] + ubuf[pl.ds(7, ext - 8), :]
    sb[pl.ds(16, ext - 16), :] = sa[pl.ds(16, ext - 16), :] + sa[pl.ds(14, ext - 16), :]
    s2 = sa[pl.ds(HALO, tile), :]
    s4 = sb[pl.ds(HALO, tile), :]
    sa[pl.ds(24, ext - 24), :] = sb[pl.ds(24, ext - 24), :] + sb[pl.ds(20, ext - 24), :]
    s8 = sa[pl.ds(HALO, tile), :]
    s16 = s8 + sa[pl.ds(HALO - 8, tile), :]
    lane = lax.broadcasted_iota(jnp.int32, (tile, GROUP_W), 1)
    grp = lane // (GROUP_W // len(POOL_WINDOWS))
    wsum = jnp.where(grp == 0, s2, jnp.where(grp == 1, s4, jnp.where(grp == 2, s8, s16)))
    width = jnp.where(grp == 0, 2, jnp.where(grp == 1, 4, jnp.where(grp == 2, 8, 16)))
    pos = pos0 + lax.broadcasted_iota(jnp.int32, (tile, GROUP_W), 0)
    cnt = jnp.minimum(width, pos + 1).astype(F32)
    d = wsum / cnt - up
    yp_ref[out_rows, :] = (_dot(d.astype(BF16), poolw_ref[...]) * pscale_ref[...]).astype(BF16)


def _compress_kernel(kr_ref, posa_ref, posb_ref, w1a_ref, w1b_ref, w2_ref, kv_ref, kvt_ref, bbuf, *, nc):
    kr = kr_ref[...]
    a = _dot((kr + posa_ref[...]).astype(BF16), w1a_ref[...])
    b = _dot((kr + posb_ref[...]).astype(BF16), w1b_ref[...])
    bbuf[0:nc, :] = b
    bbuf[nc:nc + 8, :] = jnp.zeros((8, 256), F32)
    hdn = a + bbuf[pl.ds(1, nc), :]
    hdn = hdn * jax.nn.sigmoid(hdn)
    kv = _dot(hdn.astype(BF16), w2_ref[...])
    kv_ref[...] = kv.astype(BF16)
    kvt_ref[...] = kv.T.astype(BF16)


def _compress(kcvc, posa, posb, w1a, w1b, w2bd, *, batch, seq):
    nc = seq // CMP_STRIDE
    kr = kcvc.reshape(batch, nc, CMP_STRIDE * 128)
    return pl.pallas_call(
        functools.partial(_compress_kernel, nc=nc),
        grid=(batch,),
        in_specs=[pl.BlockSpec((None, nc, CMP_STRIDE * 128), lambda b: (b, 0, 0)),
                  _const_spec(posa.shape), _const_spec(posb.shape), _const_spec(w1a.shape),
                  _const_spec(w1b.shape), _const_spec(w2bd.shape)],
        out_specs=(pl.BlockSpec((None, nc, 128), lambda b: (b, 0, 0)),
                   pl.BlockSpec((None, 128, nc), lambda b: (b, 0, 0))),
        out_shape=(jax.ShapeDtypeStruct((batch, nc, 128), BF16),
                   jax.ShapeDtypeStruct((batch, 128, nc), BF16)),
        scratch_shapes=[pltpu.VMEM((nc + 8, 256), F32)],
        compiler_params=_cparams(("parallel",)),
        name="nsa_compress",
    )(kr, posa, posb, w1a, w1b, w2bd)


def _nsa_kernel(kvc_ref, kvct_ref, ks_ref, kw_ref, vt_ref, qt_ref, gt_ref, ovl_ref, bd_ref, bd1_ref, bc_ref,
                y_ref, s_ref, key_ref, tie_ref, sel_ref, mb_ref, m_ref, acc_ref, mn_ref, accn_ref, oc_ref, ow_ref, sa_ref, sb_ref, cma_ref, cmb_ref,
                *, nc, ns, n_top):
    qb = pl.program_id(1)
    hq = NSA_HEADS * QB
    qt = jnp.concatenate([qt_ref[h * NSA_HD:(h + 1) * NSA_HD, :] for h in range(NSA_HEADS)], axis=1)
    qts = jnp.concatenate([qt, jnp.zeros_like(qt)], axis=0)
    lane_q = lax.broadcasted_iota(jnp.int32, (1, hq), 1) % QB

    tq = qb * QB + lane_q
    pad = 16
    sub8 = lax.broadcasted_iota(jnp.int32, (8, QB), 0)

    def select(nrows, nblk):
        s_ref[0:pad, :] = jnp.zeros((pad, hq), F32)
        s_ref[pad + nrows:pad + nrows + pad, :] = jnp.zeros((pad, hq), F32)
        s_ref[pad:pad + nrows, :] = _dot(kvc_ref[0:nrows, :], qts)
        w0 = pl.multiple_of(qb * 8, 8)
        s_ref[pl.ds(w0, 32), :] = s_ref[pl.ds(w0, 32), :] + bc_ref[...]
        sc = s_ref[pad:pad + nrows, :]
        cidx = lax.broadcasted_iota(jnp.int32, (nrows, hq), 0)
        last_c = jnp.minimum(lax.shift_right_arithmetic(tq - (CMP_LEN - 1), 4), nc - 2)
        sc = jnp.where(cidx <= last_c, sc, NEG_INF)
        mc = jnp.max(sc, axis=0, keepdims=True)
        pc = jnp.exp(sc - mc)
        lc = jnp.sum(pc, axis=0, keepdims=True)
        pc = pc * jnp.where(last_c >= 0, 1.0 / lc, 0.0)
        oc_ref[...] = _dot(kvct_ref[NSA_HD:2 * NSA_HD, 0:nrows], pc.astype(BF16))

        psum = pc[:, 0:QB]
        for h in range(1, NSA_HEADS):
            psum = psum + pc[:, h * QB:(h + 1) * QB]
        p_hi = psum.astype(BF16)
        p_lo = (psum - p_hi.astype(F32)).astype(BF16)
        ovl = ovl_ref[0:nblk, 0:nrows]
        imp = _dot(ovl, p_hi) + _dot(ovl, p_lo)
        jidx = lax.broadcasted_iota(jnp.int32, (nblk, QB), 0)
        iq = lax.broadcasted_iota(jnp.int32, (nblk, QB), 1)
        jc = 2 * qb + (iq >= SLC_LEN).astype(jnp.int32)
        forced = (jidx == 0) | (jidx == jc) | (jidx == jc - 1)
        imp = jnp.where(forced, FORCE_SCORE, imp)
        imp = jnp.where(jidx <= jc, imp, NEG_INF)
        key = pltpu.bitcast(imp, jnp.int32)
        key_ref[0:nblk, :] = key
        tie_ref[0:nblk, :] = jnp.zeros((nblk, QB), jnp.int32)
        n_vreg = nblk // 8

        def rank_body(g, cnts):
            base = pl.multiple_of(g * 8, 8)
            rows = key_ref[pl.ds(base, 8), :]
            out = []
            for tv in range(n_vreg):
                target = key[tv * 8:(tv + 1) * 8, :] - jnp.where(tv > g, 1, 0)
                c = cnts[tv]
                for r in range(8):
                    c = c + (rows[r:r + 1, :] > target).astype(jnp.int32)
                out.append(c)
            ties = jnp.zeros((8, QB), jnp.int32)
            for r in range(8):
                ties = ties + ((rows[r:r + 1, :] == rows) & (sub8 > r)).astype(jnp.int32)
            tie_ref[pl.ds(base, 8), :] = ties
            return tuple(out)

        n_groups = jnp.minimum((2 * qb + 9) // 8, n_vreg)
        cnts = lax.fori_loop(0, n_groups, rank_body, tuple(jnp.zeros((8, QB), jnp.int32) for _ in range(n_vreg)))
        cnt = jnp.concatenate(cnts, axis=0) + tie_ref[0:nblk, :]
        chosen = (cnt < n_top) & (imp > 0.5 * NEG_INF)
        sel_ref[0:nblk, :] = jnp.where(chosen, 1.0, 0.0)
        mb_ref[0:nblk, :] = jnp.where(chosen & (jidx < 2 * (qb - 1)), 0.0, NEG_INF)
        if nblk < ns:
            sel_ref[nblk:ns, :] = jnp.zeros((ns - nblk, QB), F32)
            mb_ref[nblk:ns, :] = jnp.full((ns - nblk, QB), NEG_INF, F32)

    nqb = nc // (QB // CMP_STRIDE)
    parts = next(p for p in (4, 2, 1) if nc % (p * LANES) == 0 and ns % (p * SUBLANES) == 0)
    for k in range(parts):
        in_part = (qb >= k * nqb // parts) & (qb < (k + 1) * nqb // parts)
        pl.when(in_part)(functools.partial(select, nc * (k + 1) // parts, ns * (k + 1) // parts))

    m_ref[...] = jnp.full((1, hq), NEG_INF, F32)
    acc_ref[...] = jnp.zeros((VT_ROWS, hq), F32)
    per_far = FAR_TILE // QB
    n_far = (jnp.maximum(qb - 1, 0) + per_far - 1) // per_far
    zpad = jnp.zeros((QB - NSA_HD - MASK_ROWS, hq), BF16)

    def q_aug(t):
        rows = mb_ref[pl.ds(pl.multiple_of(t * FAR_BLOCKS, FAR_BLOCKS), FAR_BLOCKS), :]
        rows = jnp.concatenate([rows] * NSA_HEADS, axis=1)
        rows = jnp.concatenate([rows, jnp.zeros((MASK_ROWS - FAR_BLOCKS, hq), F32)], axis=0).astype(BF16)
        return jnp.concatenate([qt, rows, zpad], axis=0)

    def produce(t, s_x, cm_x):
        s = _dot(ks_ref[pl.ds(pl.multiple_of(t * FAR_TILE, FAR_TILE), FAR_TILE), :], q_aug(t))
        s_x[...] = s
        cm_x[...] = jnp.max(s, axis=0, keepdims=True)

    def soften(s_x, cm_x):
        m_old = m_ref[...]
        m_new = jnp.maximum(m_old, cm_x[...])
        m_ref[...] = m_new
        return jnp.exp(s_x[...] - m_new).astype(BF16), jnp.exp(m_old - m_new)

    def accumulate(t, p, alpha):
        sub = FAR_TILE // QB
        vt = jnp.concatenate([vt_ref[sub * t + j, 0:VT_ROWS, :] for j in range(sub)], axis=1)
        acc_ref[...] = alpha * acc_ref[...] + _dot(vt, p)

    def consume(t, s_x, cm_x):
        accumulate(t, *soften(s_x, cm_x))

    produce(0, sa_ref, cma_ref)

    kl = lax.broadcasted_iota(jnp.int32, (QB, hq), 0)
    causal = kl <= lane_q
    anti = kl > lane_q
    n_w = WINDOW // QB
    kts = [jnp.maximum(qb - back, 0) for back in range(n_w + 1)]
    live = [qb >= back for back in range(n_w + 1)]

    def krows(ref, kt):
        return ref[pl.ds(pl.multiple_of(kt * QB, QB), QB), :]

    def sel_mask(kt):
        m0 = jnp.broadcast_to(sel_ref[pl.ds(2 * kt, 1), :], (SLC_LEN, QB))
        m1 = jnp.broadcast_to(sel_ref[pl.ds(2 * kt + 1, 1), :], (SLC_LEN, QB))
        mt = jnp.concatenate([m0, m1], axis=0)
        return jnp.concatenate([mt] * NSA_HEADS, axis=1) > 0.5

    def softmax_pv(tiles, row0):
        m = jnp.max(tiles[0], axis=0, keepdims=True)
        for s in tiles[1:]:
            m = jnp.maximum(m, jnp.max(s, axis=0, keepdims=True))
        acc = jnp.zeros((VT_ROWS, hq), F32)
        for back, s in enumerate(tiles):
            acc = acc + _dot(vt_ref[kts[back], row0:row0 + VT_ROWS, :], jnp.exp(s - m).astype(BF16))
        return m, acc

    s_win = [_dot(krows(kw_ref, kts[back]), qts) for back in range(n_w + 1)]
    s_sel = [_dot(krows(ks_ref, kts[back]), qts) for back in range(2)]
    s_win[0] = jnp.where(causal, s_win[0] + bd_ref[...], NEG_INF)
    s_win[1] = jnp.where(live[1], s_win[1] + bd1_ref[...], NEG_INF)
    for back in range(2, n_w):
        s_win[back] = jnp.where(live[back], s_win[back], NEG_INF)
    s_win[n_w] = jnp.where(anti & live[n_w], s_win[n_w], NEG_INF)
    s_sel[0] = jnp.where(sel_mask(qb) & causal, s_sel[0] + bd_ref[...], NEG_INF)
    s_sel[1] = jnp.where(sel_mask(kts[1]) & live[1], s_sel[1] + bd1_ref[...], NEG_INF)
    _, acc_w = softmax_pv(s_win, VT_ROWS)
    ow_ref[...] = acc_w[0:NSA_HD, :] * (1.0 / acc_w[NSA_HD:NSA_HD + 1, :])
    mn_ref[...], accn_ref[...] = softmax_pv(s_sel, 0)

    def pair_body(j, carry):
        produce(2 * j + 1, sb_ref, cmb_ref)
        consume(2 * j, sa_ref, cma_ref)
        produce(jnp.minimum(2 * j + 2, n_far - 1), sa_ref, cma_ref)
        consume(2 * j + 1, sb_ref, cmb_ref)
        return carry

    lax.fori_loop(0, n_far // 2, pair_body, 0)

    @pl.when(n_far % 2 == 1)
    def _():
        consume(n_far - 1, sa_ref, cma_ref)

    m_far = m_ref[...]
    m_near = mn_ref[...]
    m_all = jnp.maximum(m_far, m_near)
    acc = jnp.exp(m_far - m_all) * acc_ref[...] + jnp.exp(m_near - m_all) * accn_ref[...]
    osel = acc[0:NSA_HD, :] * (1.0 / acc[NSA_HD:NSA_HD + 1, :])

    gates = jax.nn.sigmoid(gt_ref[...])
    oc = oc_ref[...]
    ow = ow_ref[...]
    parts = []
    for h in range(NSA_HEADS):
        sl = slice(h * QB, (h + 1) * QB)
        parts.append(gates[3 * h:3 * h + 1, :] * oc[:, sl] + gates[3 * h + 1:3 * h + 2, :] * osel[:, sl]
                     + gates[3 * h + 2:3 * h + 3, :] * ow[:, sl])
    y_ref[...] = jnp.concatenate(parts, axis=0).T.astype(BF16)


def _nsa(kvc, kvct, ks, kw, vt, qt, gt, ovl_t, bd, bd1, bc, *, batch, seq):
    nc = seq // CMP_STRIDE
    ns = seq // SLC_LEN
    nqb = seq // QB
    hq = NSA_HEADS * QB
    ks3 = ks.reshape(batch, seq, 128)
    kw3 = kw.reshape(batch, seq, 128)
    vt4 = vt.reshape(batch, nqb, 2 * VT_ROWS, LANES)
    return pl.pallas_call(
        functools.partial(_nsa_kernel, nc=nc, ns=ns, n_top=min(N_SELECT, ns)),
        grid=(batch, nqb),
        in_specs=[pl.BlockSpec((None, nc, 128), lambda b, q: (b, 0, 0)),
                  pl.BlockSpec((None, 128, nc), lambda b, q: (b, 0, 0)),
                  pl.BlockSpec((None, seq, 128), lambda b, q: (b, 0, 0)),
                  pl.BlockSpec((None, seq, 128), lambda b, q: (b, 0, 0)),
                  pl.BlockSpec((None, nqb, 2 * VT_ROWS, LANES), lambda b, q: (b, 0, 0, 0)),
                  pl.BlockSpec((256, QB), lambda b, q: (0, b * nqb + q)),
                  pl.BlockSpec((16, QB), lambda b, q: (0, b * nqb + q)),
                  _const_spec(ovl_t.shape), _const_spec(bd.shape), _const_spec(bd1.shape),
                  _const_spec(bc.shape)],
        out_specs=pl.BlockSpec((QB, 256), lambda b, q: (b * nqb + q, 0)),
        out_shape=jax.ShapeDtypeStruct((batch * seq, 256), BF16),
        scratch_shapes=[pltpu.VMEM((nc + 32, hq), F32),
                        pltpu.VMEM((ns, QB), jnp.int32),
                        pltpu.VMEM((ns, QB), jnp.int32),
                        pltpu.VMEM((ns, QB), F32),
                        pltpu.VMEM((ns, QB), F32),
                        pltpu.VMEM((1, hq), F32),
                        pltpu.VMEM((VT_ROWS, hq), F32),
                        pltpu.VMEM((1, hq), F32),
                        pltpu.VMEM((VT_ROWS, hq), F32),
                        pltpu.VMEM((NSA_HD, hq), F32),
                        pltpu.VMEM((NSA_HD, hq), F32),
                        *[pltpu.VMEM((FAR_TILE, hq), F32)] * 2,
                        *[pltpu.VMEM((1, hq), F32)] * 2],
        compiler_params=_cparams(("parallel", "parallel")),
        name="nsa_attention",
    )(kvc, kvct, ks3, kw3, vt4, qt, gt, ovl_t, bd, bd1, bc)


def _mla_prep_tile(u, mla, rows, k_ref, qt_ref, vt_ref):
    qg_ref, kg_ref, wqt_ref, wqpt_ref, wk_ref, wvt_ref, cosp_ref, sinp_ref, cost_ref, sint_ref = mla
    scale = (MLA_NOPE + MLA_ROPE) ** -0.5 * LOG2E
    cqn = _rms(u[:, 0:Q_LORA], qg_ref[...]).astype(BF16)
    ckvn = _rms(u[:, Q_LORA:Q_LORA + KV_LORA], kg_ref[...]).astype(BF16)
    qta = _dot_nt(wqt_ref[...], cqn)
    qtp = _dot_nt(wqpt_ref[...], cqn)
    cos_t = cost_ref[:, rows]
    sin_t = sint_ref[:, rows]
    n_tok = u.shape[0]
    for h in range(MLA_HEADS):
        r0 = h * 128
        qt_ref[r0:r0 + MLA_NOPE, rows] = (qta[r0:r0 + MLA_NOPE, :] * scale).astype(BF16)
        rope = qta[r0 + MLA_NOPE:r0 + MLA_NOPE + MLA_ROPE, :] * cos_t + qtp[h * MLA_ROPE:(h + 1) * MLA_ROPE, :] * sin_t
        qt_ref[r0 + MLA_NOPE:r0 + MLA_NOPE + MLA_ROPE, rows] = (rope * scale).astype(BF16)
        qt_ref[r0 + MLA_NOPE + MLA_ROPE:r0 + 128, rows] = jnp.zeros((128 - MLA_NOPE - MLA_ROPE, n_tok), BF16)
    krope = u[:, 384:512] * cosp_ref[rows, :] + u[:, 512:640] * sinp_ref[rows, :]
    knope = _dot(ckvn, wk_ref[...])
    for h in range(MLA_HEADS):
        k_ref[rows, h * 128:(h + 1) * 128] = (knope[:, h * 128:(h + 1) * 128] + krope).astype(BF16)
    vt = _dot_nt(wvt_ref[...], ckvn).astype(BF16)
    for h in range(MLA_HEADS):
        vt_ref[h * VT_ROWS:h * VT_ROWS + MLA_V, :] = vt[h * MLA_V:(h + 1) * MLA_V, :]
        vt_ref[h * VT_ROWS + MLA_V:(h + 1) * VT_ROWS, :] = jnp.ones((VT_ROWS - MLA_V, n_tok), BF16)


def _mla_attn_kernel(k_ref, vt_ref, qt_ref, o_ref, m_ref, acc_ref, sa_ref, sb_ref, cma_ref, cmb_ref, *, tile):
    qi = pl.program_id(1)
    heads = range(MLA_HEADS)
    m_ref[...] = jnp.full(m_ref.shape, NEG_INF, F32)
    acc_ref[...] = jnp.zeros(acc_ref.shape, F32)

    def scores(t, h):
        hs = slice(h * 128, (h + 1) * 128)
        return _dot(k_ref[pl.ds(pl.multiple_of(t * tile, tile), tile), hs], qt_ref[hs, :])

    def produce(t, s_x, cm_x):
        for h in heads:
            s = scores(t, h)
            s_x[h] = s
            cm_x[h:h + 1, :] = jnp.max(s, axis=0, keepdims=True)

    def consume(t, tiles, maxima):
        probs, alphas = [], []
        for h in heads:
            m_old = m_ref[h:h + 1, :]
            m_new = jnp.maximum(m_old, maxima[h])
            probs.append(jnp.exp2(tiles[h] - m_new).astype(BF16))
            alphas.append(jnp.exp2(m_old - m_new))
            m_ref[h:h + 1, :] = m_new
        for h in heads:
            acc_ref[h] = alphas[h] * acc_ref[h] + _dot(vt_ref[t, h * VT_ROWS:(h + 1) * VT_ROWS, :], probs[h])

    def consume_from(t, s_x, cm_x):
        consume(t, [s_x[h] for h in heads], [cm_x[h:h + 1, :] for h in heads])

    produce(0, sa_ref, cma_ref)
    kl = lax.broadcasted_iota(jnp.int32, (tile, tile), 0)
    iq = lax.broadcasted_iota(jnp.int32, (tile, tile), 1)
    diag = [jnp.where(kl <= iq, scores(qi, h), NEG_INF) for h in heads]
    consume(qi, diag, [jnp.max(s, axis=0, keepdims=True) for s in diag])

    def pair_body(j, carry):
        produce(2 * j + 1, sb_ref, cmb_ref)
        consume_from(2 * j, sa_ref, cma_ref)
        produce(jnp.minimum(2 * j + 2, qi - 1), sa_ref, cma_ref)
        consume_from(2 * j + 1, sb_ref, cmb_ref)
        return carry

    lax.fori_loop(0, qi // 2, pair_body, 0)

    @pl.when(qi % 2 == 1)
    def _():
        consume_from(qi - 1, sa_ref, cma_ref)

    outs = [acc_ref[h, 0:MLA_V, :] * (1.0 / acc_ref[h, MLA_V:MLA_V + 1, :]) for h in heads]
    o_ref[...] = jnp.concatenate(outs, axis=0).T.astype(BF16)


def _mla_attn(k, qt, vt, *, batch, seq, tile=MLA_TILE):
    nt = seq // tile
    y = pl.pallas_call(
        functools.partial(_mla_attn_kernel, tile=tile),
        grid=(batch, nt),
        in_specs=[pl.BlockSpec((None, seq, 512), lambda b, q: (b, 0, 0)),
                  pl.BlockSpec((None, nt, MLA_HEADS * VT_ROWS, tile), lambda b, q: (b, 0, 0, 0)),
                  pl.BlockSpec((None, 512, tile), lambda b, q: (b, 0, q))],
        out_specs=pl.BlockSpec((None, tile, 256), lambda b, q: (b, q, 0)),
        out_shape=jax.ShapeDtypeStruct((batch, seq, 256), BF16),
        scratch_shapes=[pltpu.VMEM((8, tile), F32),
                        pltpu.VMEM((MLA_HEADS, VT_ROWS, tile), F32),
                        *[pltpu.VMEM((MLA_HEADS, tile, tile), F32)] * 2,
                        *[pltpu.VMEM((8, tile), F32)] * 2],
        compiler_params=_cparams(("parallel", "parallel")),
        name="mla_attention",
    )(k, vt, qt)
    return y.reshape(batch * seq, 256)


def _mlp_kernel(h_ref, yc_ref, yn_ref, ym_ref, yp_ref, wo_ref, g_ref, w1_ref, w2_ref, fg_ref, o_ref, *, final, chunk):
    y = jnp.concatenate([yc_ref[...], yn_ref[...], ym_ref[...], yp_ref[...]], axis=1)
    h2 = h_ref[...] + _dot(y, wo_ref[...])
    xn = _rms(h2, g_ref[...]).astype(BF16)
    acc = h2
    for c in range(D_FF // chunk):
        z = _dot(xn, w1_ref[:, c * chunk:(c + 1) * chunk])
        a = jnp.square(jnp.maximum(z, 0.0)).astype(BF16)
        acc = acc + _dot(a, w2_ref[c * chunk:(c + 1) * chunk, :])
    if final:
        acc = _rms(acc, fg_ref[...])
    o_ref[...] = acc


def _out_mlp(h, yc, yn, ym, yp, wo, g, w1, w2, fg, *, final, tile=1024, chunk=1024):
    n = h.shape[0]
    row = lambda w: pl.BlockSpec((tile, w), lambda i: (i, 0))
    once = lambda shape: pl.BlockSpec(shape, lambda i: (0,) * len(shape), pipeline_mode=pl.Buffered(1))
    return pl.pallas_call(
        functools.partial(_mlp_kernel, final=final, chunk=chunk),
        grid=(n // tile,),
        in_specs=[row(D_MODEL), row(256), row(256), row(256), row(256),
                  once(wo.shape), _const_spec((1, D_MODEL)), once(w1.shape), once(w2.shape),
                  _const_spec((1, D_MODEL))],
        out_specs=row(D_MODEL),
        out_shape=jax.ShapeDtypeStruct((n, D_MODEL), F32),
        compiler_params=_cparams(("parallel",)),
        name="out_mlp",
    )(h, yc, yn, ym, yp, wo, g, w1, w2, fg)


def _bucket_np(n):
    n = np.maximum(n, 0)
    max_exact = N_BUCKETS // 2
    nf = np.maximum(n, 1).astype(np.float32)
    large = max_exact + (np.log(nf / np.float32(max_exact)) / np.float32(math.log(MAX_DIST / max_exact))
                         * np.float32(N_BUCKETS - max_exact)).astype(np.int32)
    large = np.minimum(large, N_BUCKETS - 1)
    return np.where(n < max_exact, n, large).astype(np.int32)


def _bias_tiles(rel_table):
    shifted = rel_table - rel_table[N_BUCKETS - 1][None, :]
    kl = np.arange(QB)[:, None]
    iq = np.arange(QB)[None, :]

    def tile(dist):
        onehot = (_bucket_np(dist)[..., None] == np.arange(N_BUCKETS)).astype(np.float32)
        t = jnp.einsum("kib,bh->khi", jnp.asarray(onehot), shifted, precision=lax.Precision.HIGHEST)
        return t.reshape(dist.shape[0], NSA_HEADS * QB)

    bd = tile(iq - kl)
    bd1 = tile(iq + QB - kl)
    cc = np.arange(32)[:, None]
    bc = tile(iq - CMP_STRIDE * (cc - 16) - (CMP_LEN - 1))
    return bd, bd1, bc


def _overlap_t(seq):
    nc = seq // CMP_STRIDE
    ns = seq // SLC_LEN
    c_lo = np.arange(nc)[None, :] * CMP_STRIDE
    s_lo = np.arange(ns)[:, None] * SLC_LEN
    ov = np.clip(np.minimum(c_lo + CMP_LEN, s_lo + SLC_LEN) - np.maximum(c_lo, s_lo), 0, None) / CMP_LEN
    ov[:, nc - 1] = 0.0
    return jnp.asarray(ov, BF16)


def _rope_tables(seq):
    pos = jnp.arange(seq, dtype=F32)
    inv_freq = ROPE_THETA ** (-jnp.arange(0, MLA_ROPE, 2, dtype=F32) / MLA_ROPE)
    ang = pos[:, None] * inv_freq[None, :]
    cos = jnp.concatenate([jnp.cos(ang)] * 2, axis=1)
    sin = jnp.concatenate([jnp.sin(ang)] * 2, axis=1)
    place = lambda a: jnp.pad(a, ((0, 0), (MLA_NOPE, 128 - MLA_NOPE - MLA_ROPE)))
    return place(cos), place(sin), cos.T, sin.T


def _rot_cols(w):
    half = MLA_ROPE // 2
    return jnp.concatenate([-w[..., half:], w[..., :half]], axis=-1)


def _layer_weights(w_in, conv_dw, cmp_pos, cmp_w1, cmp_w2, w_uq, w_ukv, pool_w):
    kv = lambda i: w_in[:, _C_KV + 64 * i:_C_KV + 64 * (i + 1)]
    kr = w_in[:, _C_KR:_C_KR + MLA_ROPE]
    place = lambda a: jnp.pad(a, ((0, 0), (MLA_NOPE, 128 - MLA_NOPE - MLA_ROPE)))
    wrow = jnp.concatenate([
        w_in[:, _C_CONV:_C_CONV + 512], w_in[:, _C_POOL:_C_POOL + 256], kv(0), kv(1),
        jnp.pad(kv(2), ((0, 0), (0, 64))), jnp.pad(kv(4), ((0, 0), (0, 64))),
        w_in[:, _C_CQ:_C_CQ + Q_LORA], w_in[:, _C_CKV:_C_CKV + KV_LORA], place(kr), place(_rot_cols(kr))],
        axis=1).astype(BF16)
    wt = jnp.concatenate([
        w_in[:, _C_Q:_C_Q + 256].T * (NSA_HD ** -0.5), kv(3).T, kv(5).T,
        jnp.pad(w_in[:, _C_G:_C_G + 12].T, ((0, 4), (0, 0)))], axis=0).astype(BF16)

    dw = jnp.pad(conv_dw, ((0, 1), (0, 0)))

    w1r = cmp_w1.reshape(2, CMP_LEN, NSA_HD, 128)
    zk = jnp.zeros_like(w1r[0])
    w1kv = jnp.concatenate([jnp.concatenate([w1r[0], zk], axis=-1),
                            jnp.concatenate([zk, w1r[1]], axis=-1)], axis=1)
    w1a = w1kv[:16].reshape(16 * 128, 256).astype(BF16)
    w1b = w1kv[16:].reshape(16 * 128, 256).astype(BF16)
    pos_kv = jnp.concatenate([cmp_pos[0], cmp_pos[1]], axis=-1)
    posa = pos_kv[:16].reshape(1, 16 * 128)
    posb = pos_kv[16:].reshape(1, 16 * 128)
    z2 = jnp.zeros_like(cmp_w2[0])
    w2bd = jnp.concatenate([jnp.concatenate([cmp_w2[0], z2], axis=1),
                            jnp.concatenate([z2, cmp_w2[1]], axis=1)], axis=0).astype(BF16)

    wq = w_uq.reshape(Q_LORA, MLA_HEADS, MLA_NOPE + MLA_ROPE)
    wqt = jnp.pad(wq, ((0, 0), (0, 0), (0, 128 - MLA_NOPE - MLA_ROPE))).reshape(Q_LORA, 512).T.astype(BF16)
    wqpt = _rot_cols(wq[:, :, MLA_NOPE:]).reshape(Q_LORA, MLA_HEADS * MLA_ROPE).T.astype(BF16)
    wkv = w_ukv.reshape(KV_LORA, MLA_HEADS, MLA_NOPE + MLA_V)
    wk = jnp.pad(wkv[:, :, :MLA_NOPE], ((0, 0), (0, 0), (0, 128 - MLA_NOPE))).reshape(KV_LORA, 512).astype(BF16)
    wvt = wkv[:, :, MLA_NOPE:].reshape(KV_LORA, MLA_HEADS * MLA_V).T.astype(BF16)

    poolw = jax.scipy.linalg.block_diag(*[pool_w[g] for g in range(len(POOL_WINDOWS))]).astype(BF16)
    return dict(wrow=wrow, wt=wt, dw=dw, w1a=w1a, w1b=w1b, posa=posa, posb=posb, w2bd=w2bd,
                wqt=wqt, wqpt=wqpt, wk=wk, wvt=wvt, poolw=poolw)


def kernel(x, w_in, w_out, ln_mix_g, ln_mlp_g, conv_dw, conv_dw_b, conv_ln_g, conv_ln_b, conv_pw,
           nsa_cmp_pos, nsa_cmp_w1, nsa_cmp_w2, mla_q_norm_g, mla_w_uq, mla_kv_norm_g, mla_w_ukv,
           pool_w, pool_scale, mlp_w1, mlp_w2, rel_bias_table, final_norm_g):
    batch, seq, _ = x.shape
    depth = w_in.shape[0]
    bd, bd1, bc = _bias_tiles(rel_bias_table)
    ovl_t = _overlap_t(seq)
    cosp, sinp, cost, sint = _rope_tables(seq)
    r1 = lambda v: v.reshape(1, -1)
    h = x.reshape(batch * seq, D_MODEL)
    for l in range(depth):
        w = _layer_weights(w_in[l], conv_dw[l], nsa_cmp_pos[l], nsa_cmp_w1[l], nsa_cmp_w2[l],
                           mla_w_uq[l], mla_w_ukv[l], pool_w[l])
        mla_args = (r1(mla_q_norm_g[l]), r1(mla_kv_norm_g[l]), w["wqt"], w["wqpt"], w["wk"], w["wvt"],
                    cosp, sinp, cost, sint)
        loc_args = (w["dw"], r1(conv_dw_b[l]), r1(conv_ln_g[l]), r1(conv_ln_b[l]), conv_pw[l].astype(BF16),
                    w["poolw"], r1(pool_scale[l]))
        yc, yp, kcvc, ks, kw, km, qtm, vtm, qt, vt, gt = _proj(
            h, r1(ln_mix_g[l]), w["wrow"], w["wt"], mla_args, loc_args, batch=batch, seq=seq)
        kvc, kvct = _compress(kcvc, w["posa"], w["posb"], w["w1a"], w["w1b"], w["w2bd"], batch=batch, seq=seq)
        yn = _nsa(kvc, kvct, ks, kw, vt, qt, gt, ovl_t, bd, bd1, bc, batch=batch, seq=seq)
        ym = _mla_attn(km.reshape(batch, seq, 512), qtm, vtm, batch=batch, seq=seq)
        h = _out_mlp(h, yc, yn, ym, yp, w_out[l].astype(BF16), r1(ln_mlp_g[l]), mlp_w1[l].astype(BF16),
                     mlp_w2[l].astype(BF16), r1(final_norm_g), final=(l == depth - 1))
    return h.reshape(batch, seq, D_MODEL)
```

```python
import functools
import math

import numpy as np
import jax
import jax.numpy as jnp
from jax import lax
from jax.experimental import pallas as pl
from jax.experimental.pallas import tpu as pltpu

F32 = jnp.float32
BF16 = jnp.bfloat16

D_MODEL = 1024
GROUP_W = 256
CONV_K = 31
NSA_HEADS = 4
NSA_HD = 64
CMP_LEN = 32
CMP_STRIDE = 16
SLC_LEN = 64
N_SELECT = 16
WINDOW = 512
MLA_HEADS = 4
MLA_NOPE = 64
MLA_ROPE = 32
MLA_V = 64
Q_LORA = 256
KV_LORA = 128
ROPE_THETA = 10000.0
POOL_WINDOWS = (2, 4, 8, 16)
D_FF = 4 * D_MODEL
N_BUCKETS = 32
MAX_DIST = 128
EPS = 1e-6
NEG_INF = -1e30
FORCE_SCORE = 1e9
LOG2E = math.log2(math.e)

LANES = 128
SUBLANES = 8
QB = 128
HALO = 32
FAR_TILE = 512
FAR_BLOCKS = FAR_TILE // SLC_LEN
MASK_ROWS = 16
VT_ROWS = 80
MLA_TILE = 512
VMEM_LIMIT = 56 * 1024 * 1024

_C_CONV, _C_Q, _C_KV, _C_G, _C_CQ, _C_CKV, _C_KR, _C_POOL = 0, 512, 768, 1152, 1164, 1420, 1548, 1580


def _cparams(sem):
    return pltpu.CompilerParams(dimension_semantics=sem, vmem_limit_bytes=VMEM_LIMIT)


def _rms(x, g):
    ms = jnp.mean(x * x, axis=-1, keepdims=True)
    return x * lax.rsqrt(ms + EPS) * g


def _dot(a, b):
    return jnp.dot(a, b, preferred_element_type=F32)


def _dot_nt(a, b):
    return lax.dot_general(a, b, (((1,), (1,)), ((), ())), preferred_element_type=F32)


def _const_spec(shape):
    nd = len(shape)
    return pl.BlockSpec(shape, lambda *_: (0,) * nd)


def _proj_kernel(x_ref, g_ref, wrow_ref, wt_ref, *rest, tile, sub, steps_per_seq):
    mla = rest[:10]
    loc = rest[10:17]
    yc_ref, yp_ref, kcvc_ref, ks_ref, kw_ref, kmla_ref, qtmla_ref, vtmla_ref, qt_ref, vt_ref, gt_ref = rest[17:28]
    scratch = rest[28:]
    step = pl.program_id(0)
    seq_step = step % steps_per_seq

    @pl.when(step == 0)
    def _():
        scratch[0][sub:sub + HALO, :] = jnp.zeros((HALO, GROUP_W), F32)
        scratch[1][sub:sub + HALO, :] = jnp.zeros((HALO, GROUP_W), F32)

    ones = jnp.ones((VT_ROWS - NSA_HD, LANES), BF16)
    lane = lax.broadcasted_iota(jnp.int32, (sub, LANES), 1)
    for r in range(tile // sub):
        rows = slice(r * sub, (r + 1) * sub)
        xn = _rms(x_ref[rows, :], g_ref[...]).astype(BF16)
        first = (seq_step == 0) if r == 0 else False
        _local_tile(_dot(xn, wrow_ref[:, 0:512]), _dot(xn, wrow_ref[:, 512:768]), first,
                    seq_step * tile + r * sub, loc, rows, yc_ref, yp_ref, scratch)
        kk = _dot(xn, wrow_ref[:, 768:1024])
        kcvc_ref[rows, :] = kk[:, 0:LANES]
        tok = pl.program_id(0) * tile + r * sub + lax.broadcasted_iota(jnp.int32, (sub, LANES), 0)
        onehot = (lane - NSA_HD == (tok // SLC_LEN) % FAR_BLOCKS).astype(F32)
        ks_ref[rows, :] = (kk[:, LANES:2 * LANES] + onehot).astype(BF16)
        km = _dot(xn, wrow_ref[:, 1024:1792])
        kw_ref[rows, :] = km[:, 0:LANES].astype(BF16)
        _mla_prep_tile(km[:, LANES:], mla, rows, kmla_ref, qtmla_ref, vtmla_ref.at[r])
        qt_ref[:, rows] = _dot_nt(wt_ref[0:256, :], xn).astype(BF16)
        vs = _dot_nt(wt_ref[256:384, :], xn).astype(BF16)
        for j in range(sub // LANES):
            cols = slice(j * LANES, (j + 1) * LANES)
            jt = r * (sub // LANES) + j
            vt_ref[jt, 0:NSA_HD, :] = vs[0:NSA_HD, cols]
            vt_ref[jt, NSA_HD:VT_ROWS, :] = ones
            vt_ref[jt, VT_ROWS:VT_ROWS + NSA_HD, :] = vs[NSA_HD:2 * NSA_HD, cols]
            vt_ref[jt, VT_ROWS + NSA_HD:2 * VT_ROWS, :] = ones
        gt_ref[:, rows] = _dot_nt(wt_ref[384:400, :], xn)


def _proj(h, g, wrow, wt, mla_args, loc_args, *, batch, seq, tile=1024):
    qg, kg, wqt, wqpt, wk, wvt, cosp, sinp, cost, sint = mla_args
    dw, dwb, lng, lnb, pw, poolw, pscale = loc_args
    sub = MLA_TILE
    n = h.shape[0]
    spb = seq // tile
    mt = tile // MLA_TILE
    outs = (
        jax.ShapeDtypeStruct((n, GROUP_W), BF16),
        jax.ShapeDtypeStruct((n, GROUP_W), BF16),
        jax.ShapeDtypeStruct((n, 128), F32),
        jax.ShapeDtypeStruct((n, 128), BF16),
        jax.ShapeDtypeStruct((n, 128), BF16),
        jax.ShapeDtypeStruct((n, 512), BF16),
        jax.ShapeDtypeStruct((batch, 512, seq), BF16),
        jax.ShapeDtypeStruct((batch, seq // MLA_TILE, MLA_HEADS * VT_ROWS, MLA_TILE), BF16),
        jax.ShapeDtypeStruct((256, n), BF16),
        jax.ShapeDtypeStruct((n // LANES, 2 * VT_ROWS, LANES), BF16),
        jax.ShapeDtypeStruct((16, n), F32),
    )
    row = lambda w: pl.BlockSpec((tile, w), lambda i: (i, 0))
    return pl.pallas_call(
        functools.partial(_proj_kernel, tile=tile, sub=sub, steps_per_seq=spb),
        grid=(n // tile,),
        in_specs=[row(D_MODEL), _const_spec((1, D_MODEL)), _const_spec(wrow.shape), _const_spec(wt.shape),
                  _const_spec((1, Q_LORA)), _const_spec((1, KV_LORA)),
                  _const_spec(wqt.shape), _const_spec(wqpt.shape), _const_spec(wk.shape), _const_spec(wvt.shape),
                  pl.BlockSpec((tile, 128), lambda i: (i % spb, 0)), pl.BlockSpec((tile, 128), lambda i: (i % spb, 0)),
                  pl.BlockSpec((MLA_ROPE, tile), lambda i: (0, i % spb)),
                  pl.BlockSpec((MLA_ROPE, tile), lambda i: (0, i % spb)),
                  _const_spec(dw.shape), _const_spec((1, GROUP_W)), _const_spec((1, GROUP_W)),
                  _const_spec((1, GROUP_W)), _const_spec((GROUP_W, GROUP_W)),
                  _const_spec((GROUP_W, GROUP_W)), _const_spec((1, GROUP_W))],
        out_specs=(row(GROUP_W), row(GROUP_W), row(128), row(128), row(128), row(512),
                   pl.BlockSpec((None, 512, tile), lambda i: (i // spb, 0, i % spb)),
                   pl.BlockSpec((None, mt, MLA_HEADS * VT_ROWS, MLA_TILE), lambda i: (i // spb, i % spb, 0, 0)),
                   pl.BlockSpec((256, tile), lambda i: (0, i)),
                   pl.BlockSpec((tile // LANES, 2 * VT_ROWS, LANES), lambda i: (i, 0, 0)),
                   pl.BlockSpec((16, tile), lambda i: (0, i))),
        out_shape=outs,
        scratch_shapes=[pltpu.VMEM((HALO + sub, GROUP_W), F32)] * 4
        + [pltpu.VMEM((SUBLANES - 1, HALO + sub, GROUP_W), F32)],
        compiler_params=_cparams(("arbitrary",)),
        name="in_proj",
    )(h, g, wrow, wt, *mla_args, *loc_args)


def _local_tile(uc, up, first, pos0, loc, out_rows, yc_ref, yp_ref, scratch, *, rows=128):
    dw_ref, dwb_ref, lng_ref, lnb_ref, pw_ref, poolw_ref, pscale_ref = loc
    hbuf, ubuf, sa, sb, hshift = scratch
    tile = uc.shape[0]
    hbuf[0:HALO, :] = jnp.where(first, 0.0, hbuf[tile:tile + HALO, :])
    ubuf[0:HALO, :] = jnp.where(first, 0.0, ubuf[tile:tile + HALO, :])
    hbuf[HALO:HALO + tile, :] = uc[:, 0:GROUP_W] * jax.nn.sigmoid(uc[:, GROUP_W:2 * GROUP_W])
    span = HALO + tile - SUBLANES
    for j in range(1, SUBLANES):
        hshift[j - 1, 0:span, :] = hbuf[pl.ds(j, span), :]
    dwb = dwb_ref[...]
    for c in range(tile // rows):
        acc = jnp.zeros((rows, GROUP_W), F32) + dwb
        for k in range(CONV_K):
            off = c * rows + HALO - (CONV_K - 1) + k
            phase, base = off % SUBLANES, off - off % SUBLANES
            tap = hbuf[pl.ds(base, rows), :] if phase == 0 else hshift[phase - 1, pl.ds(base, rows), :]
            acc = acc + dw_ref[k:k + 1, :] * tap
        mu = jnp.mean(acc, axis=-1, keepdims=True)
        cen = acc - mu
        var = jnp.mean(cen * cen, axis=-1, keepdims=True)
        y = cen * lax.rsqrt(var + EPS) * lng_ref[...] + lnb_ref[...]
        y = y * jax.nn.sigmoid(y)
        r0 = out_rows.start + c * rows
        yc_ref[r0:r0 + rows, :] = _dot(y.astype(BF16), pw_ref[...]).astype(BF16)

    ext = HALO + tile
    ubuf[HALO:ext, :] = up
    sa[pl.ds(8, ext - 8), :] = ubuf[pl.ds(8, ext - 8), :] + ubuf[pl.ds(7, ext - 8), :]
    sb[pl.ds(16, ext - 16), :] = sa[pl.ds(16, ext - 16), :] + sa[pl.ds(14, ext - 16), :]
    s2 = sa[pl.ds(HALO, tile), :]
    s4 = sb[pl.ds(HALO, tile), :]
    sa[pl.ds(24, ext - 24), :] = sb[pl.ds(24, ext - 24), :] + sb[pl.ds(20, ext - 24), :]
    s8 = sa[pl.ds(HALO, tile), :]
    s16 = s8 + sa[pl.ds(HALO - 8, tile), :]
    lane = lax.broadcasted_iota(jnp.int32, (tile, GROUP_W), 1)
    grp = lane // (GROUP_W // len(POOL_WINDOWS))
    wsum = jnp.where(grp == 0, s2, jnp.where(grp == 1, s4, jnp.where(grp == 2, s8, s16)))
    width = jnp.where(grp == 0, 2, jnp.where(grp == 1, 4, jnp.where(grp == 2, 8, 16)))
    pos = pos0 + lax.broadcasted_iota(jnp.int32, (tile, GROUP_W), 0)
    cnt = jnp.minimum(width, pos + 1).astype(F32)
    d = wsum / cnt - up
    yp_ref[out_rows, :] = (_dot(d.astype(BF16), poolw_ref[...]) * pscale_ref[...]).astype(BF16)


def _compress_kernel(kr_ref, posa_ref, posb_ref, w1a_ref, w1b_ref, w2_ref, kv_ref, kvt_ref, bbuf, *, nc):
    kr = jnp.concatenate([kr_ref[pl.ds(l, nc, stride=CMP_STRIDE), :] for l in range(CMP_STRIDE)], axis=1)
    a = _dot((kr + posa_ref[...]).astype(BF16), w1a_ref[...])
    b = _dot((kr + posb_ref[...]).astype(BF16), w1b_ref[...])
    bbuf[0:nc, :] = b
    bbuf[nc:nc + 8, :] = jnp.zeros((8, 256), F32)
    hdn = a + bbuf[pl.ds(1, nc), :]
    hdn = hdn * jax.nn.sigmoid(hdn)
    kv = _dot(hdn.astype(BF16), w2_ref[...])
    kv_ref[...] = kv.astype(BF16)
    kvt_ref[...] = kv.T.astype(BF16)


def _compress(kcvc, posa, posb, w1a, w1b, w2bd, *, batch, seq):
    nc = seq // CMP_STRIDE
    kr = kcvc.reshape(batch, seq, 128)
    return pl.pallas_call(
        functools.partial(_compress_kernel, nc=nc),
        grid=(batch,),
        in_specs=[pl.BlockSpec((None, seq, 128), lambda b: (b, 0, 0)),
                  _const_spec(posa.shape), _const_spec(posb.shape), _const_spec(w1a.shape),
                  _const_spec(w1b.shape), _const_spec(w2bd.shape)],
        out_specs=(pl.BlockSpec((None, nc, 128), lambda b: (b, 0, 0)),
                   pl.BlockSpec((None, 128, nc), lambda b: (b, 0, 0))),
        out_shape=(jax.ShapeDtypeStruct((batch, nc, 128), BF16),
                   jax.ShapeDtypeStruct((batch, 128, nc), BF16)),
        scratch_shapes=[pltpu.VMEM((nc + 8, 256), F32)],
        compiler_params=_cparams(("parallel",)),
        name="nsa_compress",
    )(kr, posa, posb, w1a, w1b, w2bd)


def _nsa_kernel(kvc_ref, kvct_ref, ks_ref, kw_ref, vt_ref, qt_ref, gt_ref, ovl_ref, bd_ref, bd1_ref, bc_ref,
                y_ref, s_ref, key_ref, tie_ref, sel_ref, mb_ref, m_ref, acc_ref, mn_ref, accn_ref, oc_ref, ow_ref, sa_ref, sb_ref, cma_ref, cmb_ref,
                *, nc, ns, n_top):
    qb = pl.program_id(1)
    hq = NSA_HEADS * QB
    qt = jnp.concatenate([qt_ref[h * NSA_HD:(h + 1) * NSA_HD, :] for h in range(NSA_HEADS)], axis=1)
    qts = jnp.concatenate([qt, jnp.zeros_like(qt)], axis=0)
    lane_q = lax.broadcasted_iota(jnp.int32, (1, hq), 1) % QB

    tq = qb * QB + lane_q
    pad = 16
    sub8 = lax.broadcasted_iota(jnp.int32, (8, QB), 0)

    def select(nrows, nblk):
        s_ref[0:pad, :] = jnp.zeros((pad, hq), F32)
        s_ref[pad + nrows:pad + nrows + pad, :] = jnp.zeros((pad, hq), F32)
        s_ref[pad:pad + nrows, :] = _dot(kvc_ref[0:nrows, :], qts)
        w0 = pl.multiple_of(qb * 8, 8)
        s_ref[pl.ds(w0, 32), :] = s_ref[pl.ds(w0, 32), :] + bc_ref[...]
        sc = s_ref[pad:pad + nrows, :]
        cidx = lax.broadcasted_iota(jnp.int32, (nrows, hq), 0)
        last_c = jnp.minimum(lax.shift_right_arithmetic(tq - (CMP_LEN - 1), 4), nc - 2)
        sc = jnp.where(cidx <= last_c, sc, NEG_INF)
        mc = jnp.max(sc, axis=0, keepdims=True)
        pc = jnp.exp(sc - mc)
        lc = jnp.sum(pc, axis=0, keepdims=True)
        pc = pc * jnp.where(last_c >= 0, 1.0 / lc, 0.0)
        oc_ref[...] = _dot(kvct_ref[NSA_HD:2 * NSA_HD, 0:nrows], pc.astype(BF16))

        psum = pc[:, 0:QB]
        for h in range(1, NSA_HEADS):
            psum = psum + pc[:, h * QB:(h + 1) * QB]
        p_hi = psum.astype(BF16)
        p_lo = (psum - p_hi.astype(F32)).astype(BF16)
        ovl = ovl_ref[0:nblk, 0:nrows]
        imp = _dot(ovl, p_hi) + _dot(ovl, p_lo)
        jidx = lax.broadcasted_iota(jnp.int32, (nblk, QB), 0)
        iq = lax.broadcasted_iota(jnp.int32, (nblk, QB), 1)
        jc = 2 * qb + (iq >= SLC_LEN).astype(jnp.int32)
        forced = (jidx == 0) | (jidx == jc) | (jidx == jc - 1)
        imp = jnp.where(forced, FORCE_SCORE, imp)
        imp = jnp.where(jidx <= jc, imp, NEG_INF)
        key = pltpu.bitcast(imp, jnp.int32)
        key_ref[0:nblk, :] = key
        tie_ref[0:nblk, :] = jnp.zeros((nblk, QB), jnp.int32)
        n_vreg = nblk // 8

        def rank_body(g, cnts):
            base = pl.multiple_of(g * 8, 8)
            rows = key_ref[pl.ds(base, 8), :]
            out = []
            for tv in range(n_vreg):
                target = key[tv * 8:(tv + 1) * 8, :] - jnp.where(tv > g, 1, 0)
                c = cnts[tv]
                for r in range(8):
                    c = c + (rows[r:r + 1, :] > target).astype(jnp.int32)
                out.append(c)
            ties = jnp.zeros((8, QB), jnp.int32)
            for r in range(8):
                ties = ties + ((rows[r:r + 1, :] == rows) & (sub8 > r)).astype(jnp.int32)
            tie_ref[pl.ds(base, 8), :] = ties
            return tuple(out)

        n_groups = jnp.minimum((2 * qb + 9) // 8, n_vreg)
        cnts = lax.fori_loop(0, n_groups, rank_body, tuple(jnp.zeros((8, QB), jnp.int32) for _ in range(n_vreg)))
        cnt = jnp.concatenate(cnts, axis=0) + tie_ref[0:nblk, :]
        chosen = (cnt < n_top) & (imp > 0.5 * NEG_INF)
        sel_ref[0:nblk, :] = jnp.where(chosen, 1.0, 0.0)
        mb_ref[0:nblk, :] = jnp.where(chosen & (jidx < 2 * (qb - 1)), 0.0, NEG_INF)
        if nblk < ns:
            sel_ref[nblk:ns, :] = jnp.zeros((ns - nblk, QB), F32)
            mb_ref[nblk:ns, :] = jnp.full((ns - nblk, QB), NEG_INF, F32)

    nqb = nc // (QB // CMP_STRIDE)
    parts = next(p for p in (4, 2, 1) if nc % (p * LANES) == 0 and ns % (p * SUBLANES) == 0)
    for k in range(parts):
        in_part = (qb >= k * nqb // parts) & (qb < (k + 1) * nqb // parts)
        pl.when(in_part)(functools.partial(select, nc * (k + 1) // parts, ns * (k + 1) // parts))

    m_ref[...] = jnp.full((1, hq), NEG_INF, F32)
    acc_ref[...] = jnp.zeros((VT_ROWS, hq), F32)
    per_far = FAR_TILE // QB
    n_far = (jnp.maximum(qb - 1, 0) + per_far - 1) // per_far
    zpad = jnp.zeros((QB - NSA_HD - MASK_ROWS, hq), BF16)

    def q_aug(t):
        rows = mb_ref[pl.ds(pl.multiple_of(t * FAR_BLOCKS, FAR_BLOCKS), FAR_BLOCKS), :]
        rows = jnp.concatenate([rows] * NSA_HEADS, axis=1)
        rows = jnp.concatenate([rows, jnp.zeros((MASK_ROWS - FAR_BLOCKS, hq), F32)], axis=0).astype(BF16)
        return jnp.concatenate([qt, rows, zpad], axis=0)

    def produce(t, s_x, cm_x):
        s = _dot(ks_ref[pl.ds(pl.multiple_of(t * FAR_TILE, FAR_TILE), FAR_TILE), :], q_aug(t))
        s_x[...] = s
        cm_x[...] = jnp.max(s, axis=0, keepdims=True)

    def soften(s_x, cm_x):
        m_old = m_ref[...]
        m_new = jnp.maximum(m_old, cm_x[...])
        m_ref[...] = m_new
        return jnp.exp(s_x[...] - m_new).astype(BF16), jnp.exp(m_old - m_new)

    def accumulate(t, p, alpha):
        sub = FAR_TILE // QB
        vt = jnp.concatenate([vt_ref[sub * t + j, 0:VT_ROWS, :] for j in range(sub)], axis=1)
        acc_ref[...] = alpha * acc_ref[...] + _dot(vt, p)

    def consume(t, s_x, cm_x):
        accumulate(t, *soften(s_x, cm_x))

    produce(0, sa_ref, cma_ref)

    kl = lax.broadcasted_iota(jnp.int32, (QB, hq), 0)
    causal = kl <= lane_q
    anti = kl > lane_q
    n_w = WINDOW // QB
    kts = [jnp.maximum(qb - back, 0) for back in range(n_w + 1)]
    live = [qb >= back for back in range(n_w + 1)]

    def krows(ref, kt):
        return ref[pl.ds(pl.multiple_of(kt * QB, QB), QB), :]

    def sel_mask(kt):
        m0 = jnp.broadcast_to(sel_ref[pl.ds(2 * kt, 1), :], (SLC_LEN, QB))
        m1 = jnp.broadcast_to(sel_ref[pl.ds(2 * kt + 1, 1), :], (SLC_LEN, QB))
        mt = jnp.concatenate([m0, m1], axis=0)
        return jnp.concatenate([mt] * NSA_HEADS, axis=1) > 0.5

    def softmax_pv(tiles, row0):
        m = jnp.max(tiles[0], axis=0, keepdims=True)
        for s in tiles[1:]:
            m = jnp.maximum(m, jnp.max(s, axis=0, keepdims=True))
        acc = jnp.zeros((VT_ROWS, hq), F32)
        for back, s in enumerate(tiles):
            acc = acc + _dot(vt_ref[kts[back], row0:row0 + VT_ROWS, :], jnp.exp(s - m).astype(BF16))
        return m, acc

    s_win = [_dot(krows(kw_ref, kts[back]), qts) for back in range(n_w + 1)]
    s_sel = [_dot(krows(ks_ref, kts[back]), qts) for back in range(2)]
    s_win[0] = jnp.where(causal, s_win[0] + bd_ref[...], NEG_INF)
    s_win[1] = jnp.where(live[1], s_win[1] + bd1_ref[...], NEG_INF)
    for back in range(2, n_w):
        s_win[back] = jnp.where(live[back], s_win[back], NEG_INF)
    s_win[n_w] = jnp.where(anti & live[n_w], s_win[n_w], NEG_INF)
    s_sel[0] = jnp.where(sel_mask(qb) & causal, s_sel[0] + bd_ref[...], NEG_INF)
    s_sel[1] = jnp.where(sel_mask(kts[1]) & live[1], s_sel[1] + bd1_ref[...], NEG_INF)
    _, acc_w = softmax_pv(s_win, VT_ROWS)
    ow_ref[...] = acc_w[0:NSA_HD, :] * (1.0 / acc_w[NSA_HD:NSA_HD + 1, :])
    mn_ref[...], accn_ref[...] = softmax_pv(s_sel, 0)

    def pair_body(j, carry):
        produce(2 * j + 1, sb_ref, cmb_ref)
        consume(2 * j, sa_ref, cma_ref)
        produce(jnp.minimum(2 * j + 2, n_far - 1), sa_ref, cma_ref)
        consume(2 * j + 1, sb_ref, cmb_ref)
        return carry

    lax.fori_loop(0, n_far // 2, pair_body, 0)

    @pl.when(n_far % 2 == 1)
    def _():
        consume(n_far - 1, sa_ref, cma_ref)

    m_far = m_ref[...]
    m_near = mn_ref[...]
    m_all = jnp.maximum(m_far, m_near)
    acc = jnp.exp(m_far - m_all) * acc_ref[...] + jnp.exp(m_near - m_all) * accn_ref[...]
    osel = acc[0:NSA_HD, :] * (1.0 / acc[NSA_HD:NSA_HD + 1, :])

    gates = jax.nn.sigmoid(gt_ref[...])
    oc = oc_ref[...]
    ow = ow_ref[...]
    parts = []
    for h in range(NSA_HEADS):
        sl = slice(h * QB, (h + 1) * QB)
        parts.append(gates[3 * h:3 * h + 1, :] * oc[:, sl] + gates[3 * h + 1:3 * h + 2, :] * osel[:, sl]
                     + gates[3 * h + 2:3 * h + 3, :] * ow[:, sl])
    y_ref[...] = jnp.concatenate(parts, axis=0).T.astype(BF16)


def _nsa(kvc, kvct, ks, kw, vt, qt, gt, ovl_t, bd, bd1, bc, *, batch, seq):
    nc = seq // CMP_STRIDE
    ns = seq // SLC_LEN
    nqb = seq // QB
    hq = NSA_HEADS * QB
    ks3 = ks.reshape(batch, seq, 128)
    kw3 = kw.reshape(batch, seq, 128)
    vt4 = vt.reshape(batch, nqb, 2 * VT_ROWS, LANES)
    return pl.pallas_call(
        functools.partial(_nsa_kernel, nc=nc, ns=ns, n_top=min(N_SELECT, ns)),
        grid=(batch, nqb),
        in_specs=[pl.BlockSpec((None, nc, 128), lambda b, q: (b, 0, 0)),
                  pl.BlockSpec((None, 128, nc), lambda b, q: (b, 0, 0)),
                  pl.BlockSpec((None, seq, 128), lambda b, q: (b, 0, 0)),
                  pl.BlockSpec((None, seq, 128), lambda b, q: (b, 0, 0)),
                  pl.BlockSpec((None, nqb, 2 * VT_ROWS, LANES), lambda b, q: (b, 0, 0, 0)),
                  pl.BlockSpec((256, QB), lambda b, q: (0, b * nqb + q)),
                  pl.BlockSpec((16, QB), lambda b, q: (0, b * nqb + q)),
                  _const_spec(ovl_t.shape), _const_spec(bd.shape), _const_spec(bd1.shape),
                  _const_spec(bc.shape)],
        out_specs=pl.BlockSpec((QB, 256), lambda b, q: (b * nqb + q, 0)),
        out_shape=jax.ShapeDtypeStruct((batch * seq, 256), BF16),
        scratch_shapes=[pltpu.VMEM((nc + 32, hq), F32),
                        pltpu.VMEM((ns, QB), jnp.int32),
                        pltpu.VMEM((ns, QB), jnp.int32),
                        pltpu.VMEM((ns, QB), F32),
                        pltpu.VMEM((ns, QB), F32),
                        pltpu.VMEM((1, hq), F32),
                        pltpu.VMEM((VT_ROWS, hq), F32),
                        pltpu.VMEM((1, hq), F32),
                        pltpu.VMEM((VT_ROWS, hq), F32),
                        pltpu.VMEM((NSA_HD, hq), F32),
                        pltpu.VMEM((NSA_HD, hq), F32),
                        *[pltpu.VMEM((FAR_TILE, hq), F32)] * 2,
                        *[pltpu.VMEM((1, hq), F32)] * 2],
        compiler_params=_cparams(("parallel", "parallel")),
        name="nsa_attention",
    )(kvc, kvct, ks3, kw3, vt4, qt, gt, ovl_t, bd, bd1, bc)


def _mla_prep_tile(u, mla, rows, k_ref, qt_ref, vt_ref):
    qg_ref, kg_ref, wqt_ref, wqpt_ref, wk_ref, wvt_ref, cosp_ref, sinp_ref, cost_ref, sint_ref = mla
    scale = (MLA_NOPE + MLA_ROPE) ** -0.5 * LOG2E
    cqn = _rms(u[:, 0:Q_LORA], qg_ref[...]).astype(BF16)
    ckvn = _rms(u[:, Q_LORA:Q_LORA + KV_LORA], kg_ref[...]).astype(BF16)
    qta = _dot_nt(wqt_ref[...], cqn)
    qtp = _dot_nt(wqpt_ref[...], cqn)
    cos_t = cost_ref[:, rows]
    sin_t = sint_ref[:, rows]
    n_tok = u.shape[0]
    for h in range(MLA_HEADS):
        r0 = h * 128
        qt_ref[r0:r0 + MLA_NOPE, rows] = (qta[r0:r0 + MLA_NOPE, :] * scale).astype(BF16)
        rope = qta[r0 + MLA_NOPE:r0 + MLA_NOPE + MLA_ROPE, :] * cos_t + qtp[h * MLA_ROPE:(h + 1) * MLA_ROPE, :] * sin_t
        qt_ref[r0 + MLA_NOPE:r0 + MLA_NOPE + MLA_ROPE, rows] = (rope * scale).astype(BF16)
        qt_ref[r0 + MLA_NOPE + MLA_ROPE:r0 + 128, rows] = jnp.zeros((128 - MLA_NOPE - MLA_ROPE, n_tok), BF16)
    krope = u[:, 384:512] * cosp_ref[rows, :] + u[:, 512:640] * sinp_ref[rows, :]
    knope = _dot(ckvn, wk_ref[...])
    for h in range(MLA_HEADS):
        k_ref[rows, h * 128:(h + 1) * 128] = (knope[:, h * 128:(h + 1) * 128] + krope).astype(BF16)
    vt = _dot_nt(wvt_ref[...], ckvn).astype(BF16)
    for h in range(MLA_HEADS):
        vt_ref[h * VT_ROWS:h * VT_ROWS + MLA_V, :] = vt[h * MLA_V:(h + 1) * MLA_V, :]
        vt_ref[h * VT_ROWS + MLA_V:(h + 1) * VT_ROWS, :] = jnp.ones((VT_ROWS - MLA_V, n_tok), BF16)


def _mla_attn_kernel(k_ref, vt_ref, qt_ref, o_ref, m_ref, acc_ref, sa_ref, sb_ref, cma_ref, cmb_ref, *, tile):
    qi = pl.program_id(1)
    heads = range(MLA_HEADS)
    m_ref[...] = jnp.full(m_ref.shape, NEG_INF, F32)
    acc_ref[...] = jnp.zeros(acc_ref.shape, F32)

    def scores(t, h):
        hs = slice(h * 128, (h + 1) * 128)
        return _dot(k_ref[pl.ds(pl.multiple_of(t * tile, tile), tile), hs], qt_ref[hs, :])

    def produce(t, s_x, cm_x):
        for h in heads:
            s = scores(t, h)
            s_x[h] = s
            cm_x[h:h + 1, :] = jnp.max(s, axis=0, keepdims=True)

    def consume(t, tiles, maxima):
        probs, alphas = [], []
        for h in heads:
            m_old = m_ref[h:h + 1, :]
            m_new = jnp.maximum(m_old, maxima[h])
            probs.append(jnp.exp2(tiles[h] - m_new).astype(BF16))
            alphas.append(jnp.exp2(m_old - m_new))
            m_ref[h:h + 1, :] = m_new
        for h in heads:
            acc_ref[h] = alphas[h] * acc_ref[h] + _dot(vt_ref[t, h * VT_ROWS:(h + 1) * VT_ROWS, :], probs[h])

    def consume_from(t, s_x, cm_x):
        consume(t, [s_x[h] for h in heads], [cm_x[h:h + 1, :] for h in heads])

    produce(0, sa_ref, cma_ref)
    kl = lax.broadcasted_iota(jnp.int32, (tile, tile), 0)
    iq = lax.broadcasted_iota(jnp.int32, (tile, tile), 1)
    diag = [jnp.where(kl <= iq, scores(qi, h), NEG_INF) for h in heads]
    consume(qi, diag, [jnp.max(s, axis=0, keepdims=True) for s in diag])

    def pair_body(j, carry):
        produce(2 * j + 1, sb_ref, cmb_ref)
        consume_from(2 * j, sa_ref, cma_ref)
        produce(jnp.minimum(2 * j + 2, qi - 1), sa_ref, cma_ref)
        consume_from(2 * j + 1, sb_ref, cmb_ref)
        return carry

    lax.fori_loop(0, qi // 2, pair_body, 0)

    @pl.when(qi % 2 == 1)
    def _():
        consume_from(qi - 1, sa_ref, cma_ref)

    outs = [acc_ref[h, 0:MLA_V, :] * (1.0 / acc_ref[h, MLA_V:MLA_V + 1, :]) for h in heads]
    o_ref[...] = jnp.concatenate(outs, axis=0).T.astype(BF16)


def _mla_attn(k, qt, vt, *, batch, seq, tile=MLA_TILE):
    nt = seq // tile
    y = pl.pallas_call(
        functools.partial(_mla_attn_kernel, tile=tile),
        grid=(batch, nt),
        in_specs=[pl.BlockSpec((None, seq, 512), lambda b, q: (b, 0, 0)),
                  pl.BlockSpec((None, nt, MLA_HEADS * VT_ROWS, tile), lambda b, q: (b, 0, 0, 0)),
                  pl.BlockSpec((None, 512, tile), lambda b, q: (b, 0, q))],
        out_specs=pl.BlockSpec((None, tile, 256), lambda b, q: (b, q, 0)),
        out_shape=jax.ShapeDtypeStruct((batch, seq, 256), BF16),
        scratch_shapes=[pltpu.VMEM((8, tile), F32),
                        pltpu.VMEM((MLA_HEADS, VT_ROWS, tile), F32),
                        *[pltpu.VMEM((MLA_HEADS, tile, tile), F32)] * 2,
                        *[pltpu.VMEM((8, tile), F32)] * 2],
        compiler_params=_cparams(("parallel", "parallel")),
        name="mla_attention",
    )(k, vt, qt)
    return y.reshape(batch * seq, 256)


def _mlp_kernel(h_ref, yc_ref, yn_ref, ym_ref, yp_ref, wo_ref, g_ref, w1_ref, w2_ref, fg_ref, o_ref, *, final, chunk):
    y = jnp.concatenate([yc_ref[...], yn_ref[...], ym_ref[...], yp_ref[...]], axis=1)
    h2 = h_ref[...] + _dot(y, wo_ref[...])
    xn = _rms(h2, g_ref[...]).astype(BF16)
    acc = h2
    for c in range(D_FF // chunk):
        z = _dot(xn, w1_ref[:, c * chunk:(c + 1) * chunk])
        a = jnp.square(jnp.maximum(z, 0.0)).astype(BF16)
        acc = acc + _dot(a, w2_ref[c * chunk:(c + 1) * chunk, :])
    if final:
        acc = _rms(acc, fg_ref[...])
    o_ref[...] = acc


def _out_mlp(h, yc, yn, ym, yp, wo, g, w1, w2, fg, *, final, tile=1024, chunk=1024):
    n = h.shape[0]
    row = lambda w: pl.BlockSpec((tile, w), lambda i: (i, 0))
    once = lambda shape: pl.BlockSpec(shape, lambda i: (0,) * len(shape), pipeline_mode=pl.Buffered(1))
    return pl.pallas_call(
        functools.partial(_mlp_kernel, final=final, chunk=chunk),
        grid=(n // tile,),
        in_specs=[row(D_MODEL), row(256), row(256), row(256), row(256),
                  once(wo.shape), _const_spec((1, D_MODEL)), once(w1.shape), once(w2.shape),
                  _const_spec((1, D_MODEL))],
        out_specs=row(D_MODEL),
        out_shape=jax.ShapeDtypeStruct((n, D_MODEL), F32),
        compiler_params=_cparams(("parallel",)),
        name="out_mlp",
    )(h, yc, yn, ym, yp, wo, g, w1, w2, fg)


def _bucket_np(n):
    n = np.maximum(n, 0)
    max_exact = N_BUCKETS // 2
    nf = np.maximum(n, 1).astype(np.float32)
    large = max_exact + (np.log(nf / np.float32(max_exact)) / np.float32(math.log(MAX_DIST / max_exact))
                         * np.float32(N_BUCKETS - max_exact)).astype(np.int32)
    large = np.minimum(large, N_BUCKETS - 1)
    return np.where(n < max_exact, n, large).astype(np.int32)


def _bias_tiles(rel_table):
    shifted = rel_table - rel_table[N_BUCKETS - 1][None, :]
    kl = np.arange(QB)[:, None]
    iq = np.arange(QB)[None, :]

    def tile(dist):
        onehot = (_bucket_np(dist)[..., None] == np.arange(N_BUCKETS)).astype(np.float32)
        t = jnp.einsum("kib,bh->khi", jnp.asarray(onehot), shifted, precision=lax.Precision.HIGHEST)
        return t.reshape(dist.shape[0], NSA_HEADS * QB)

    bd = tile(iq - kl)
    bd1 = tile(iq + QB - kl)
    cc = np.arange(32)[:, None]
    bc = tile(iq - CMP_STRIDE * (cc - 16) - (CMP_LEN - 1))
    return bd, bd1, bc


def _overlap_t(seq):
    nc = seq // CMP_STRIDE
    ns = seq // SLC_LEN
    c_lo = np.arange(nc)[None, :] * CMP_STRIDE
    s_lo = np.arange(ns)[:, None] * SLC_LEN
    ov = np.clip(np.minimum(c_lo + CMP_LEN, s_lo + SLC_LEN) - np.maximum(c_lo, s_lo), 0, None) / CMP_LEN
    ov[:, nc - 1] = 0.0
    return jnp.asarray(ov, BF16)


def _rope_tables(seq):
    pos = jnp.arange(seq, dtype=F32)
    inv_freq = ROPE_THETA ** (-jnp.arange(0, MLA_ROPE, 2, dtype=F32) / MLA_ROPE)
    ang = pos[:, None] * inv_freq[None, :]
    cos = jnp.concatenate([jnp.cos(ang)] * 2, axis=1)
    sin = jnp.concatenate([jnp.sin(ang)] * 2, axis=1)
    place = lambda a: jnp.pad(a, ((0, 0), (MLA_NOPE, 128 - MLA_NOPE - MLA_ROPE)))
    return place(cos), place(sin), cos.T, sin.T


def _rot_cols(w):
    half = MLA_ROPE // 2
    return jnp.concatenate([-w[..., half:], w[..., :half]], axis=-1)


def _layer_weights(w_in, conv_dw, cmp_pos, cmp_w1, cmp_w2, w_uq, w_ukv, pool_w):
    kv = lambda i: w_in[:, _C_KV + 64 * i:_C_KV + 64 * (i + 1)]
    kr = w_in[:, _C_KR:_C_KR + MLA_ROPE]
    place = lambda a: jnp.pad(a, ((0, 0), (MLA_NOPE, 128 - MLA_NOPE - MLA_ROPE)))
    wrow = jnp.concatenate([
        w_in[:, _C_CONV:_C_CONV + 512], w_in[:, _C_POOL:_C_POOL + 256], kv(0), kv(1),
        jnp.pad(kv(2), ((0, 0), (0, 64))), jnp.pad(kv(4), ((0, 0), (0, 64))),
        w_in[:, _C_CQ:_C_CQ + Q_LORA], w_in[:, _C_CKV:_C_CKV + KV_LORA], place(kr), place(_rot_cols(kr))],
        axis=1).astype(BF16)
    wt = jnp.concatenate([
        w_in[:, _C_Q:_C_Q + 256].T * (NSA_HD ** -0.5), kv(3).T, kv(5).T,
        jnp.pad(w_in[:, _C_G:_C_G + 12].T, ((0, 4), (0, 0)))], axis=0).astype(BF16)

    dw = jnp.pad(conv_dw, ((0, 1), (0, 0)))

    w1r = cmp_w1.reshape(2, CMP_LEN, NSA_HD, 128)
    zk = jnp.zeros_like(w1r[0])
    w1kv = jnp.concatenate([jnp.concatenate([w1r[0], zk], axis=-1),
                            jnp.concatenate([zk, w1r[1]], axis=-1)], axis=1)
    w1a = w1kv[:16].reshape(16 * 128, 256).astype(BF16)
    w1b = w1kv[16:].reshape(16 * 128, 256).astype(BF16)
    pos_kv = jnp.concatenate([cmp_pos[0], cmp_pos[1]], axis=-1)
    posa = pos_kv[:16].reshape(1, 16 * 128)
    posb = pos_kv[16:].reshape(1, 16 * 128)
    z2 = jnp.zeros_like(cmp_w2[0])
    w2bd = jnp.concatenate([jnp.concatenate([cmp_w2[0], z2], axis=1),
                            jnp.concatenate([z2, cmp_w2[1]], axis=1)], axis=0).astype(BF16)

    wq = w_uq.reshape(Q_LORA, MLA_HEADS, MLA_NOPE + MLA_ROPE)
    wqt = jnp.pad(wq, ((0, 0), (0, 0), (0, 128 - MLA_NOPE - MLA_ROPE))).reshape(Q_LORA, 512).T.astype(BF16)
    wqpt = _rot_cols(wq[:, :, MLA_NOPE:]).reshape(Q_LORA, MLA_HEADS * MLA_ROPE).T.astype(BF16)
    wkv = w_ukv.reshape(KV_LORA, MLA_HEADS, MLA_NOPE + MLA_V)
    wk = jnp.pad(wkv[:, :, :MLA_NOPE], ((0, 0), (0, 0), (0, 128 - MLA_NOPE))).reshape(KV_LORA, 512).astype(BF16)
    wvt = wkv[:, :, MLA_NOPE:].reshape(KV_LORA, MLA_HEADS * MLA_V).T.astype(BF16)

    poolw = jax.scipy.linalg.block_diag(*[pool_w[g] for g in range(len(POOL_WINDOWS))]).astype(BF16)
    return dict(wrow=wrow, wt=wt, dw=dw, w1a=w1a, w1b=w1b, posa=posa, posb=posb, w2bd=w2bd,
                wqt=wqt, wqpt=wqpt, wk=wk, wvt=wvt, poolw=poolw)


def kernel(x, w_in, w_out, ln_mix_g, ln_mlp_g, conv_dw, conv_dw_b, conv_ln_g, conv_ln_b, conv_pw,
           nsa_cmp_pos, nsa_cmp_w1, nsa_cmp_w2, mla_q_norm_g, mla_w_uq, mla_kv_norm_g, mla_w_ukv,
           pool_w, pool_scale, mlp_w1, mlp_w2, rel_bias_table, final_norm_g):
    batch, seq, _ = x.shape
    depth = w_in.shape[0]
    bd, bd1, bc = _bias_tiles(rel_bias_table)
    ovl_t = _overlap_t(seq)
    cosp, sinp, cost, sint = _rope_tables(seq)
    r1 = lambda v: v.reshape(1, -1)
    h = x.reshape(batch * seq, D_MODEL)
    for l in range(depth):
        w = _layer_weights(w_in[l], conv_dw[l], nsa_cmp_pos[l], nsa_cmp_w1[l], nsa_cmp_w2[l],
                           mla_w_uq[l], mla_w_ukv[l], pool_w[l])
        mla_args = (r1(mla_q_norm_g[l]), r1(mla_kv_norm_g[l]), w["wqt"], w["wqpt"], w["wk"], w["wvt"],
                    cosp, sinp, cost, sint)
        loc_args = (w["dw"], r1(conv_dw_b[l]), r1(conv_ln_g[l]), r1(conv_ln_b[l]), conv_pw[l].astype(BF16),
                    w["poolw"], r1(pool_scale[l]))
        yc, yp, kcvc, ks, kw, km, qtm, vtm, qt, vt, gt = _proj(
            h, r1(ln_mix_g[l]), w["wrow"], w["wt"], mla_args, loc_args, batch=batch, seq=seq)
        kvc, kvct = _compress(kcvc, w["posa"], w["posb"], w["w1a"], w["w1b"], w["w2bd"], batch=batch, seq=seq)
        yn = _nsa(kvc, kvct, ks, kw, vt, qt, gt, ovl_t, bd, bd1, bc, batch=batch, seq=seq)
        ym = _mla_attn(km.reshape(batch, seq, 512), qtm, vtm, batch=batch, seq=seq)
        h = _out_mlp(h, yc, yn, ym, yp, w_out[l].astype(BF16), r1(ln_mlp_g[l]), mlp_w1[l].astype(BF16),
                     mlp_w2[l].astype(BF16), r1(final_norm_g), final=(l == depth - 1))
    return h.reshape(batch, seq, D_MODEL)
```

```python
import functools
import math

import numpy as np
import jax
import jax.numpy as jnp
from jax import lax
from jax.experimental import pallas as pl
from jax.experimental.pallas import tpu as pltpu

F32 = jnp.float32
BF16 = jnp.bfloat16

D_MODEL = 1024
GROUP_W = 256
CONV_K = 31
NSA_HEADS = 4
NSA_HD = 64
CMP_LEN = 32
CMP_STRIDE = 16
SLC_LEN = 64
N_SELECT = 16
WINDOW = 512
MLA_HEADS = 4
MLA_NOPE = 64
MLA_ROPE = 32
MLA_V = 64
Q_LORA = 256
KV_LORA = 128
ROPE_THETA = 10000.0
POOL_WINDOWS = (2, 4, 8, 16)
D_FF = 4 * D_MODEL
N_BUCKETS = 32
MAX_DIST = 128
EPS = 1e-6
NEG_INF = -1e30
FORCE_SCORE = 1e9
LOG2E = math.log2(math.e)

LANES = 128
SUBLANES = 8
QB = 128
HALO = 32
FAR_TILE = 512
FAR_BLOCKS = FAR_TILE // SLC_LEN
MASK_ROWS = 16
VT_ROWS = 80
MLA_TILE = 512
VMEM_LIMIT = 56 * 1024 * 1024

_C_CONV, _C_Q, _C_KV, _C_G, _C_CQ, _C_CKV, _C_KR, _C_POOL = 0, 512, 768, 1152, 1164, 1420, 1548, 1580


def _cparams(sem):
    return pltpu.CompilerParams(dimension_semantics=sem, vmem_limit_bytes=VMEM_LIMIT)


def _rms(x, g):
    ms = jnp.mean(x * x, axis=-1, keepdims=True)
    return x * lax.rsqrt(ms + EPS) * g


def _dot(a, b):
    return jnp.dot(a, b, preferred_element_type=F32)


def _dot_nt(a, b):
    return lax.dot_general(a, b, (((1,), (1,)), ((), ())), preferred_element_type=F32)


def _const_spec(shape):
    nd = len(shape)
    return pl.BlockSpec(shape, lambda *_: (0,) * nd)


def _proj_kernel(x_ref, g_ref, wrow_ref, wt_ref, *rest, tile, sub, steps_per_seq):
    mla = rest[:10]
    loc = rest[10:17]
    yc_ref, yp_ref, kcvc_ref, ks_ref, kw_ref, kmla_ref, qtmla_ref, vtmla_ref, qt_ref, vt_ref, gt_ref = rest[17:28]
    scratch = rest[28:]
    step = pl.program_id(0)
    seq_step = step % steps_per_seq

    @pl.when(step == 0)
    def _():
        scratch[0][sub:sub + HALO, :] = jnp.zeros((HALO, GROUP_W), F32)
        scratch[1][sub:sub + HALO, :] = jnp.zeros((HALO, GROUP_W), F32)

    ones = jnp.ones((VT_ROWS - NSA_HD, LANES), BF16)
    lane = lax.broadcasted_iota(jnp.int32, (sub, LANES), 1)
    mixers = []
    for r in range(tile // sub):
        rows = slice(r * sub, (r + 1) * sub)
        xn = _rms(x_ref[rows, :], g_ref[...]).astype(BF16)
        uc = _dot(xn, wrow_ref[:, 0:512])
        up = _dot(xn, wrow_ref[:, 512:768])
        kk = _dot(xn, wrow_ref[:, 768:1024])
        kcvc_ref[rows, :] = kk[:, 0:LANES]
        tok = pl.program_id(0) * tile + r * sub + lax.broadcasted_iota(jnp.int32, (sub, LANES), 0)
        onehot = (lane - NSA_HD == (tok // SLC_LEN) % FAR_BLOCKS).astype(F32)
        ks_ref[rows, :] = (kk[:, LANES:2 * LANES] + onehot).astype(BF16)
        km = _dot(xn, wrow_ref[:, 1024:1792])
        kw_ref[rows, :] = km[:, 0:LANES].astype(BF16)
        qt_ref[:, rows] = _dot_nt(wt_ref[0:256, :], xn).astype(BF16)
        vs = _dot_nt(wt_ref[256:384, :], xn).astype(BF16)
        for j in range(sub // LANES):
            cols = slice(j * LANES, (j + 1) * LANES)
            jt = r * (sub // LANES) + j
            vt_ref[jt, 0:NSA_HD, :] = vs[0:NSA_HD, cols]
            vt_ref[jt, NSA_HD:VT_ROWS, :] = ones
            vt_ref[jt, VT_ROWS:VT_ROWS + NSA_HD, :] = vs[NSA_HD:2 * NSA_HD, cols]
            vt_ref[jt, VT_ROWS + NSA_HD:2 * VT_ROWS, :] = ones
        gt_ref[:, rows] = _dot_nt(wt_ref[384:400, :], xn)
        _mla_prep_tile(km[:, LANES:], mla, rows, kmla_ref, qtmla_ref, vtmla_ref.at[r])
        mixers.append((uc, up, r, rows))
    for uc, up, r, rows in mixers:
        first = (seq_step == 0) if r == 0 else False
        _local_tile(uc, up, first, seq_step * tile + r * sub, loc, rows, yc_ref, yp_ref, scratch)


def _proj(h, g, wrow, wt, mla_args, loc_args, *, batch, seq, tile=1024):
    qg, kg, wqt, wqpt, wk, wvt, cosp, sinp, cost, sint = mla_args
    dw, dwb, lng, lnb, pw, poolw, pscale = loc_args
    sub = MLA_TILE
    n = h.shape[0]
    spb = seq // tile
    mt = tile // MLA_TILE
    outs = (
        jax.ShapeDtypeStruct((n, GROUP_W), BF16),
        jax.ShapeDtypeStruct((n, GROUP_W), BF16),
        jax.ShapeDtypeStruct((n, 128), F32),
        jax.ShapeDtypeStruct((n, 128), BF16),
        jax.ShapeDtypeStruct((n, 128), BF16),
        jax.ShapeDtypeStruct((n, 512), BF16),
        jax.ShapeDtypeStruct((batch, 512, seq), BF16),
        jax.ShapeDtypeStruct((batch, seq // MLA_TILE, MLA_HEADS * VT_ROWS, MLA_TILE), BF16),
        jax.ShapeDtypeStruct((256, n), BF16),
        jax.ShapeDtypeStruct((n // LANES, 2 * VT_ROWS, LANES), BF16),
        jax.ShapeDtypeStruct((16, n), F32),
    )
    row = lambda w: pl.BlockSpec((tile, w), lambda i: (i, 0))
    return pl.pallas_call(
        functools.partial(_proj_kernel, tile=tile, sub=sub, steps_per_seq=spb),
        grid=(n // tile,),
        in_specs=[row(D_MODEL), _const_spec((1, D_MODEL)), _const_spec(wrow.shape), _const_spec(wt.shape),
                  _const_spec((1, Q_LORA)), _const_spec((1, KV_LORA)),
                  _const_spec(wqt.shape), _const_spec(wqpt.shape), _const_spec(wk.shape), _const_spec(wvt.shape),
                  pl.BlockSpec((tile, 128), lambda i: (i % spb, 0)), pl.BlockSpec((tile, 128), lambda i: (i % spb, 0)),
                  pl.BlockSpec((MLA_ROPE, tile), lambda i: (0, i % spb)),
                  pl.BlockSpec((MLA_ROPE, tile), lambda i: (0, i % spb)),
                  _const_spec(dw.shape), _const_spec((1, GROUP_W)), _const_spec((1, GROUP_W)),
                  _const_spec((1, GROUP_W)), _const_spec((GROUP_W, GROUP_W)),
                  _const_spec((GROUP_W, GROUP_W)), _const_spec((1, GROUP_W))],
        out_specs=(row(GROUP_W), row(GROUP_W), row(128), row(128), row(128), row(512),
                   pl.BlockSpec((None, 512, tile), lambda i: (i // spb, 0, i % spb)),
                   pl.BlockSpec((None, mt, MLA_HEADS * VT_ROWS, MLA_TILE), lambda i: (i // spb, i % spb, 0, 0)),
                   pl.BlockSpec((256, tile), lambda i: (0, i)),
                   pl.BlockSpec((tile // LANES, 2 * VT_ROWS, LANES), lambda i: (i, 0, 0)),
                   pl.BlockSpec((16, tile), lambda i: (0, i))),
        out_shape=outs,
        scratch_shapes=[pltpu.VMEM((HALO + sub, GROUP_W), F32)] * 4
        + [pltpu.VMEM((SUBLANES - 1, HALO + sub, GROUP_W), F32)],
        compiler_params=_cparams(("arbitrary",)),
        name="in_proj",
    )(h, g, wrow, wt, *mla_args, *loc_args)


def _local_tile(uc, up, first, pos0, loc, out_rows, yc_ref, yp_ref, scratch, *, rows=128):
    dw_ref, dwb_ref, lng_ref, lnb_ref, pw_ref, poolw_ref, pscale_ref = loc
    hbuf, ubuf, sa, sb, hshift = scratch
    tile = uc.shape[0]
    hbuf[0:HALO, :] = jnp.where(first, 0.0, hbuf[tile:tile + HALO, :])
    ubuf[0:HALO, :] = jnp.where(first, 0.0, ubuf[tile:tile + HALO, :])
    hbuf[HALO:HALO + tile, :] = uc[:, 0:GROUP_W] * jax.nn.sigmoid(uc[:, GROUP_W:2 * GROUP_W])
    span = HALO + tile - SUBLANES
    for j in range(1, SUBLANES):
        hshift[j - 1, 0:span, :] = hbuf[pl.ds(j, span), :]
    dwb = dwb_ref[...]
    for c in range(tile // rows):
        acc = jnp.zeros((rows, GROUP_W), F32) + dwb
        for k in range(CONV_K):
            off = c * rows + HALO - (CONV_K - 1) + k
            phase, base = off % SUBLANES, off - off % SUBLANES
            tap = hbuf[pl.ds(base, rows), :] if phase == 0 else hshift[phase - 1, pl.ds(base, rows), :]
            acc = acc + dw_ref[k:k + 1, :] * tap
        mu = jnp.mean(acc, axis=-1, keepdims=True)
        cen = acc - mu
        var = jnp.mean(cen * cen, axis=-1, keepdims=True)
        y = cen * lax.rsqrt(var + EPS) * lng_ref[...] + lnb_ref[...]
        y = y * jax.nn.sigmoid(y)
        r0 = out_rows.start + c * rows
        yc_ref[r0:r0 + rows, :] = _dot(y.astype(BF16), pw_ref[...]).astype(BF16)

    ext = HALO + tile
    ubuf[HALO:ext, :] = up
    sa[pl.ds(8, ext - 8), :] = ubuf[pl.ds(8, ext - 8), :] + ubuf[pl.ds(7, ext - 8), :]
    sb[pl.ds(16, ext - 16), :] = sa[pl.ds(16, ext - 16), :] + sa[pl.ds(14, ext - 16), :]
    s2 = sa[pl.ds(HALO, tile), :]
    s4 = sb[pl.ds(HALO, tile), :]
    sa[pl.ds(24, ext - 24), :] = sb[pl.ds(24, ext - 24), :] + sb[pl.ds(20, ext - 24), :]
    s8 = sa[pl.ds(HALO, tile), :]
    s16 = s8 + sa[pl.ds(HALO - 8, tile), :]
    lane = lax.broadcasted_iota(jnp.int32, (tile, GROUP_W), 1)
    grp = lane // (GROUP_W // len(POOL_WINDOWS))
    wsum = jnp.where(grp == 0, s2, jnp.where(grp == 1, s4, jnp.where(grp == 2, s8, s16)))
    width = jnp.where(grp == 0, 2, jnp.where(grp == 1, 4, jnp.where(grp == 2, 8, 16)))
    pos = pos0 + lax.broadcasted_iota(jnp.int32, (tile, GROUP_W), 0)
    cnt = jnp.minimum(width, pos + 1).astype(F32)
    d = wsum / cnt - up
    yp_ref[out_rows, :] = (_dot(d.astype(BF16), poolw_ref[...]) * pscale_ref[...]).astype(BF16)


def _compress_kernel(kr_ref, posa_ref, posb_ref, w1a_ref, w1b_ref, w2_ref, kv_ref, kvt_ref, bbuf, *, nc):
    kr = jnp.concatenate([kr_ref[pl.ds(l, nc, stride=CMP_STRIDE), :] for l in range(CMP_STRIDE)], axis=1)
    a = _dot((kr + posa_ref[...]).astype(BF16), w1a_ref[...])
    b = _dot((kr + posb_ref[...]).astype(BF16), w1b_ref[...])
    bbuf[0:nc, :] = b
    bbuf[nc:nc + 8, :] = jnp.zeros((8, 256), F32)
    hdn = a + bbuf[pl.ds(1, nc), :]
    hdn = hdn * jax.nn.sigmoid(hdn)
    kv = _dot(hdn.astype(BF16), w2_ref[...])
    kv_ref[...] = kv.astype(BF16)
    kvt_ref[...] = kv.T.astype(BF16)


def _compress(kcvc, posa, posb, w1a, w1b, w2bd, *, batch, seq):
    nc = seq // CMP_STRIDE
    kr = kcvc.reshape(batch, seq, 128)
    return pl.pallas_call(
        functools.partial(_compress_kernel, nc=nc),
        grid=(batch,),
        in_specs=[pl.BlockSpec((None, seq, 128), lambda b: (b, 0, 0)),
                  _const_spec(posa.shape), _const_spec(posb.shape), _const_spec(w1a.shape),
                  _const_spec(w1b.shape), _const_spec(w2bd.shape)],
        out_specs=(pl.BlockSpec((None, nc, 128), lambda b: (b, 0, 0)),
                   pl.BlockSpec((None, 128, nc), lambda b: (b, 0, 0))),
        out_shape=(jax.ShapeDtypeStruct((batch, nc, 128), BF16),
                   jax.ShapeDtypeStruct((batch, 128, nc), BF16)),
        scratch_shapes=[pltpu.VMEM((nc + 8, 256), F32)],
        compiler_params=_cparams(("parallel",)),
        name="nsa_compress",
    )(kr, posa, posb, w1a, w1b, w2bd)


def _nsa_kernel(kvc_ref, kvct_ref, ks_ref, kw_ref, vt_ref, qt_ref, gt_ref, ovl_ref, bd_ref, bd1_ref, bc_ref,
                y_ref, s_ref, key_ref, tie_ref, sel_ref, mb_ref, m_ref, acc_ref, mn_ref, accn_ref, oc_ref, ow_ref, sa_ref, sb_ref, cma_ref, cmb_ref,
                *, nc, ns, n_top):
    qb = pl.program_id(1)
    hq = NSA_HEADS * QB
    qt = jnp.concatenate([qt_ref[h * NSA_HD:(h + 1) * NSA_HD, :] for h in range(NSA_HEADS)], axis=1)
    qts = jnp.concatenate([qt, jnp.zeros_like(qt)], axis=0)
    lane_q = lax.broadcasted_iota(jnp.int32, (1, hq), 1) % QB

    tq = qb * QB + lane_q
    pad = 16
    sub8 = lax.broadcasted_iota(jnp.int32, (8, QB), 0)

    def select(nrows, nblk):
        s_ref[0:pad, :] = jnp.zeros((pad, hq), F32)
        s_ref[pad + nrows:pad + nrows + pad, :] = jnp.zeros((pad, hq), F32)
        s_ref[pad:pad + nrows, :] = _dot(kvc_ref[0:nrows, :], qts)
        w0 = pl.multiple_of(qb * 8, 8)
        s_ref[pl.ds(w0, 32), :] = s_ref[pl.ds(w0, 32), :] + bc_ref[...]
        sc = s_ref[pad:pad + nrows, :]
        cidx = lax.broadcasted_iota(jnp.int32, (nrows, hq), 0)
        last_c = jnp.minimum(lax.shift_right_arithmetic(tq - (CMP_LEN - 1), 4), nc - 2)
        sc = jnp.where(cidx <= last_c, sc, NEG_INF)
        mc = jnp.max(sc, axis=0, keepdims=True)
        pc = jnp.exp(sc - mc)
        lc = jnp.sum(pc, axis=0, keepdims=True)
        pc = pc * jnp.where(last_c >= 0, 1.0 / lc, 0.0)
        oc_ref[...] = _dot(kvct_ref[NSA_HD:2 * NSA_HD, 0:nrows], pc.astype(BF16))

        psum = pc[:, 0:QB]
        for h in range(1, NSA_HEADS):
            psum = psum + pc[:, h * QB:(h + 1) * QB]
        p_hi = psum.astype(BF16)
        p_lo = (psum - p_hi.astype(F32)).astype(BF16)
        ovl = ovl_ref[0:nblk, 0:nrows]
        imp = _dot(ovl, p_hi) + _dot(ovl, p_lo)
        jidx = lax.broadcasted_iota(jnp.int32, (nblk, QB), 0)
        iq = lax.broadcasted_iota(jnp.int32, (nblk, QB), 1)
        jc = 2 * qb + (iq >= SLC_LEN).astype(jnp.int32)
        forced = (jidx == 0) | (jidx == jc) | (jidx == jc - 1)
        imp = jnp.where(forced, FORCE_SCORE, imp)
        imp = jnp.where(jidx <= jc, imp, NEG_INF)
        key = pltpu.bitcast(imp, jnp.int32)
        key_ref[0:nblk, :] = key
        tie_ref[0:nblk, :] = jnp.zeros((nblk, QB), jnp.int32)
        n_vreg = nblk // 8

        def rank_body(g, cnts):
            base = pl.multiple_of(g * 8, 8)
            rows = key_ref[pl.ds(base, 8), :]
            out = []
            for tv in range(n_vreg):
                target = key[tv * 8:(tv + 1) * 8, :] - jnp.where(tv > g, 1, 0)
                c = cnts[tv]
                for r in range(8):
                    c = c + (rows[r:r + 1, :] > target).astype(jnp.int32)
                out.append(c)
            ties = jnp.zeros((8, QB), jnp.int32)
            for r in range(8):
                ties = ties + ((rows[r:r + 1, :] == rows) & (sub8 > r)).astype(jnp.int32)
            tie_ref[pl.ds(base, 8), :] = ties
            return tuple(out)

        n_groups = jnp.minimum((2 * qb + 9) // 8, n_vreg)
        cnts = lax.fori_loop(0, n_groups, rank_body, tuple(jnp.zeros((8, QB), jnp.int32) for _ in range(n_vreg)))
        cnt = jnp.concatenate(cnts, axis=0) + tie_ref[0:nblk, :]
        chosen = (cnt < n_top) & (imp > 0.5 * NEG_INF)
        sel_ref[0:nblk, :] = jnp.where(chosen, 1.0, 0.0)
        mb_ref[0:nblk, :] = jnp.where(chosen & (jidx < 2 * (qb - 1)), 0.0, NEG_INF)
        if nblk < ns:
            sel_ref[nblk:ns, :] = jnp.zeros((ns - nblk, QB), F32)
            mb_ref[nblk:ns, :] = jnp.full((ns - nblk, QB), NEG_INF, F32)

    nqb = nc // (QB // CMP_STRIDE)
    parts = next(p for p in (4, 2, 1) if nc % (p * LANES) == 0 and ns % (p * SUBLANES) == 0)
    for k in range(parts):
        in_part = (qb >= k * nqb // parts) & (qb < (k + 1) * nqb // parts)
        pl.when(in_part)(functools.partial(select, nc * (k + 1) // parts, ns * (k + 1) // parts))

    m_ref[...] = jnp.full((1, hq), NEG_INF, F32)
    acc_ref[...] = jnp.zeros((VT_ROWS, hq), F32)
    per_far = FAR_TILE // QB
    n_far = (jnp.maximum(qb - 1, 0) + per_far - 1) // per_far
    zpad = jnp.zeros((QB - NSA_HD - MASK_ROWS, hq), BF16)

    def q_aug(t):
        rows = mb_ref[pl.ds(pl.multiple_of(t * FAR_BLOCKS, FAR_BLOCKS), FAR_BLOCKS), :]
        rows = jnp.concatenate([rows] * NSA_HEADS, axis=1)
        rows = jnp.concatenate([rows, jnp.zeros((MASK_ROWS - FAR_BLOCKS, hq), F32)], axis=0).astype(BF16)
        return jnp.concatenate([qt, rows, zpad], axis=0)

    def produce(t, s_x, cm_x):
        s = _dot(ks_ref[pl.ds(pl.multiple_of(t * FAR_TILE, FAR_TILE), FAR_TILE), :], q_aug(t))
        s_x[...] = s
        cm_x[...] = jnp.max(s, axis=0, keepdims=True)

    def soften(s_x, cm_x):
        m_old = m_ref[...]
        m_new = jnp.maximum(m_old, cm_x[...])
        m_ref[...] = m_new
        return jnp.exp(s_x[...] - m_new).astype(BF16), jnp.exp(m_old - m_new)

    def accumulate(t, p, alpha):
        sub = FAR_TILE // QB
        vt = jnp.concatenate([vt_ref[sub * t + j, 0:VT_ROWS, :] for j in range(sub)], axis=1)
        acc_ref[...] = alpha * acc_ref[...] + _dot(vt, p)

    def consume(t, s_x, cm_x):
        accumulate(t, *soften(s_x, cm_x))

    produce(0, sa_ref, cma_ref)

    kl = lax.broadcasted_iota(jnp.int32, (QB, hq), 0)
    causal = kl <= lane_q
    anti = kl > lane_q
    n_w = WINDOW // QB
    kts = [jnp.maximum(qb - back, 0) for back in range(n_w + 1)]
    live = [qb >= back for back in range(n_w + 1)]

    def krows(ref, kt):
        return ref[pl.ds(pl.multiple_of(kt * QB, QB), QB), :]

    def sel_mask(kt):
        m0 = jnp.broadcast_to(sel_ref[pl.ds(2 * kt, 1), :], (SLC_LEN, QB))
        m1 = jnp.broadcast_to(sel_ref[pl.ds(2 * kt + 1, 1), :], (SLC_LEN, QB))
        mt = jnp.concatenate([m0, m1], axis=0)
        return jnp.concatenate([mt] * NSA_HEADS, axis=1) > 0.5

    def softmax_pv(tiles, row0):
        m = jnp.max(tiles[0], axis=0, keepdims=True)
        for s in tiles[1:]:
            m = jnp.maximum(m, jnp.max(s, axis=0, keepdims=True))
        acc = jnp.zeros((VT_ROWS, hq), F32)
        for back, s in enumerate(tiles):
            acc = acc + _dot(vt_ref[kts[back], row0:row0 + VT_ROWS, :], jnp.exp(s - m).astype(BF16))
        return m, acc

    s_win = [_dot(krows(kw_ref, kts[back]), qts) for back in range(n_w + 1)]
    s_sel = [_dot(krows(ks_ref, kts[back]), qts) for back in range(2)]
    s_win[0] = jnp.where(causal, s_win[0] + bd_ref[...], NEG_INF)
    s_win[1] = jnp.where(live[1], s_win[1] + bd1_ref[...], NEG_INF)
    for back in range(2, n_w):
        s_win[back] = jnp.where(live[back], s_win[back], NEG_INF)
    s_win[n_w] = jnp.where(anti & live[n_w], s_win[n_w], NEG_INF)
    s_sel[0] = jnp.where(sel_mask(qb) & causal, s_sel[0] + bd_ref[...], NEG_INF)
    s_sel[1] = jnp.where(sel_mask(kts[1]) & live[1], s_sel[1] + bd1_ref[...], NEG_INF)
    _, acc_w = softmax_pv(s_win, VT_ROWS)
    ow_ref[...] = acc_w[0:NSA_HD, :] * (1.0 / acc_w[NSA_HD:NSA_HD + 1, :])
    mn_ref[...], accn_ref[...] = softmax_pv(s_sel, 0)

    def pair_body(j, carry):
        produce(2 * j + 1, sb_ref, cmb_ref)
        consume(2 * j, sa_ref, cma_ref)
        produce(jnp.minimum(2 * j + 2, n_far - 1), sa_ref, cma_ref)
        consume(2 * j + 1, sb_ref, cmb_ref)
        return carry

    lax.fori_loop(0, n_far // 2, pair_body, 0)

    @pl.when(n_far % 2 == 1)
    def _():
        consume(n_far - 1, sa_ref, cma_ref)

    m_far = m_ref[...]
    m_near = mn_ref[...]
    m_all = jnp.maximum(m_far, m_near)
    acc = jnp.exp(m_far - m_all) * acc_ref[...] + jnp.exp(m_near - m_all) * accn_ref[...]
    osel = acc[0:NSA_HD, :] * (1.0 / acc[NSA_HD:NSA_HD + 1, :])

    gates = jax.nn.sigmoid(gt_ref[...])
    oc = oc_ref[...]
    ow = ow_ref[...]
    parts = []
    for h in range(NSA_HEADS):
        sl = slice(h * QB, (h + 1) * QB)
        parts.append(gates[3 * h:3 * h + 1, :] * oc[:, sl] + gates[3 * h + 1:3 * h + 2, :] * osel[:, sl]
                     + gates[3 * h + 2:3 * h + 3, :] * ow[:, sl])
    y_ref[...] = jnp.concatenate(parts, axis=0).T.astype(BF16)


def _nsa(kvc, kvct, ks, kw, vt, qt, gt, ovl_t, bd, bd1, bc, *, batch, seq):
    nc = seq // CMP_STRIDE
    ns = seq // SLC_LEN
    nqb = seq // QB
    hq = NSA_HEADS * QB
    ks3 = ks.reshape(batch, seq, 128)
    kw3 = kw.reshape(batch, seq, 128)
    vt4 = vt.reshape(batch, nqb, 2 * VT_ROWS, LANES)
    return pl.pallas_call(
        functools.partial(_nsa_kernel, nc=nc, ns=ns, n_top=min(N_SELECT, ns)),
        grid=(batch, nqb),
        in_specs=[pl.BlockSpec((None, nc, 128), lambda b, q: (b, 0, 0)),
                  pl.BlockSpec((None, 128, nc), lambda b, q: (b, 0, 0)),
                  pl.BlockSpec((None, seq, 128), lambda b, q: (b, 0, 0)),
                  pl.BlockSpec((None, seq, 128), lambda b, q: (b, 0, 0)),
                  pl.BlockSpec((None, nqb, 2 * VT_ROWS, LANES), lambda b, q: (b, 0, 0, 0)),
                  pl.BlockSpec((256, QB), lambda b, q: (0, b * nqb + q)),
                  pl.BlockSpec((16, QB), lambda b, q: (0, b * nqb + q)),
                  _const_spec(ovl_t.shape), _const_spec(bd.shape), _const_spec(bd1.shape),
                  _const_spec(bc.shape)],
        out_specs=pl.BlockSpec((QB, 256), lambda b, q: (b * nqb + q, 0)),
        out_shape=jax.ShapeDtypeStruct((batch * seq, 256), BF16),
        scratch_shapes=[pltpu.VMEM((nc + 32, hq), F32),
                        pltpu.VMEM((ns, QB), jnp.int32),
                        pltpu.VMEM((ns, QB), jnp.int32),
                        pltpu.VMEM((ns, QB), F32),
                        pltpu.VMEM((ns, QB), F32),
                        pltpu.VMEM((1, hq), F32),
                        pltpu.VMEM((VT_ROWS, hq), F32),
                        pltpu.VMEM((1, hq), F32),
                        pltpu.VMEM((VT_ROWS, hq), F32),
                        pltpu.VMEM((NSA_HD, hq), F32),
                        pltpu.VMEM((NSA_HD, hq), F32),
                        *[pltpu.VMEM((FAR_TILE, hq), F32)] * 2,
                        *[pltpu.VMEM((1, hq), F32)] * 2],
        compiler_params=_cparams(("parallel", "parallel")),
        name="nsa_attention",
    )(kvc, kvct, ks3, kw3, vt4, qt, gt, ovl_t, bd, bd1, bc)


def _mla_prep_tile(u, mla, rows, k_ref, qt_ref, vt_ref):
    qg_ref, kg_ref, wqt_ref, wqpt_ref, wk_ref, wvt_ref, cosp_ref, sinp_ref, cost_ref, sint_ref = mla
    scale = (MLA_NOPE + MLA_ROPE) ** -0.5 * LOG2E
    cqn = _rms(u[:, 0:Q_LORA], qg_ref[...]).astype(BF16)
    ckvn = _rms(u[:, Q_LORA:Q_LORA + KV_LORA], kg_ref[...]).astype(BF16)
    qta = _dot_nt(wqt_ref[...], cqn)
    qtp = _dot_nt(wqpt_ref[...], cqn)
    cos_t = cost_ref[:, rows]
    sin_t = sint_ref[:, rows]
    n_tok = u.shape[0]
    for h in range(MLA_HEADS):
        r0 = h * 128
        qt_ref[r0:r0 + MLA_NOPE, rows] = (qta[r0:r0 + MLA_NOPE, :] * scale).astype(BF16)
        rope = qta[r0 + MLA_NOPE:r0 + MLA_NOPE + MLA_ROPE, :] * cos_t + qtp[h * MLA_ROPE:(h + 1) * MLA_ROPE, :] * sin_t
        qt_ref[r0 + MLA_NOPE:r0 + MLA_NOPE + MLA_ROPE, rows] = (rope * scale).astype(BF16)
        qt_ref[r0 + MLA_NOPE + MLA_ROPE:r0 + 128, rows] = jnp.zeros((128 - MLA_NOPE - MLA_ROPE, n_tok), BF16)
    krope = u[:, 384:512] * cosp_ref[rows, :] + u[:, 512:640] * sinp_ref[rows, :]
    knope = _dot(ckvn, wk_ref[...])
    for h in range(MLA_HEADS):
        k_ref[rows, h * 128:(h + 1) * 128] = (knope[:, h * 128:(h + 1) * 128] + krope).astype(BF16)
    vt = _dot_nt(wvt_ref[...], ckvn).astype(BF16)
    for h in range(MLA_HEADS):
        vt_ref[h * VT_ROWS:h * VT_ROWS + MLA_V, :] = vt[h * MLA_V:(h + 1) * MLA_V, :]
        vt_ref[h * VT_ROWS + MLA_V:(h + 1) * VT_ROWS, :] = jnp.ones((VT_ROWS - MLA_V, n_tok), BF16)


def _mla_attn_kernel(k_ref, vt_ref, qt_ref, o_ref, m_ref, acc_ref, sa_ref, sb_ref, cma_ref, cmb_ref, *, tile):
    qi = pl.program_id(1)
    heads = range(MLA_HEADS)
    m_ref[...] = jnp.full(m_ref.shape, NEG_INF, F32)
    acc_ref[...] = jnp.zeros(acc_ref.shape, F32)

    def scores(t, h):
        hs = slice(h * 128, (h + 1) * 128)
        return _dot(k_ref[pl.ds(pl.multiple_of(t * tile, tile), tile), hs], qt_ref[hs, :])

    def produce(t, s_x, cm_x):
        for h in heads:
            s = scores(t, h)
            s_x[h] = s
            cm_x[h:h + 1, :] = jnp.max(s, axis=0, keepdims=True)

    def consume(t, tiles, maxima):
        probs, alphas = [], []
        for h in heads:
            m_old = m_ref[h:h + 1, :]
            m_new = jnp.maximum(m_old, maxima[h])
            probs.append(jnp.exp2(tiles[h] - m_new).astype(BF16))
            alphas.append(jnp.exp2(m_old - m_new))
            m_ref[h:h + 1, :] = m_new
        for h in heads:
            acc_ref[h] = alphas[h] * acc_ref[h] + _dot(vt_ref[t, h * VT_ROWS:(h + 1) * VT_ROWS, :], probs[h])

    def consume_from(t, s_x, cm_x):
        consume(t, [s_x[h] for h in heads], [cm_x[h:h + 1, :] for h in heads])

    produce(0, sa_ref, cma_ref)
    kl = lax.broadcasted_iota(jnp.int32, (tile, tile), 0)
    iq = lax.broadcasted_iota(jnp.int32, (tile, tile), 1)
    diag = [jnp.where(kl <= iq, scores(qi, h), NEG_INF) for h in heads]
    consume(qi, diag, [jnp.max(s, axis=0, keepdims=True) for s in diag])

    def pair_body(j, carry):
        produce(2 * j + 1, sb_ref, cmb_ref)
        consume_from(2 * j, sa_ref, cma_ref)
        produce(jnp.minimum(2 * j + 2, qi - 1), sa_ref, cma_ref)
        consume_from(2 * j + 1, sb_ref, cmb_ref)
        return carry

    lax.fori_loop(0, qi // 2, pair_body, 0)

    @pl.when(qi % 2 == 1)
    def _():
        consume_from(qi - 1, sa_ref, cma_ref)

    outs = [acc_ref[h, 0:MLA_V, :] * (1.0 / acc_ref[h, MLA_V:MLA_V + 1, :]) for h in heads]
    o_ref[...] = jnp.concatenate(outs, axis=0).T.astype(BF16)


def _mla_attn(k, qt, vt, *, batch, seq, tile=MLA_TILE):
    nt = seq // tile
    y = pl.pallas_call(
        functools.partial(_mla_attn_kernel, tile=tile),
        grid=(batch, nt),
        in_specs=[pl.BlockSpec((None, seq, 512), lambda b, q: (b, 0, 0)),
                  pl.BlockSpec((None, nt, MLA_HEADS * VT_ROWS, tile), lambda b, q: (b, 0, 0, 0)),
                  pl.BlockSpec((None, 512, tile), lambda b, q: (b, 0, q))],
        out_specs=pl.BlockSpec((None, tile, 256), lambda b, q: (b, q, 0)),
        out_shape=jax.ShapeDtypeStruct((batch, seq, 256), BF16),
        scratch_shapes=[pltpu.VMEM((8, tile), F32),
                        pltpu.VMEM((MLA_HEADS, VT_ROWS, tile), F32),
                        *[pltpu.VMEM((MLA_HEADS, tile, tile), F32)] * 2,
                        *[pltpu.VMEM((8, tile), F32)] * 2],
        compiler_params=_cparams(("parallel", "parallel")),
        name="mla_attention",
    )(k, vt, qt)
    return y.reshape(batch * seq, 256)


def _mlp_kernel(h_ref, yc_ref, yn_ref, ym_ref, yp_ref, wo_ref, g_ref, w1_ref, w2_ref, fg_ref, o_ref, *, final, chunk):
    y = jnp.concatenate([yc_ref[...], yn_ref[...], ym_ref[...], yp_ref[...]], axis=1)
    h2 = h_ref[...] + _dot(y, wo_ref[...])
    xn = _rms(h2, g_ref[...]).astype(BF16)
    acc = h2
    for c in range(D_FF // chunk):
        z = _dot(xn, w1_ref[:, c * chunk:(c + 1) * chunk])
        a = jnp.square(jnp.maximum(z, 0.0)).astype(BF16)
        acc = acc + _dot(a, w2_ref[c * chunk:(c + 1) * chunk, :])
    if final:
        acc = _rms(acc, fg_ref[...])
    o_ref[...] = acc


def _out_mlp(h, yc, yn, ym, yp, wo, g, w1, w2, fg, *, final, tile=1024, chunk=1024):
    n = h.shape[0]
    row = lambda w: pl.BlockSpec((tile, w), lambda i: (i, 0))
    once = lambda shape: pl.BlockSpec(shape, lambda i: (0,) * len(shape), pipeline_mode=pl.Buffered(1))
    return pl.pallas_call(
        functools.partial(_mlp_kernel, final=final, chunk=chunk),
        grid=(n // tile,),
        in_specs=[row(D_MODEL), row(256), row(256), row(256), row(256),
                  once(wo.shape), _const_spec((1, D_MODEL)), once(w1.shape), once(w2.shape),
                  _const_spec((1, D_MODEL))],
        out_specs=row(D_MODEL),
        out_shape=jax.ShapeDtypeStruct((n, D_MODEL), F32),
        compiler_params=_cparams(("parallel",)),
        name="out_mlp",
    )(h, yc, yn, ym, yp, wo, g, w1, w2, fg)


def _bucket_np(n):
    n = np.maximum(n, 0)
    max_exact = N_BUCKETS // 2
    nf = np.maximum(n, 1).astype(np.float32)
    large = max_exact + (np.log(nf / np.float32(max_exact)) / np.float32(math.log(MAX_DIST / max_exact))
                         * np.float32(N_BUCKETS - max_exact)).astype(np.int32)
    large = np.minimum(large, N_BUCKETS - 1)
    return np.where(n < max_exact, n, large).astype(np.int32)


def _bias_tiles(rel_table):
    shifted = rel_table - rel_table[N_BUCKETS - 1][None, :]
    kl = np.arange(QB)[:, None]
    iq = np.arange(QB)[None, :]

    def tile(dist):
        onehot = (_bucket_np(dist)[..., None] == np.arange(N_BUCKETS)).astype(np.float32)
        t = jnp.einsum("kib,bh->khi", jnp.asarray(onehot), shifted, precision=lax.Precision.HIGHEST)
        return t.reshape(dist.shape[0], NSA_HEADS * QB)

    bd = tile(iq - kl)
    bd1 = tile(iq + QB - kl)
    cc = np.arange(32)[:, None]
    bc = tile(iq - CMP_STRIDE * (cc - 16) - (CMP_LEN - 1))
    return bd, bd1, bc


def _overlap_t(seq):
    nc = seq // CMP_STRIDE
    ns = seq // SLC_LEN
    c_lo = np.arange(nc)[None, :] * CMP_STRIDE
    s_lo = np.arange(ns)[:, None] * SLC_LEN
    ov = np.clip(np.minimum(c_lo + CMP_LEN, s_lo + SLC_LEN) - np.maximum(c_lo, s_lo), 0, None) / CMP_LEN
    ov[:, nc - 1] = 0.0
    return jnp.asarray(ov, BF16)


def _rope_tables(seq):
    pos = jnp.arange(seq, dtype=F32)
    inv_freq = ROPE_THETA ** (-jnp.arange(0, MLA_ROPE, 2, dtype=F32) / MLA_ROPE)
    ang = pos[:, None] * inv_freq[None, :]
    cos = jnp.concatenate([jnp.cos(ang)] * 2, axis=1)
    sin = jnp.concatenate([jnp.sin(ang)] * 2, axis=1)
    place = lambda a: jnp.pad(a, ((0, 0), (MLA_NOPE, 128 - MLA_NOPE - MLA_ROPE)))
    return place(cos), place(sin), cos.T, sin.T


def _rot_cols(w):
    half = MLA_ROPE // 2
    return jnp.concatenate([-w[..., half:], w[..., :half]], axis=-1)


def _layer_weights(w_in, conv_dw, cmp_pos, cmp_w1, cmp_w2, w_uq, w_ukv, pool_w):
    kv = lambda i: w_in[:, _C_KV + 64 * i:_C_KV + 64 * (i + 1)]
    kr = w_in[:, _C_KR:_C_KR + MLA_ROPE]
    place = lambda a: jnp.pad(a, ((0, 0), (MLA_NOPE, 128 - MLA_NOPE - MLA_ROPE)))
    wrow = jnp.concatenate([
        w_in[:, _C_CONV:_C_CONV + 512], w_in[:, _C_POOL:_C_POOL + 256], kv(0), kv(1),
        jnp.pad(kv(2), ((0, 0), (0, 64))), jnp.pad(kv(4), ((0, 0), (0, 64))),
        w_in[:, _C_CQ:_C_CQ + Q_LORA], w_in[:, _C_CKV:_C_CKV + KV_LORA], place(kr), place(_rot_cols(kr))],
        axis=1).astype(BF16)
    wt = jnp.concatenate([
        w_in[:, _C_Q:_C_Q + 256].T * (NSA_HD ** -0.5), kv(3).T, kv(5).T,
        jnp.pad(w_in[:, _C_G:_C_G + 12].T, ((0, 4), (0, 0)))], axis=0).astype(BF16)

    dw = jnp.pad(conv_dw, ((0, 1), (0, 0)))

    w1r = cmp_w1.reshape(2, CMP_LEN, NSA_HD, 128)
    zk = jnp.zeros_like(w1r[0])
    w1kv = jnp.concatenate([jnp.concatenate([w1r[0], zk], axis=-1),
                            jnp.concatenate([zk, w1r[1]], axis=-1)], axis=1)
    w1a = w1kv[:16].reshape(16 * 128, 256).astype(BF16)
    w1b = w1kv[16:].reshape(16 * 128, 256).astype(BF16)
    pos_kv = jnp.concatenate([cmp_pos[0], cmp_pos[1]], axis=-1)
    posa = pos_kv[:16].reshape(1, 16 * 128)
    posb = pos_kv[16:].reshape(1, 16 * 128)
    z2 = jnp.zeros_like(cmp_w2[0])
    w2bd = jnp.concatenate([jnp.concatenate([cmp_w2[0], z2], axis=1),
                            jnp.concatenate([z2, cmp_w2[1]], axis=1)], axis=0).astype(BF16)

    wq = w_uq.reshape(Q_LORA, MLA_HEADS, MLA_NOPE + MLA_ROPE)
    wqt = jnp.pad(wq, ((0, 0), (0, 0), (0, 128 - MLA_NOPE - MLA_ROPE))).reshape(Q_LORA, 512).T.astype(BF16)
    wqpt = _rot_cols(wq[:, :, MLA_NOPE:]).reshape(Q_LORA, MLA_HEADS * MLA_ROPE).T.astype(BF16)
    wkv = w_ukv.reshape(KV_LORA, MLA_HEADS, MLA_NOPE + MLA_V)
    wk = jnp.pad(wkv[:, :, :MLA_NOPE], ((0, 0), (0, 0), (0, 128 - MLA_NOPE))).reshape(KV_LORA, 512).astype(BF16)
    wvt = wkv[:, :, MLA_NOPE:].reshape(KV_LORA, MLA_HEADS * MLA_V).T.astype(BF16)

    poolw = jax.scipy.linalg.block_diag(*[pool_w[g] for g in range(len(POOL_WINDOWS))]).astype(BF16)
    return dict(wrow=wrow, wt=wt, dw=dw, w1a=w1a, w1b=w1b, posa=posa, posb=posb, w2bd=w2bd,
                wqt=wqt, wqpt=wqpt, wk=wk, wvt=wvt, poolw=poolw)


def kernel(x, w_in, w_out, ln_mix_g, ln_mlp_g, conv_dw, conv_dw_b, conv_ln_g, conv_ln_b, conv_pw,
           nsa_cmp_pos, nsa_cmp_w1, nsa_cmp_w2, mla_q_norm_g, mla_w_uq, mla_kv_norm_g, mla_w_ukv,
           pool_w, pool_scale, mlp_w1, mlp_w2, rel_bias_table, final_norm_g):
    batch, seq, _ = x.shape
    depth = w_in.shape[0]
    bd, bd1, bc = _bias_tiles(rel_bias_table)
    ovl_t = _overlap_t(seq)
    cosp, sinp, cost, sint = _rope_tables(seq)
    r1 = lambda v: v.reshape(1, -1)
    h = x.reshape(batch * seq, D_MODEL)
    for l in range(depth):
        w = _layer_weights(w_in[l], conv_dw[l], nsa_cmp_pos[l], nsa_cmp_w1[l], nsa_cmp_w2[l],
                           mla_w_uq[l], mla_w_ukv[l], pool_w[l])
        mla_args = (r1(mla_q_norm_g[l]), r1(mla_kv_norm_g[l]), w["wqt"], w["wqpt"], w["wk"], w["wvt"],
                    cosp, sinp, cost, sint)
        loc_args = (w["dw"], r1(conv_dw_b[l]), r1(conv_ln_g[l]), r1(conv_ln_b[l]), conv_pw[l].astype(BF16),
                    w["poolw"], r1(pool_scale[l]))
        yc, yp, kcvc, ks, kw, km, qtm, vtm, qt, vt, gt = _proj(
            h, r1(ln_mix_g[l]), w["wrow"], w["wt"], mla_args, loc_args, batch=batch, seq=seq)
        kvc, kvct = _compress(kcvc, w["posa"], w["posb"], w["w1a"], w["w1b"], w["w2bd"], batch=batch, seq=seq)
        yn = _nsa(kvc, kvct, ks, kw, vt, qt, gt, ovl_t, bd, bd1, bc, batch=batch, seq=seq)
        ym = _mla_attn(km.reshape(batch, seq, 512), qtm, vtm, batch=batch, seq=seq)
        h = _out_mlp(h, yc, yn, ym, yp, w_out[l].astype(BF16), r1(ln_mlp_g[l]), mlp_w1[l].astype(BF16),
                     mlp_w2[l].astype(BF16), r1(final_norm_g), final=(l == depth - 1))
    return h.reshape(batch, seq, D_MODEL)
```

```python
import functools
import math

import numpy as np
import jax
import jax.numpy as jnp
from jax import lax
from jax.experimental import pallas as pl
from jax.experimental.pallas import tpu as pltpu

F32 = jnp.float32
BF16 = jnp.bfloat16

D_MODEL = 1024
GROUP_W = 256
CONV_K = 31
NSA_HEADS = 4
NSA_HD = 64
CMP_LEN = 32
CMP_STRIDE = 16
SLC_LEN = 64
N_SELECT = 16
WINDOW = 512
MLA_HEADS = 4
MLA_NOPE = 64
MLA_ROPE = 32
MLA_V = 64
Q_LORA = 256
KV_LORA = 128
ROPE_THETA = 10000.0
POOL_WINDOWS = (2, 4, 8, 16)
D_FF = 4 * D_MODEL
N_BUCKETS = 32
MAX_DIST = 128
EPS = 1e-6
NEG_INF = -1e30
FORCE_SCORE = 1e9
LOG2E = math.log2(math.e)

LANES = 128
SUBLANES = 8
QB = 128
HALO = 32
FAR_TILE = 512
FAR_BLOCKS = FAR_TILE // SLC_LEN
MASK_ROWS = 16
VT_ROWS = 80
MLA_TILE = 512
VMEM_LIMIT = 56 * 1024 * 1024

_C_CONV, _C_Q, _C_KV, _C_G, _C_CQ, _C_CKV, _C_KR, _C_POOL = 0, 512, 768, 1152, 1164, 1420, 1548, 1580


def _cparams(sem):
    return pltpu.CompilerParams(dimension_semantics=sem, vmem_limit_bytes=VMEM_LIMIT)


def _rms(x, g):
    ms = jnp.mean(x * x, axis=-1, keepdims=True)
    return x * lax.rsqrt(ms + EPS) * g


def _dot(a, b):
    return jnp.dot(a, b, preferred_element_type=F32)


def _dot_nt(a, b):
    return lax.dot_general(a, b, (((1,), (1,)), ((), ())), preferred_element_type=F32)


def _const_spec(shape):
    nd = len(shape)
    return pl.BlockSpec(shape, lambda *_: (0,) * nd)


def _proj_kernel(x_ref, g_ref, wrow_ref, wt_ref, *rest, tile, sub, steps_per_seq):
    mla = rest[:10]
    loc = rest[10:17]
    yc_ref, yp_ref, kcvc_ref, ks_ref, kw_ref, kmla_ref, qtmla_ref, vtmla_ref, qt_ref, vt_ref, gt_ref = rest[17:28]
    scratch = rest[28:]
    step = pl.program_id(0)
    seq_step = step % steps_per_seq

    @pl.when(step == 0)
    def _():
        scratch[0][sub:sub + HALO, :] = jnp.zeros((HALO, GROUP_W), F32)
        scratch[1][sub:sub + HALO, :] = jnp.zeros((HALO, GROUP_W), F32)

    ones = jnp.ones((VT_ROWS - NSA_HD, LANES), BF16)
    lane = lax.broadcasted_iota(jnp.int32, (sub, LANES), 1)
    mixers = []
    for r in range(tile // sub):
        rows = slice(r * sub, (r + 1) * sub)
        xn = _rms(x_ref[rows, :], g_ref[...]).astype(BF16)
        uc = _dot(xn, wrow_ref[:, 0:512])
        up = _dot(xn, wrow_ref[:, 512:768])
        kk = _dot(xn, wrow_ref[:, 768:1024])
        kcvc_ref[rows, :] = kk[:, 0:LANES]
        tok = pl.program_id(0) * tile + r * sub + lax.broadcasted_iota(jnp.int32, (sub, LANES), 0)
        onehot = (lane - NSA_HD == (tok // SLC_LEN) % FAR_BLOCKS).astype(F32)
        ks_ref[rows, :] = (kk[:, LANES:2 * LANES] + onehot).astype(BF16)
        km = _dot(xn, wrow_ref[:, 1024:1792])
        kw_ref[rows, :] = km[:, 0:LANES].astype(BF16)
        qt_ref[:, rows] = _dot_nt(wt_ref[0:256, :], xn).astype(BF16)
        vs = _dot_nt(wt_ref[256:384, :], xn).astype(BF16)
        for j in range(sub // LANES):
            cols = slice(j * LANES, (j + 1) * LANES)
            jt = r * (sub // LANES) + j
            vt_ref[jt, 0:NSA_HD, :] = vs[0:NSA_HD, cols]
            vt_ref[jt, NSA_HD:VT_ROWS, :] = ones
            vt_ref[jt, VT_ROWS:VT_ROWS + NSA_HD, :] = vs[NSA_HD:2 * NSA_HD, cols]
            vt_ref[jt, VT_ROWS + NSA_HD:2 * VT_ROWS, :] = ones
        gt_ref[:, rows] = _dot_nt(wt_ref[384:400, :], xn)
        _mla_prep_tile(km[:, LANES:], mla, rows, kmla_ref, qtmla_ref, vtmla_ref.at[r])
        mixers.append((uc, up, r, rows))
    for uc, up, r, rows in mixers:
        first = (seq_step == 0) if r == 0 else False
        _local_tile(uc, up, first, seq_step * tile + r * sub, loc, rows, yc_ref, yp_ref, scratch)


def _proj(h, g, wrow, wt, mla_args, loc_args, *, batch, seq, tile=1024):
    qg, kg, wqt, wqpt, wk, wvt, cosp, sinp, cost, sint = mla_args
    dw, dwb, lng, lnb, pw, poolw, pscale = loc_args
    sub = MLA_TILE
    n = h.shape[0]
    spb = seq // tile
    mt = tile // MLA_TILE
    outs = (
        jax.ShapeDtypeStruct((n, GROUP_W), BF16),
        jax.ShapeDtypeStruct((n, GROUP_W), BF16),
        jax.ShapeDtypeStruct((n, 128), F32),
        jax.ShapeDtypeStruct((n, 128), BF16),
        jax.ShapeDtypeStruct((n, 128), BF16),
        jax.ShapeDtypeStruct((n, 512), BF16),
        jax.ShapeDtypeStruct((batch, 512, seq), BF16),
        jax.ShapeDtypeStruct((batch, seq // MLA_TILE, MLA_HEADS * VT_ROWS, MLA_TILE), BF16),
        jax.ShapeDtypeStruct((256, n), BF16),
        jax.ShapeDtypeStruct((n // LANES, 2 * VT_ROWS, LANES), BF16),
        jax.ShapeDtypeStruct((16, n), F32),
    )
    row = lambda w: pl.BlockSpec((tile, w), lambda i: (i, 0))
    return pl.pallas_call(
        functools.partial(_proj_kernel, tile=tile, sub=sub, steps_per_seq=spb),
        grid=(n // tile,),
        in_specs=[row(D_MODEL), _const_spec((1, D_MODEL)), _const_spec(wrow.shape), _const_spec(wt.shape),
                  _const_spec((1, Q_LORA)), _const_spec((1, KV_LORA)),
                  _const_spec(wqt.shape), _const_spec(wqpt.shape), _const_spec(wk.shape), _const_spec(wvt.shape),
                  pl.BlockSpec((tile, 128), lambda i: (i % spb, 0)), pl.BlockSpec((tile, 128), lambda i: (i % spb, 0)),
                  pl.BlockSpec((MLA_ROPE, tile), lambda i: (0, i % spb)),
                  pl.BlockSpec((MLA_ROPE, tile), lambda i: (0, i % spb)),
                  _const_spec(dw.shape), _const_spec((1, GROUP_W)), _const_spec((1, GROUP_W)),
                  _const_spec((1, GROUP_W)), _const_spec((GROUP_W, GROUP_W)),
                  _const_spec((GROUP_W, GROUP_W)), _const_spec((1, GROUP_W))],
        out_specs=(row(GROUP_W), row(GROUP_W), row(128), row(128), row(128), row(512),
                   pl.BlockSpec((None, 512, tile), lambda i: (i // spb, 0, i % spb)),
                   pl.BlockSpec((None, mt, MLA_HEADS * VT_ROWS, MLA_TILE), lambda i: (i // spb, i % spb, 0, 0)),
                   pl.BlockSpec((256, tile), lambda i: (0, i)),
                   pl.BlockSpec((tile // LANES, 2 * VT_ROWS, LANES), lambda i: (i, 0, 0)),
                   pl.BlockSpec((16, tile), lambda i: (0, i))),
        out_shape=outs,
        scratch_shapes=[pltpu.VMEM((HALO + sub, GROUP_W), F32)] * 4
        + [pltpu.VMEM((SUBLANES - 1, HALO + sub, GROUP_W), F32)],
        compiler_params=_cparams(("arbitrary",)),
        name="in_proj",
    )(h, g, wrow, wt, *mla_args, *loc_args)


def _local_tile(uc, up, first, pos0, loc, out_rows, yc_ref, yp_ref, scratch, *, rows=128):
    dw_ref, dwb_ref, lng_ref, lnb_ref, pw_ref, poolw_ref, pscale_ref = loc
    hbuf, ubuf, sa, sb, hshift = scratch
    tile = uc.shape[0]
    hbuf[0:HALO, :] = jnp.where(first, 0.0, hbuf[tile:tile + HALO, :])
    ubuf[0:HALO, :] = jnp.where(first, 0.0, ubuf[tile:tile + HALO, :])
    hbuf[HALO:HALO + tile, :] = uc[:, 0:GROUP_W] * jax.nn.sigmoid(uc[:, GROUP_W:2 * GROUP_W])
    span = HALO + tile - SUBLANES
    for j in range(1, SUBLANES):
        hshift[j - 1, 0:span, :] = hbuf[pl.ds(j, span), :]
    dwb = dwb_ref[...]
    for c in range(tile // rows):
        acc = jnp.zeros((rows, GROUP_W), F32) + dwb
        for k in range(CONV_K):
            off = c * rows + HALO - (CONV_K - 1) + k
            phase, base = off % SUBLANES, off - off % SUBLANES
            tap = hbuf[pl.ds(base, rows), :] if phase == 0 else hshift[phase - 1, pl.ds(base, rows), :]
            acc = acc + dw_ref[k:k + 1, :] * tap
        mu = jnp.mean(acc, axis=-1, keepdims=True)
        cen = acc - mu
        var = jnp.mean(cen * cen, axis=-1, keepdims=True)
        y = cen * lax.rsqrt(var + EPS) * lng_ref[...] + lnb_ref[...]
        y = y * jax.nn.sigmoid(y)
        r0 = out_rows.start + c * rows
        yc_ref[r0:r0 + rows, :] = _dot(y.astype(BF16), pw_ref[...]).astype(BF16)

    ext = HALO + tile
    ubuf[HALO:ext, :] = up
    sa[pl.ds(8, ext - 8), :] = ubuf[pl.ds(8, ext - 8), :] + ubuf[pl.ds(7, ext - 8), :]
    sb[pl.ds(16, ext - 16), :] = sa[pl.ds(16, ext - 16), :] + sa[pl.ds(14, ext - 16), :]
    s2 = sa[pl.ds(HALO, tile), :]
    s4 = sb[pl.ds(HALO, tile), :]
    sa[pl.ds(24, ext - 24), :] = sb[pl.ds(24, ext - 24), :] + sb[pl.ds(20, ext - 24), :]
    s8 = sa[pl.ds(HALO, tile), :]
    s16 = s8 + sa[pl.ds(HALO - 8, tile), :]
    lane = lax.broadcasted_iota(jnp.int32, (tile, GROUP_W), 1)
    grp = lane // (GROUP_W // len(POOL_WINDOWS))
    wsum = jnp.where(grp == 0, s2, jnp.where(grp == 1, s4, jnp.where(grp == 2, s8, s16)))
    width = jnp.where(grp == 0, 2, jnp.where(grp == 1, 4, jnp.where(grp == 2, 8, 16)))
    pos = pos0 + lax.broadcasted_iota(jnp.int32, (tile, GROUP_W), 0)
    cnt = jnp.minimum(width, pos + 1).astype(F32)
    d = wsum / cnt - up
    yp_ref[out_rows, :] = (_dot(d.astype(BF16), poolw_ref[...]) * pscale_ref[...]).astype(BF16)


def _compress_kernel(kr_ref, posa_ref, posb_ref, w1a_ref, w1b_ref, w2_ref, kv_ref, kvt_ref, bbuf, *, nc):
    kr = jnp.concatenate([kr_ref[pl.ds(l, nc, stride=CMP_STRIDE), :] for l in range(CMP_STRIDE)], axis=1)
    a = _dot((kr + posa_ref[...]).astype(BF16), w1a_ref[...])
    b = _dot((kr + posb_ref[...]).astype(BF16), w1b_ref[...])
    bbuf[0:nc, :] = b
    bbuf[nc:nc + 8, :] = jnp.zeros((8, 256), F32)
    hdn = a + bbuf[pl.ds(1, nc), :]
    hdn = hdn * jax.nn.sigmoid(hdn)
    kv = _dot(hdn.astype(BF16), w2_ref[...])
    kv_ref[...] = kv.astype(BF16)
    kvt_ref[...] = kv.T.astype(BF16)


def _compress(kcvc, posa, posb, w1a, w1b, w2bd, *, batch, seq):
    nc = seq // CMP_STRIDE
    kr = kcvc.reshape(batch, seq, 128)
    return pl.pallas_call(
        functools.partial(_compress_kernel, nc=nc),
        grid=(batch,),
        in_specs=[pl.BlockSpec((None, seq, 128), lambda b: (b, 0, 0)),
                  _const_spec(posa.shape), _const_spec(posb.shape), _const_spec(w1a.shape),
                  _const_spec(w1b.shape), _const_spec(w2bd.shape)],
        out_specs=(pl.BlockSpec((None, nc, 128), lambda b: (b, 0, 0)),
                   pl.BlockSpec((None, 128, nc), lambda b: (b, 0, 0))),
        out_shape=(jax.ShapeDtypeStruct((batch, nc, 128), BF16),
                   jax.ShapeDtypeStruct((batch, 128, nc), BF16)),
        scratch_shapes=[pltpu.VMEM((nc + 8, 256), F32)],
        compiler_params=_cparams(("parallel",)),
        name="nsa_compress",
    )(kr, posa, posb, w1a, w1b, w2bd)


def _nsa_kernel(kvc_ref, kvct_ref, ks_ref, kw_ref, vt_ref, qt_ref, gt_ref, ovl_ref, bd_ref, bd1_ref, bc_ref,
                y_ref, *scratch, nc, ns, n_top, per_step):
    for j in range(per_step):
        cols = pl.ds(j * QB, QB)
        _nsa_block(pl.program_id(1) * per_step + j, kvc_ref, kvct_ref, ks_ref, kw_ref, vt_ref, qt_ref.at[:, cols],
                   gt_ref.at[:, cols], ovl_ref, bd_ref, bd1_ref, bc_ref, y_ref.at[cols, :], *scratch,
                   nc=nc, ns=ns, n_top=n_top)


def _nsa_block(qb, kvc_ref, kvct_ref, ks_ref, kw_ref, vt_ref, qt_ref, gt_ref, ovl_ref, bd_ref, bd1_ref, bc_ref,
               y_ref, s_ref, key_ref, tie_ref, sel_ref, mb_ref, m_ref, acc_ref, mn_ref, accn_ref, oc_ref, ow_ref, sa_ref, sb_ref, cma_ref, cmb_ref,
               *, nc, ns, n_top):
    hq = NSA_HEADS * QB
    qt = jnp.concatenate([qt_ref[h * NSA_HD:(h + 1) * NSA_HD, :] for h in range(NSA_HEADS)], axis=1)
    qts = jnp.concatenate([qt, jnp.zeros_like(qt)], axis=0)
    lane_q = lax.broadcasted_iota(jnp.int32, (1, hq), 1) % QB

    tq = qb * QB + lane_q
    pad = 16
    sub8 = lax.broadcasted_iota(jnp.int32, (8, QB), 0)

    def select(nrows, nblk):
        s_ref[0:pad, :] = jnp.zeros((pad, hq), F32)
        s_ref[pad + nrows:pad + nrows + pad, :] = jnp.zeros((pad, hq), F32)
        s_ref[pad:pad + nrows, :] = _dot(kvc_ref[0:nrows, :], qts)
        w0 = pl.multiple_of(qb * 8, 8)
        s_ref[pl.ds(w0, 32), :] = s_ref[pl.ds(w0, 32), :] + bc_ref[...]
        sc = s_ref[pad:pad + nrows, :]
        cidx = lax.broadcasted_iota(jnp.int32, (nrows, hq), 0)
        last_c = jnp.minimum(lax.shift_right_arithmetic(tq - (CMP_LEN - 1), 4), nc - 2)
        sc = jnp.where(cidx <= last_c, sc, NEG_INF)
        mc = jnp.max(sc, axis=0, keepdims=True)
        pc = jnp.exp(sc - mc)
        lc = jnp.sum(pc, axis=0, keepdims=True)
        pc = pc * jnp.where(last_c >= 0, 1.0 / lc, 0.0)
        oc_ref[...] = _dot(kvct_ref[NSA_HD:2 * NSA_HD, 0:nrows], pc.astype(BF16))

        psum = pc[:, 0:QB]
        for h in range(1, NSA_HEADS):
            psum = psum + pc[:, h * QB:(h + 1) * QB]
        p_hi = psum.astype(BF16)
        p_lo = (psum - p_hi.astype(F32)).astype(BF16)
        ovl = ovl_ref[0:nblk, 0:nrows]
        imp = _dot(ovl, p_hi) + _dot(ovl, p_lo)
        jidx = lax.broadcasted_iota(jnp.int32, (nblk, QB), 0)
        iq = lax.broadcasted_iota(jnp.int32, (nblk, QB), 1)
        jc = 2 * qb + (iq >= SLC_LEN).astype(jnp.int32)
        forced = (jidx == 0) | (jidx == jc) | (jidx == jc - 1)
        imp = jnp.where(forced, FORCE_SCORE, imp)
        imp = jnp.where(jidx <= jc, imp, NEG_INF)
        key = pltpu.bitcast(imp, jnp.int32)
        key_ref[0:nblk, :] = key
        tie_ref[0:nblk, :] = jnp.zeros((nblk, QB), jnp.int32)
        n_vreg = nblk // 8

        def rank_body(g, cnts):
            base = pl.multiple_of(g * 8, 8)
            rows = key_ref[pl.ds(base, 8), :]
            out = []
            for tv in range(n_vreg):
                target = key[tv * 8:(tv + 1) * 8, :] - jnp.where(tv > g, 1, 0)
                c = cnts[tv]
                for r in range(8):
                    c = c + (rows[r:r + 1, :] > target).astype(jnp.int32)
                out.append(c)
            ties = jnp.zeros((8, QB), jnp.int32)
            for r in range(8):
                ties = ties + ((rows[r:r + 1, :] == rows) & (sub8 > r)).astype(jnp.int32)
            tie_ref[pl.ds(base, 8), :] = ties
            return tuple(out)

        n_groups = jnp.minimum((2 * qb + 9) // 8, n_vreg)
        cnts = lax.fori_loop(0, n_groups, rank_body, tuple(jnp.zeros((8, QB), jnp.int32) for _ in range(n_vreg)))
        cnt = jnp.concatenate(cnts, axis=0) + tie_ref[0:nblk, :]
        chosen = (cnt < n_top) & (imp > 0.5 * NEG_INF)
        sel_ref[0:nblk, :] = jnp.where(chosen, 1.0, 0.0)
        mb_ref[0:nblk, :] = jnp.where(chosen & (jidx < 2 * (qb - 1)), 0.0, NEG_INF)
        if nblk < ns:
            sel_ref[nblk:ns, :] = jnp.zeros((ns - nblk, QB), F32)
            mb_ref[nblk:ns, :] = jnp.full((ns - nblk, QB), NEG_INF, F32)

    nqb = nc // (QB // CMP_STRIDE)
    parts = next(p for p in (4, 2, 1) if nc % (p * LANES) == 0 and ns % (p * SUBLANES) == 0)
    for k in range(parts):
        in_part = (qb >= k * nqb // parts) & (qb < (k + 1) * nqb // parts)
        pl.when(in_part)(functools.partial(select, nc * (k + 1) // parts, ns * (k + 1) // parts))

    m_ref[...] = jnp.full((1, hq), NEG_INF, F32)
    acc_ref[...] = jnp.zeros((VT_ROWS, hq), F32)
    per_far = FAR_TILE // QB
    n_far = (jnp.maximum(qb - 1, 0) + per_far - 1) // per_far
    zpad = jnp.zeros((QB - NSA_HD - MASK_ROWS, hq), BF16)

    def q_aug(t):
        rows = mb_ref[pl.ds(pl.multiple_of(t * FAR_BLOCKS, FAR_BLOCKS), FAR_BLOCKS), :]
        rows = jnp.concatenate([rows] * NSA_HEADS, axis=1)
        rows = jnp.concatenate([rows, jnp.zeros((MASK_ROWS - FAR_BLOCKS, hq), F32)], axis=0).astype(BF16)
        return jnp.concatenate([qt, rows, zpad], axis=0)

    def produce(t, s_x, cm_x):
        s = _dot(ks_ref[pl.ds(pl.multiple_of(t * FAR_TILE, FAR_TILE), FAR_TILE), :], q_aug(t))
        s_x[...] = s
        cm_x[...] = jnp.max(s, axis=0, keepdims=True)

    def soften(s_x, cm_x):
        m_old = m_ref[...]
        m_new = jnp.maximum(m_old, cm_x[...])
        m_ref[...] = m_new
        return jnp.exp(s_x[...] - m_new).astype(BF16), jnp.exp(m_old - m_new)

    def accumulate(t, p, alpha):
        sub = FAR_TILE // QB
        vt = jnp.concatenate([vt_ref[sub * t + j, 0:VT_ROWS, :] for j in range(sub)], axis=1)
        acc_ref[...] = alpha * acc_ref[...] + _dot(vt, p)

    def consume(t, s_x, cm_x):
        accumulate(t, *soften(s_x, cm_x))

    produce(0, sa_ref, cma_ref)

    kl = lax.broadcasted_iota(jnp.int32, (QB, hq), 0)
    causal = kl <= lane_q
    anti = kl > lane_q
    n_w = WINDOW // QB
    kts = [jnp.maximum(qb - back, 0) for back in range(n_w + 1)]
    live = [qb >= back for back in range(n_w + 1)]

    def krows(ref, kt):
        return ref[pl.ds(pl.multiple_of(kt * QB, QB), QB), :]

    def sel_mask(kt):
        m0 = jnp.broadcast_to(sel_ref[pl.ds(2 * kt, 1), :], (SLC_LEN, QB))
        m1 = jnp.broadcast_to(sel_ref[pl.ds(2 * kt + 1, 1), :], (SLC_LEN, QB))
        mt = jnp.concatenate([m0, m1], axis=0)
        return jnp.concatenate([mt] * NSA_HEADS, axis=1) > 0.5

    def softmax_pv(tiles, row0):
        m = jnp.max(tiles[0], axis=0, keepdims=True)
        for s in tiles[1:]:
            m = jnp.maximum(m, jnp.max(s, axis=0, keepdims=True))
        acc = jnp.zeros((VT_ROWS, hq), F32)
        for back, s in enumerate(tiles):
            acc = acc + _dot(vt_ref[kts[back], row0:row0 + VT_ROWS, :], jnp.exp(s - m).astype(BF16))
        return m, acc

    s_win = [_dot(krows(kw_ref, kts[back]), qts) for back in range(n_w + 1)]
    s_sel = [_dot(krows(ks_ref, kts[back]), qts) for back in range(2)]
    s_win[0] = jnp.where(causal, s_win[0] + bd_ref[...], NEG_INF)
    s_win[1] = jnp.where(live[1], s_win[1] + bd1_ref[...], NEG_INF)
    for back in range(2, n_w):
        s_win[back] = jnp.where(live[back], s_win[back], NEG_INF)
    s_win[n_w] = jnp.where(anti & live[n_w], s_win[n_w], NEG_INF)
    s_sel[0] = jnp.where(sel_mask(qb) & causal, s_sel[0] + bd_ref[...], NEG_INF)
    s_sel[1] = jnp.where(sel_mask(kts[1]) & live[1], s_sel[1] + bd1_ref[...], NEG_INF)
    _, acc_w = softmax_pv(s_win, VT_ROWS)
    ow_ref[...] = acc_w[0:NSA_HD, :] * (1.0 / acc_w[NSA_HD:NSA_HD + 1, :])
    mn_ref[...], accn_ref[...] = softmax_pv(s_sel, 0)

    def pair_body(j, carry):
        produce(2 * j + 1, sb_ref, cmb_ref)
        consume(2 * j, sa_ref, cma_ref)
        produce(jnp.minimum(2 * j + 2, n_far - 1), sa_ref, cma_ref)
        consume(2 * j + 1, sb_ref, cmb_ref)
        return carry

    lax.fori_loop(0, n_far // 2, pair_body, 0)

    @pl.when(n_far % 2 == 1)
    def _():
        consume(n_far - 1, sa_ref, cma_ref)

    m_far = m_ref[...]
    m_near = mn_ref[...]
    m_all = jnp.maximum(m_far, m_near)
    acc = jnp.exp(m_far - m_all) * acc_ref[...] + jnp.exp(m_near - m_all) * accn_ref[...]
    osel = acc[0:NSA_HD, :] * (1.0 / acc[NSA_HD:NSA_HD + 1, :])

    gates = jax.nn.sigmoid(gt_ref[...])
    oc = oc_ref[...]
    ow = ow_ref[...]
    parts = []
    for h in range(NSA_HEADS):
        sl = slice(h * QB, (h + 1) * QB)
        parts.append(gates[3 * h:3 * h + 1, :] * oc[:, sl] + gates[3 * h + 1:3 * h + 2, :] * osel[:, sl]
                     + gates[3 * h + 2:3 * h + 3, :] * ow[:, sl])
    y_ref[...] = jnp.concatenate(parts, axis=0).T.astype(BF16)


def _nsa(kvc, kvct, ks, kw, vt, qt, gt, ovl_t, bd, bd1, bc, *, batch, seq):
    nc = seq // CMP_STRIDE
    ns = seq // SLC_LEN
    nqb = seq // QB
    hq = NSA_HEADS * QB
    ks3 = ks.reshape(batch, seq, 128)
    kw3 = kw.reshape(batch, seq, 128)
    vt4 = vt.reshape(batch, nqb, 2 * VT_ROWS, LANES)
    per_step = 2 if nqb % 2 == 0 else 1
    steps = nqb // per_step
    return pl.pallas_call(
        functools.partial(_nsa_kernel, nc=nc, ns=ns, n_top=min(N_SELECT, ns), per_step=per_step),
        grid=(batch, steps),
        in_specs=[pl.BlockSpec((None, nc, 128), lambda b, q: (b, 0, 0)),
                  pl.BlockSpec((None, 128, nc), lambda b, q: (b, 0, 0)),
                  pl.BlockSpec((None, seq, 128), lambda b, q: (b, 0, 0)),
                  pl.BlockSpec((None, seq, 128), lambda b, q: (b, 0, 0)),
                  pl.BlockSpec((None, nqb, 2 * VT_ROWS, LANES), lambda b, q: (b, 0, 0, 0)),
                  pl.BlockSpec((256, per_step * QB), lambda b, q: (0, b * steps + q)),
                  pl.BlockSpec((16, per_step * QB), lambda b, q: (0, b * steps + q)),
                  _const_spec(ovl_t.shape), _const_spec(bd.shape), _const_spec(bd1.shape),
                  _const_spec(bc.shape)],
        out_specs=pl.BlockSpec((per_step * QB, 256), lambda b, q: (b * steps + q, 0)),
        out_shape=jax.ShapeDtypeStruct((batch * seq, 256), BF16),
        scratch_shapes=[pltpu.VMEM((nc + 32, hq), F32),
                        pltpu.VMEM((ns, QB), jnp.int32),
                        pltpu.VMEM((ns, QB), jnp.int32),
                        pltpu.VMEM((ns, QB), F32),
                        pltpu.VMEM((ns, QB), F32),
                        pltpu.VMEM((1, hq), F32),
                        pltpu.VMEM((VT_ROWS, hq), F32),
                        pltpu.VMEM((1, hq), F32),
                        pltpu.VMEM((VT_ROWS, hq), F32),
                        pltpu.VMEM((NSA_HD, hq), F32),
                        pltpu.VMEM((NSA_HD, hq), F32),
                        *[pltpu.VMEM((FAR_TILE, hq), F32)] * 2,
                        *[pltpu.VMEM((1, hq), F32)] * 2],
        compiler_params=_cparams(("parallel", "parallel")),
        name="nsa_attention",
    )(kvc, kvct, ks3, kw3, vt4, qt, gt, ovl_t, bd, bd1, bc)


def _mla_prep_tile(u, mla, rows, k_ref, qt_ref, vt_ref):
    qg_ref, kg_ref, wqt_ref, wqpt_ref, wk_ref, wvt_ref, cosp_ref, sinp_ref, cost_ref, sint_ref = mla
    scale = (MLA_NOPE + MLA_ROPE) ** -0.5 * LOG2E
    cqn = _rms(u[:, 0:Q_LORA], qg_ref[...]).astype(BF16)
    ckvn = _rms(u[:, Q_LORA:Q_LORA + KV_LORA], kg_ref[...]).astype(BF16)
    qta = _dot_nt(wqt_ref[...], cqn)
    qtp = _dot_nt(wqpt_ref[...], cqn)
    cos_t = cost_ref[:, rows]
    sin_t = sint_ref[:, rows]
    n_tok = u.shape[0]
    for h in range(MLA_HEADS):
        r0 = h * 128
        qt_ref[r0:r0 + MLA_NOPE, rows] = (qta[r0:r0 + MLA_NOPE, :] * scale).astype(BF16)
        rope = qta[r0 + MLA_NOPE:r0 + MLA_NOPE + MLA_ROPE, :] * cos_t + qtp[h * MLA_ROPE:(h + 1) * MLA_ROPE, :] * sin_t
        qt_ref[r0 + MLA_NOPE:r0 + MLA_NOPE + MLA_ROPE, rows] = (rope * scale).astype(BF16)
        qt_ref[r0 + MLA_NOPE + MLA_ROPE:r0 + 128, rows] = jnp.zeros((128 - MLA_NOPE - MLA_ROPE, n_tok), BF16)
    krope = u[:, 384:512] * cosp_ref[rows, :] + u[:, 512:640] * sinp_ref[rows, :]
    knope = _dot(ckvn, wk_ref[...])
    for h in range(MLA_HEADS):
        k_ref[rows, h * 128:(h + 1) * 128] = (knope[:, h * 128:(h + 1) * 128] + krope).astype(BF16)
    vt = _dot_nt(wvt_ref[...], ckvn).astype(BF16)
    for h in range(MLA_HEADS):
        vt_ref[h * VT_ROWS:h * VT_ROWS + MLA_V, :] = vt[h * MLA_V:(h + 1) * MLA_V, :]
        vt_ref[h * VT_ROWS + MLA_V:(h + 1) * VT_ROWS, :] = jnp.ones((VT_ROWS - MLA_V, n_tok), BF16)


def _mla_attn_kernel(k_ref, vt_ref, qt_ref, o_ref, m_ref, acc_ref, sa_ref, sb_ref, cma_ref, cmb_ref, *, tile):
    qi = pl.program_id(1)
    heads = range(MLA_HEADS)
    m_ref[...] = jnp.full(m_ref.shape, NEG_INF, F32)
    acc_ref[...] = jnp.zeros(acc_ref.shape, F32)

    def scores(t, h):
        hs = slice(h * 128, (h + 1) * 128)
        return _dot(k_ref[pl.ds(pl.multiple_of(t * tile, tile), tile), hs], qt_ref[hs, :])

    def produce(t, s_x, cm_x):
        for h in heads:
            s = scores(t, h)
            s_x[h] = s
            cm_x[h:h + 1, :] = jnp.max(s, axis=0, keepdims=True)

    def consume(t, tiles, maxima):
        probs, alphas = [], []
        for h in heads:
            m_old = m_ref[h:h + 1, :]
            m_new = jnp.maximum(m_old, maxima[h])
            probs.append(jnp.exp2(tiles[h] - m_new).astype(BF16))
            alphas.append(jnp.exp2(m_old - m_new))
            m_ref[h:h + 1, :] = m_new
        for h in heads:
            acc_ref[h] = alphas[h] * acc_ref[h] + _dot(vt_ref[t, h * VT_ROWS:(h + 1) * VT_ROWS, :], probs[h])

    def consume_from(t, s_x, cm_x):
        consume(t, [s_x[h] for h in heads], [cm_x[h:h + 1, :] for h in heads])

    produce(0, sa_ref, cma_ref)
    kl = lax.broadcasted_iota(jnp.int32, (tile, tile), 0)
    iq = lax.broadcasted_iota(jnp.int32, (tile, tile), 1)
    diag = [jnp.where(kl <= iq, scores(qi, h), NEG_INF) for h in heads]
    consume(qi, diag, [jnp.max(s, axis=0, keepdims=True) for s in diag])

    def pair_body(j, carry):
        produce(2 * j + 1, sb_ref, cmb_ref)
        consume_from(2 * j, sa_ref, cma_ref)
        produce(jnp.minimum(2 * j + 2, qi - 1), sa_ref, cma_ref)
        consume_from(2 * j + 1, sb_ref, cmb_ref)
        return carry

    lax.fori_loop(0, qi // 2, pair_body, 0)

    @pl.when(qi % 2 == 1)
    def _():
        consume_from(qi - 1, sa_ref, cma_ref)

    outs = [acc_ref[h, 0:MLA_V, :] * (1.0 / acc_ref[h, MLA_V:MLA_V + 1, :]) for h in heads]
    o_ref[...] = jnp.concatenate(outs, axis=0).T.astype(BF16)


def _mla_attn(k, qt, vt, *, batch, seq, tile=MLA_TILE):
    nt = seq // tile
    y = pl.pallas_call(
        functools.partial(_mla_attn_kernel, tile=tile),
        grid=(batch, nt),
        in_specs=[pl.BlockSpec((None, seq, 512), lambda b, q: (b, 0, 0)),
                  pl.BlockSpec((None, nt, MLA_HEADS * VT_ROWS, tile), lambda b, q: (b, 0, 0, 0)),
                  pl.BlockSpec((None, 512, tile), lambda b, q: (b, 0, q))],
        out_specs=pl.BlockSpec((None, tile, 256), lambda b, q: (b, q, 0)),
        out_shape=jax.ShapeDtypeStruct((batch, seq, 256), BF16),
        scratch_shapes=[pltpu.VMEM((8, tile), F32),
                        pltpu.VMEM((MLA_HEADS, VT_ROWS, tile), F32),
                        *[pltpu.VMEM((MLA_HEADS, tile, tile), F32)] * 2,
                        *[pltpu.VMEM((8, tile), F32)] * 2],
        compiler_params=_cparams(("parallel", "parallel")),
        name="mla_attention",
    )(k, vt, qt)
    return y.reshape(batch * seq, 256)


def _mlp_kernel(h_ref, yc_ref, yn_ref, ym_ref, yp_ref, wo_ref, g_ref, w1_ref, w2_ref, fg_ref, o_ref, *, final, chunk):
    y = jnp.concatenate([yc_ref[...], yn_ref[...], ym_ref[...], yp_ref[...]], axis=1)
    h2 = h_ref[...] + _dot(y, wo_ref[...])
    xn = _rms(h2, g_ref[...]).astype(BF16)
    acc = h2
    for c in range(D_FF // chunk):
        z = _dot(xn, w1_ref[:, c * chunk:(c + 1) * chunk])
        a = jnp.square(jnp.maximum(z, 0.0)).astype(BF16)
        acc = acc + _dot(a, w2_ref[c * chunk:(c + 1) * chunk, :])
    if final:
        acc = _rms(acc, fg_ref[...])
    o_ref[...] = acc


def _out_mlp(h, yc, yn, ym, yp, wo, g, w1, w2, fg, *, final, tile=1024, chunk=1024):
    n = h.shape[0]
    row = lambda w: pl.BlockSpec((tile, w), lambda i: (i, 0))
    once = lambda shape: pl.BlockSpec(shape, lambda i: (0,) * len(shape), pipeline_mode=pl.Buffered(1))
    return pl.pallas_call(
        functools.partial(_mlp_kernel, final=final, chunk=chunk),
        grid=(n // tile,),
        in_specs=[row(D_MODEL), row(256), row(256), row(256), row(256),
                  once(wo.shape), _const_spec((1, D_MODEL)), once(w1.shape), once(w2.shape),
                  _const_spec((1, D_MODEL))],
        out_specs=row(D_MODEL),
        out_shape=jax.ShapeDtypeStruct((n, D_MODEL), F32),
        compiler_params=_cparams(("parallel",)),
        name="out_mlp",
    )(h, yc, yn, ym, yp, wo, g, w1, w2, fg)


def _bucket_np(n):
    n = np.maximum(n, 0)
    max_exact = N_BUCKETS // 2
    nf = np.maximum(n, 1).astype(np.float32)
    large = max_exact + (np.log(nf / np.float32(max_exact)) / np.float32(math.log(MAX_DIST / max_exact))
                         * np.float32(N_BUCKETS - max_exact)).astype(np.int32)
    large = np.minimum(large, N_BUCKETS - 1)
    return np.where(n < max_exact, n, large).astype(np.int32)


def _bias_tiles(rel_table):
    shifted = rel_table - rel_table[N_BUCKETS - 1][None, :]
    kl = np.arange(QB)[:, None]
    iq = np.arange(QB)[None, :]

    def tile(dist):
        onehot = (_bucket_np(dist)[..., None] == np.arange(N_BUCKETS)).astype(np.float32)
        t = jnp.einsum("kib,bh->khi", jnp.asarray(onehot), shifted, precision=lax.Precision.HIGHEST)
        return t.reshape(dist.shape[0], NSA_HEADS * QB)

    bd = tile(iq - kl)
    bd1 = tile(iq + QB - kl)
    cc = np.arange(32)[:, None]
    bc = tile(iq - CMP_STRIDE * (cc - 16) - (CMP_LEN - 1))
    return bd, bd1, bc


def _overlap_t(seq):
    nc = seq // CMP_STRIDE
    ns = seq // SLC_LEN
    c_lo = np.arange(nc)[None, :] * CMP_STRIDE
    s_lo = np.arange(ns)[:, None] * SLC_LEN
    ov = np.clip(np.minimum(c_lo + CMP_LEN, s_lo + SLC_LEN) - np.maximum(c_lo, s_lo), 0, None) / CMP_LEN
    ov[:, nc - 1] = 0.0
    return jnp.asarray(ov, BF16)


def _rope_tables(seq):
    pos = jnp.arange(seq, dtype=F32)
    inv_freq = ROPE_THETA ** (-jnp.arange(0, MLA_ROPE, 2, dtype=F32) / MLA_ROPE)
    ang = pos[:, None] * inv_freq[None, :]
    cos = jnp.concatenate([jnp.cos(ang)] * 2, axis=1)
    sin = jnp.concatenate([jnp.sin(ang)] * 2, axis=1)
    place = lambda a: jnp.pad(a, ((0, 0), (MLA_NOPE, 128 - MLA_NOPE - MLA_ROPE)))
    return place(cos), place(sin), cos.T, sin.T


def _rot_cols(w):
    half = MLA_ROPE // 2
    return jnp.concatenate([-w[..., half:], w[..., :half]], axis=-1)


def _layer_weights(w_in, conv_dw, cmp_pos, cmp_w1, cmp_w2, w_uq, w_ukv, pool_w):
    kv = lambda i: w_in[:, _C_KV + 64 * i:_C_KV + 64 * (i + 1)]
    kr = w_in[:, _C_KR:_C_KR + MLA_ROPE]
    place = lambda a: jnp.pad(a, ((0, 0), (MLA_NOPE, 128 - MLA_NOPE - MLA_ROPE)))
    wrow = jnp.concatenate([
        w_in[:, _C_CONV:_C_CONV + 512], w_in[:, _C_POOL:_C_POOL + 256], kv(0), kv(1),
        jnp.pad(kv(2), ((0, 0), (0, 64))), jnp.pad(kv(4), ((0, 0), (0, 64))),
        w_in[:, _C_CQ:_C_CQ + Q_LORA], w_in[:, _C_CKV:_C_CKV + KV_LORA], place(kr), place(_rot_cols(kr))],
        axis=1).astype(BF16)
    wt = jnp.concatenate([
        w_in[:, _C_Q:_C_Q + 256].T * (NSA_HD ** -0.5), kv(3).T, kv(5).T,
        jnp.pad(w_in[:, _C_G:_C_G + 12].T, ((0, 4), (0, 0)))], axis=0).astype(BF16)

    dw = jnp.pad(conv_dw, ((0, 1), (0, 0)))

    w1r = cmp_w1.reshape(2, CMP_LEN, NSA_HD, 128)
    zk = jnp.zeros_like(w1r[0])
    w1kv = jnp.concatenate([jnp.concatenate([w1r[0], zk], axis=-1),
                            jnp.concatenate([zk, w1r[1]], axis=-1)], axis=1)
    w1a = w1kv[:16].reshape(16 * 128, 256).astype(BF16)
    w1b = w1kv[16:].reshape(16 * 128, 256).astype(BF16)
    pos_kv = jnp.concatenate([cmp_pos[0], cmp_pos[1]], axis=-1)
    posa = pos_kv[:16].reshape(1, 16 * 128)
    posb = pos_kv[16:].reshape(1, 16 * 128)
    z2 = jnp.zeros_like(cmp_w2[0])
    w2bd = jnp.concatenate([jnp.concatenate([cmp_w2[0], z2], axis=1),
                            jnp.concatenate([z2, cmp_w2[1]], axis=1)], axis=0).astype(BF16)

    wq = w_uq.reshape(Q_LORA, MLA_HEADS, MLA_NOPE + MLA_ROPE)
    wqt = jnp.pad(wq, ((0, 0), (0, 0), (0, 128 - MLA_NOPE - MLA_ROPE))).reshape(Q_LORA, 512).T.astype(BF16)
    wqpt = _rot_cols(wq[:, :, MLA_NOPE:]).reshape(Q_LORA, MLA_HEADS * MLA_ROPE).T.astype(BF16)
    wkv = w_ukv.reshape(KV_LORA, MLA_HEADS, MLA_NOPE + MLA_V)
    wk = jnp.pad(wkv[:, :, :MLA_NOPE], ((0, 0), (0, 0), (0, 128 - MLA_NOPE))).reshape(KV_LORA, 512).astype(BF16)
    wvt = wkv[:, :, MLA_NOPE:].reshape(KV_LORA, MLA_HEADS * MLA_V).T.astype(BF16)

    poolw = jax.scipy.linalg.block_diag(*[pool_w[g] for g in range(len(POOL_WINDOWS))]).astype(BF16)
    return dict(wrow=wrow, wt=wt, dw=dw, w1a=w1a, w1b=w1b, posa=posa, posb=posb, w2bd=w2bd,
                wqt=wqt, wqpt=wqpt, wk=wk, wvt=wvt, poolw=poolw)


def kernel(x, w_in, w_out, ln_mix_g, ln_mlp_g, conv_dw, conv_dw_b, conv_ln_g, conv_ln_b, conv_pw,
           nsa_cmp_pos, nsa_cmp_w1, nsa_cmp_w2, mla_q_norm_g, mla_w_uq, mla_kv_norm_g, mla_w_ukv,
           pool_w, pool_scale, mlp_w1, mlp_w2, rel_bias_table, final_norm_g):
    batch, seq, _ = x.shape
    depth = w_in.shape[0]
    bd, bd1, bc = _bias_tiles(rel_bias_table)
    ovl_t = _overlap_t(seq)
    cosp, sinp, cost, sint = _rope_tables(seq)
    r1 = lambda v: v.reshape(1, -1)
    h = x.reshape(batch * seq, D_MODEL)
    for l in range(depth):
        w = _layer_weights(w_in[l], conv_dw[l], nsa_cmp_pos[l], nsa_cmp_w1[l], nsa_cmp_w2[l],
                           mla_w_uq[l], mla_w_ukv[l], pool_w[l])
        mla_args = (r1(mla_q_norm_g[l]), r1(mla_kv_norm_g[l]), w["wqt"], w["wqpt"], w["wk"], w["wvt"],
                    cosp, sinp, cost, sint)
        loc_args = (w["dw"], r1(conv_dw_b[l]), r1(conv_ln_g[l]), r1(conv_ln_b[l]), conv_pw[l].astype(BF16),
                    w["poolw"], r1(pool_scale[l]))
        yc, yp, kcvc, ks, kw, km, qtm, vtm, qt, vt, gt = _proj(
            h, r1(ln_mix_g[l]), w["wrow"], w["wt"], mla_args, loc_args, batch=batch, seq=seq)
        kvc, kvct = _compress(kcvc, w["posa"], w["posb"], w["w1a"], w["w1b"], w["w2bd"], batch=batch, seq=seq)
        yn = _nsa(kvc, kvct, ks, kw, vt, qt, gt, ovl_t, bd, bd1, bc, batch=batch, seq=seq)
        ym = _mla_attn(km.reshape(batch, seq, 512), qtm, vtm, batch=batch, seq=seq)
        h = _out_mlp(h, yc, yn, ym, yp, w_out[l].astype(BF16), r1(ln_mlp_g[l]), mlp_w1[l].astype(BF16),
                     mlp_w2[l].astype(BF16), r1(final_norm_g), final=(l == depth - 1))
    return h.reshape(batch, seq, D_MODEL)
```

```python
import functools
import math

import numpy as np
import jax
import jax.numpy as jnp
from jax import lax
from jax.experimental import pallas as pl
from jax.experimental.pallas import tpu as pltpu

F32 = jnp.float32
BF16 = jnp.bfloat16

D_MODEL = 1024
GROUP_W = 256
CONV_K = 31
NSA_HEADS = 4
NSA_HD = 64
CMP_LEN = 32
CMP_STRIDE = 16
SLC_LEN = 64
N_SELECT = 16
WINDOW = 512
MLA_HEADS = 4
MLA_NOPE = 64
MLA_ROPE = 32
MLA_V = 64
Q_LORA = 256
KV_LORA = 128
ROPE_THETA = 10000.0
POOL_WINDOWS = (2, 4, 8, 16)
D_FF = 4 * D_MODEL
N_BUCKETS = 32
MAX_DIST = 128
EPS = 1e-6
NEG_INF = -1e30
FORCE_SCORE = 1e9
LOG2E = math.log2(math.e)

LANES = 128
SUBLANES = 8
QB = 128
HALO = 32
FAR_TILE = 512
FAR_BLOCKS = FAR_TILE // SLC_LEN
MASK_ROWS = 16
VT_ROWS = 80
MLA_TILE = 512
VMEM_LIMIT = 56 * 1024 * 1024

_C_CONV, _C_Q, _C_KV, _C_G, _C_CQ, _C_CKV, _C_KR, _C_POOL = 0, 512, 768, 1152, 1164, 1420, 1548, 1580


def _cparams(sem):
    return pltpu.CompilerParams(dimension_semantics=sem, vmem_limit_bytes=VMEM_LIMIT)


def _rms(x, g):
    ms = jnp.mean(x * x, axis=-1, keepdims=True)
    return x * lax.rsqrt(ms + EPS) * g


def _dot(a, b):
    return jnp.dot(a, b, preferred_element_type=F32)


def _dot_nt(a, b):
    return lax.dot_general(a, b, (((1,), (1,)), ((), ())), preferred_element_type=F32)


def _const_spec(shape):
    nd = len(shape)
    return pl.BlockSpec(shape, lambda *_: (0,) * nd)


def _proj_kernel(x_ref, g_ref, wrow_ref, wt_ref, *rest, tile, sub, steps_per_seq):
    mla = rest[:10]
    loc = rest[10:17]
    yc_ref, yp_ref, kcvc_ref, ks_ref, kw_ref, kmla_ref, qtmla_ref, vtmla_ref, qt_ref, vt_ref, gt_ref = rest[17:28]
    scratch = rest[28:]
    step = pl.program_id(0)
    seq_step = step % steps_per_seq

    @pl.when(step == 0)
    def _():
        scratch[0][sub:sub + HALO, :] = jnp.zeros((HALO, GROUP_W), F32)
        scratch[1][sub:sub + HALO, :] = jnp.zeros((HALO, GROUP_W), F32)

    ones = jnp.ones((VT_ROWS - NSA_HD, LANES), BF16)
    lane = lax.broadcasted_iota(jnp.int32, (sub, LANES), 1)
    mixers = []
    for r in range(tile // sub):
        rows = slice(r * sub, (r + 1) * sub)
        xn = _rms(x_ref[rows, :], g_ref[...]).astype(BF16)
        uc = _dot(xn, wrow_ref[:, 0:512])
        up = _dot(xn, wrow_ref[:, 512:768])
        kk = _dot(xn, wrow_ref[:, 768:1024])
        kcvc_ref[rows, :] = kk[:, 0:LANES]
        tok = pl.program_id(0) * tile + r * sub + lax.broadcasted_iota(jnp.int32, (sub, LANES), 0)
        onehot = (lane - NSA_HD == (tok // SLC_LEN) % FAR_BLOCKS).astype(F32)
        ks_ref[rows, :] = (kk[:, LANES:2 * LANES] + onehot).astype(BF16)
        km = _dot(xn, wrow_ref[:, 1024:1792])
        kw_ref[rows, :] = km[:, 0:LANES].astype(BF16)
        qt_ref[:, rows] = _dot_nt(wt_ref[0:256, :], xn).astype(BF16)
        vs = _dot_nt(wt_ref[256:384, :], xn).astype(BF16)
        for j in range(sub // LANES):
            cols = slice(j * LANES, (j + 1) * LANES)
            jt = r * (sub // LANES) + j
            vt_ref[jt, 0:NSA_HD, :] = vs[0:NSA_HD, cols]
            vt_ref[jt, NSA_HD:VT_ROWS, :] = ones
            vt_ref[jt, VT_ROWS:VT_ROWS + NSA_HD, :] = vs[NSA_HD:2 * NSA_HD, cols]
            vt_ref[jt, VT_ROWS + NSA_HD:2 * VT_ROWS, :] = ones
        gt_ref[:, rows] = _dot_nt(wt_ref[384:400, :], xn)
        _mla_prep_tile(km[:, LANES:], mla, rows, kmla_ref, qtmla_ref, vtmla_ref.at[r])
        mixers.append((uc, up, r, rows))
    for uc, up, r, rows in mixers:
        first = (seq_step == 0) if r == 0 else False
        _local_tile(uc, up, first, seq_step * tile + r * sub, loc, rows, yc_ref, yp_ref, scratch)


def _proj(h, g, wrow, wt, mla_args, loc_args, *, batch, seq, tile=1024):
    qg, kg, wqt, wqpt, wk, wvt, cosp, sinp, cost, sint = mla_args
    dw, dwb, lng, lnb, pw, poolw, pscale = loc_args
    sub = MLA_TILE
    n = h.shape[0]
    spb = seq // tile
    mt = tile // MLA_TILE
    outs = (
        jax.ShapeDtypeStruct((n, GROUP_W), BF16),
        jax.ShapeDtypeStruct((n, GROUP_W), BF16),
        jax.ShapeDtypeStruct((n, 128), F32),
        jax.ShapeDtypeStruct((n, 128), BF16),
        jax.ShapeDtypeStruct((n, 128), BF16),
        jax.ShapeDtypeStruct((n, 512), BF16),
        jax.ShapeDtypeStruct((batch, 512, seq), BF16),
        jax.ShapeDtypeStruct((batch, seq // MLA_TILE, MLA_HEADS * VT_ROWS, MLA_TILE), BF16),
        jax.ShapeDtypeStruct((256, n), BF16),
        jax.ShapeDtypeStruct((n // LANES, 2 * VT_ROWS, LANES), BF16),
        jax.ShapeDtypeStruct((16, n), F32),
    )
    row = lambda w: pl.BlockSpec((tile, w), lambda i: (i, 0))
    return pl.pallas_call(
        functools.partial(_proj_kernel, tile=tile, sub=sub, steps_per_seq=spb),
        grid=(n // tile,),
        in_specs=[row(D_MODEL), _const_spec((1, D_MODEL)), _const_spec(wrow.shape), _const_spec(wt.shape),
                  _const_spec((1, Q_LORA)), _const_spec((1, KV_LORA)),
                  _const_spec(wqt.shape), _const_spec(wqpt.shape), _const_spec(wk.shape), _const_spec(wvt.shape),
                  pl.BlockSpec((tile, 128), lambda i: (i % spb, 0)), pl.BlockSpec((tile, 128), lambda i: (i % spb, 0)),
                  pl.BlockSpec((MLA_ROPE, tile), lambda i: (0, i % spb)),
                  pl.BlockSpec((MLA_ROPE, tile), lambda i: (0, i % spb)),
                  _const_spec(dw.shape), _const_spec((1, GROUP_W)), _const_spec((1, GROUP_W)),
                  _const_spec((1, GROUP_W)), _const_spec((GROUP_W, GROUP_W)),
                  _const_spec((GROUP_W, GROUP_W)), _const_spec((1, GROUP_W))],
        out_specs=(row(GROUP_W), row(GROUP_W), row(128), row(128), row(128), row(512),
                   pl.BlockSpec((None, 512, tile), lambda i: (i // spb, 0, i % spb)),
                   pl.BlockSpec((None, mt, MLA_HEADS * VT_ROWS, MLA_TILE), lambda i: (i // spb, i % spb, 0, 0)),
                   pl.BlockSpec((256, tile), lambda i: (0, i)),
                   pl.BlockSpec((tile // LANES, 2 * VT_ROWS, LANES), lambda i: (i, 0, 0)),
                   pl.BlockSpec((16, tile), lambda i: (0, i))),
        out_shape=outs,
        scratch_shapes=[pltpu.VMEM((HALO + sub, GROUP_W), F32)] * 4
        + [pltpu.VMEM((SUBLANES - 1, HALO + sub, GROUP_W), F32)],
        compiler_params=_cparams(("arbitrary",)),
        name="in_proj",
    )(h, g, wrow, wt, *mla_args, *loc_args)


def _local_tile(uc, up, first, pos0, loc, out_rows, yc_ref, yp_ref, scratch, *, rows=128):
    dw_ref, dwb_ref, lng_ref, lnb_ref, pw_ref, poolw_ref, pscale_ref = loc
    hbuf, ubuf, sa, sb, hshift = scratch
    tile = uc.shape[0]
    hbuf[0:HALO, :] = jnp.where(first, 0.0, hbuf[tile:tile + HALO, :])
    ubuf[0:HALO, :] = jnp.where(first, 0.0, ubuf[tile:tile + HALO, :])
    hbuf[HALO:HALO + tile, :] = uc[:, 0:GROUP_W] * jax.nn.sigmoid(uc[:, GROUP_W:2 * GROUP_W])
    span = HALO + tile - SUBLANES
    for j in range(1, SUBLANES):
        hshift[j - 1, 0:span, :] = hbuf[pl.ds(j, span), :]
    dwb = dwb_ref[...]
    for c in range(tile // rows):
        acc = jnp.zeros((rows, GROUP_W), F32) + dwb
        for k in range(CONV_K):
            off = c * rows + HALO - (CONV_K - 1) + k
            phase, base = off % SUBLANES, off - off % SUBLANES
            tap = hbuf[pl.ds(base, rows), :] if phase == 0 else hshift[phase - 1, pl.ds(base, rows), :]
            acc = acc + dw_ref[k:k + 1, :] * tap
        mu = jnp.mean(acc, axis=-1, keepdims=True)
        cen = acc - mu
        var = jnp.mean(cen * cen, axis=-1, keepdims=True)
        y = cen * lax.rsqrt(var + EPS) * lng_ref[...] + lnb_ref[...]
        y = y * jax.nn.sigmoid(y)
        r0 = out_rows.start + c * rows
        yc_ref[r0:r0 + rows, :] = _dot(y.astype(BF16), pw_ref[...]).astype(BF16)

    ext = HALO + tile
    ubuf[HALO:ext, :] = up
    sa[pl.ds(8, ext - 8), :] = ubuf[pl.ds(8, ext - 8), :] + ubuf[pl.ds(7, ext - 8), :]
    sb[pl.ds(16, ext - 16), :] = sa[pl.ds(16, ext - 16), :] + sa[pl.ds(14, ext - 16), :]
    s2 = sa[pl.ds(HALO, tile), :]
    s4 = sb[pl.ds(HALO, tile), :]
    sa[pl.ds(24, ext - 24), :] = sb[pl.ds(24, ext - 24), :] + sb[pl.ds(20, ext - 24), :]
    s8 = sa[pl.ds(HALO, tile), :]
    s16 = s8 + sa[pl.ds(HALO - 8, tile), :]
    lane = lax.broadcasted_iota(jnp.int32, (tile, GROUP_W), 1)
    grp = lane // (GROUP_W // len(POOL_WINDOWS))
    wsum = jnp.where(grp == 0, s2, jnp.where(grp == 1, s4, jnp.where(grp == 2, s8, s16)))
    width = jnp.where(grp == 0, 2, jnp.where(grp == 1, 4, jnp.where(grp == 2, 8, 16)))
    pos = pos0 + lax.broadcasted_iota(jnp.int32, (tile, GROUP_W), 0)
    cnt = jnp.minimum(width, pos + 1).astype(F32)
    d = wsum / cnt - up
    yp_ref[out_rows, :] = (_dot(d.astype(BF16), poolw_ref[...]) * pscale_ref[...]).astype(BF16)


def _compress_kernel(kr_ref, posa_ref, posb_ref, w1a_ref, w1b_ref, w2_ref, kv_ref, kvt_ref, bbuf, *, nc):
    kr = jnp.concatenate([kr_ref[pl.ds(l, nc, stride=CMP_STRIDE), :] for l in range(CMP_STRIDE)], axis=1)
    a = _dot((kr + posa_ref[...]).astype(BF16), w1a_ref[...])
    b = _dot((kr + posb_ref[...]).astype(BF16), w1b_ref[...])
    bbuf[0:nc, :] = b
    bbuf[nc:nc + 8, :] = jnp.zeros((8, 256), F32)
    hdn = a + bbuf[pl.ds(1, nc), :]
    hdn = hdn * jax.nn.sigmoid(hdn)
    kv = _dot(hdn.astype(BF16), w2_ref[...])
    kv_ref[...] = kv.astype(BF16)
    kvt_ref[...] = kv.T.astype(BF16)


def _compress(kcvc, posa, posb, w1a, w1b, w2bd, *, batch, seq):
    nc = seq // CMP_STRIDE
    kr = kcvc.reshape(batch, seq, 128)
    return pl.pallas_call(
        functools.partial(_compress_kernel, nc=nc),
        grid=(batch,),
        in_specs=[pl.BlockSpec((None, seq, 128), lambda b: (b, 0, 0)),
                  _const_spec(posa.shape), _const_spec(posb.shape), _const_spec(w1a.shape),
                  _const_spec(w1b.shape), _const_spec(w2bd.shape)],
        out_specs=(pl.BlockSpec((None, nc, 128), lambda b: (b, 0, 0)),
                   pl.BlockSpec((None, 128, nc), lambda b: (b, 0, 0))),
        out_shape=(jax.ShapeDtypeStruct((batch, nc, 128), BF16),
                   jax.ShapeDtypeStruct((batch, 128, nc), BF16)),
        scratch_shapes=[pltpu.VMEM((nc + 8, 256), F32)],
        compiler_params=_cparams(("parallel",)),
        name="nsa_compress",
    )(kr, posa, posb, w1a, w1b, w2bd)


def _nsa_kernel(kvc_ref, kvct_ref, ks_ref, kw_ref, vt_ref, qt_ref, gt_ref, ovl_ref, bd_ref, bd1_ref, bc_ref,
                y_ref, *scratch, nc, ns, n_top, per_step):
    for j in range(per_step):
        cols = pl.ds(j * QB, QB)
        _nsa_block(pl.program_id(1) * per_step + j, kvc_ref, kvct_ref, ks_ref, kw_ref, vt_ref, qt_ref.at[:, cols],
                   gt_ref.at[:, cols], ovl_ref, bd_ref, bd1_ref, bc_ref, y_ref.at[cols, :], *scratch,
                   nc=nc, ns=ns, n_top=n_top)


def _nsa_block(qb, kvc_ref, kvct_ref, ks_ref, kw_ref, vt_ref, qt_ref, gt_ref, ovl_ref, bd_ref, bd1_ref, bc_ref,
               y_ref, s_ref, key_ref, tie_ref, sel_ref, mb_ref, m_ref, acc_ref, mn_ref, accn_ref, oc_ref, ow_ref, sa_ref, sb_ref, cma_ref, cmb_ref,
               *, nc, ns, n_top):
    hq = NSA_HEADS * QB
    qt = jnp.concatenate([qt_ref[h * NSA_HD:(h + 1) * NSA_HD, :] for h in range(NSA_HEADS)], axis=1)
    qts = jnp.concatenate([qt, jnp.zeros_like(qt)], axis=0)
    lane_q = lax.broadcasted_iota(jnp.int32, (1, hq), 1) % QB

    tq = qb * QB + lane_q
    pad = 16
    sub8 = lax.broadcasted_iota(jnp.int32, (8, QB), 0)

    def select(nrows, nblk):
        s_ref[0:pad, :] = jnp.zeros((pad, hq), F32)
        s_ref[pad + nrows:pad + nrows + pad, :] = jnp.zeros((pad, hq), F32)
        s_ref[pad:pad + nrows, :] = _dot(kvc_ref[0:nrows, :], qts)
        w0 = pl.multiple_of(qb * 8, 8)
        s_ref[pl.ds(w0, 32), :] = s_ref[pl.ds(w0, 32), :] + bc_ref[...]
        sc = s_ref[pad:pad + nrows, :]
        cidx = lax.broadcasted_iota(jnp.int32, (nrows, hq), 0)
        last_c = jnp.minimum(lax.shift_right_arithmetic(tq - (CMP_LEN - 1), 4), nc - 2)
        sc = jnp.where(cidx <= last_c, sc, NEG_INF)
        mc = jnp.max(sc, axis=0, keepdims=True)
        pc = jnp.exp(sc - mc)
        lc = jnp.sum(pc, axis=0, keepdims=True)
        pc = pc * jnp.where(last_c >= 0, 1.0 / lc, 0.0)
        oc_ref[...] = _dot(kvct_ref[NSA_HD:2 * NSA_HD, 0:nrows], pc.astype(BF16))

        psum = pc[:, 0:QB]
        for h in range(1, NSA_HEADS):
            psum = psum + pc[:, h * QB:(h + 1) * QB]
        p_hi = psum.astype(BF16)
        p_lo = (psum - p_hi.astype(F32)).astype(BF16)
        ovl = ovl_ref[0:nblk, 0:nrows]
        imp = _dot(ovl, p_hi) + _dot(ovl, p_lo)
        jidx = lax.broadcasted_iota(jnp.int32, (nblk, QB), 0)
        iq = lax.broadcasted_iota(jnp.int32, (nblk, QB), 1)
        jc = 2 * qb + (iq >= SLC_LEN).astype(jnp.int32)
        forced = (jidx == 0) | (jidx == jc) | (jidx == jc - 1)
        imp = jnp.where(forced, FORCE_SCORE, imp)
        imp = jnp.where(jidx <= jc, imp, NEG_INF)
        key = pltpu.bitcast(imp, jnp.int32)
        key_ref[0:nblk, :] = key
        tie_ref[0:nblk, :] = jnp.zeros((nblk, QB), jnp.int32)
        n_vreg = nblk // 8

        def rank_body(g, cnts):
            base = pl.multiple_of(g * 8, 8)
            rows = key_ref[pl.ds(base, 8), :]
            out = []
            for tv in range(n_vreg):
                target = key[tv * 8:(tv + 1) * 8, :] - jnp.where(tv > g, 1, 0)
                c = cnts[tv]
                for r in range(8):
                    c = c + (rows[r:r + 1, :] > target).astype(jnp.int32)
                out.append(c)
            ties = jnp.zeros((8, QB), jnp.int32)
            for r in range(8):
                ties = ties + ((rows[r:r + 1, :] == rows) & (sub8 > r)).astype(jnp.int32)
            tie_ref[pl.ds(base, 8), :] = ties
            return tuple(out)

        n_groups = jnp.minimum((2 * qb + 9) // 8, n_vreg)
        cnts = lax.fori_loop(0, n_groups, rank_body, tuple(jnp.zeros((8, QB), jnp.int32) for _ in range(n_vreg)))
        cnt = jnp.concatenate(cnts, axis=0) + tie_ref[0:nblk, :]
        chosen = (cnt < n_top) & (imp > 0.5 * NEG_INF)
        sel_ref[0:nblk, :] = jnp.where(chosen, 1.0, 0.0)
        mb_ref[0:nblk, :] = jnp.where(chosen & (jidx < 2 * (qb - 1)), 0.0, NEG_INF)
        if nblk < ns:
            sel_ref[nblk:ns, :] = jnp.zeros((ns - nblk, QB), F32)
            mb_ref[nblk:ns, :] = jnp.full((ns - nblk, QB), NEG_INF, F32)

    nqb = nc // (QB // CMP_STRIDE)
    parts = next(p for p in (4, 2, 1) if nc % (p * LANES) == 0 and ns % (p * SUBLANES) == 0)
    for k in range(parts):
        in_part = (qb >= k * nqb // parts) & (qb < (k + 1) * nqb // parts)
        pl.when(in_part)(functools.partial(select, nc * (k + 1) // parts, ns * (k + 1) // parts))

    m_ref[...] = jnp.full((1, hq), NEG_INF, F32)
    acc_ref[...] = jnp.zeros((VT_ROWS, hq), F32)
    per_far = FAR_TILE // QB
    n_far = (jnp.maximum(qb - 1, 0) + per_far - 1) // per_far
    zpad = jnp.zeros((QB - NSA_HD - MASK_ROWS, hq), BF16)

    def q_aug(t):
        rows = mb_ref[pl.ds(pl.multiple_of(t * FAR_BLOCKS, FAR_BLOCKS), FAR_BLOCKS), :]
        rows = jnp.concatenate([rows] * NSA_HEADS, axis=1)
        rows = jnp.concatenate([rows, jnp.zeros((MASK_ROWS - FAR_BLOCKS, hq), F32)], axis=0).astype(BF16)
        return jnp.concatenate([qt, rows, zpad], axis=0)

    def produce(t, s_x, cm_x):
        s = _dot(ks_ref[pl.ds(pl.multiple_of(t * FAR_TILE, FAR_TILE), FAR_TILE), :], q_aug(t))
        s_x[...] = s
        cm_x[...] = jnp.max(s, axis=0, keepdims=True)

    def soften(s_x, cm_x):
        m_old = m_ref[...]
        m_new = jnp.maximum(m_old, cm_x[...])
        m_ref[...] = m_new
        return jnp.exp(s_x[...] - m_new).astype(BF16), jnp.exp(m_old - m_new)

    def accumulate(t, p, alpha):
        sub = FAR_TILE // QB
        vt = jnp.concatenate([vt_ref[sub * t + j, 0:VT_ROWS, :] for j in range(sub)], axis=1)
        acc_ref[...] = alpha * acc_ref[...] + _dot(vt, p)

    def consume(t, s_x, cm_x):
        accumulate(t, *soften(s_x, cm_x))

    produce(0, sa_ref, cma_ref)

    kl = lax.broadcasted_iota(jnp.int32, (QB, hq), 0)
    causal = kl <= lane_q
    anti = kl > lane_q
    n_w = WINDOW // QB
    kts = [jnp.maximum(qb - back, 0) for back in range(n_w + 1)]
    live = [qb >= back for back in range(n_w + 1)]

    def krows(ref, kt):
        return ref[pl.ds(pl.multiple_of(kt * QB, QB), QB), :]

    def sel_mask(kt):
        m0 = jnp.broadcast_to(sel_ref[pl.ds(2 * kt, 1), :], (SLC_LEN, QB))
        m1 = jnp.broadcast_to(sel_ref[pl.ds(2 * kt + 1, 1), :], (SLC_LEN, QB))
        mt = jnp.concatenate([m0, m1], axis=0)
        return jnp.concatenate([mt] * NSA_HEADS, axis=1) > 0.5

    def softmax_pv(tiles, row0):
        m = jnp.max(tiles[0], axis=0, keepdims=True)
        for s in tiles[1:]:
            m = jnp.maximum(m, jnp.max(s, axis=0, keepdims=True))
        acc = jnp.zeros((VT_ROWS, hq), F32)
        for back, s in enumerate(tiles):
            acc = acc + _dot(vt_ref[kts[back], row0:row0 + VT_ROWS, :], jnp.exp(s - m).astype(BF16))
        return m, acc

    s_win = [_dot(krows(kw_ref, kts[back]), qts) for back in range(n_w + 1)]
    s_sel = [_dot(krows(ks_ref, kts[back]), qts) for back in range(2)]
    s_win[0] = jnp.where(causal, s_win[0] + bd_ref[...], NEG_INF)
    s_win[1] = jnp.where(live[1], s_win[1] + bd1_ref[...], NEG_INF)
    for back in range(2, n_w):
        s_win[back] = jnp.where(live[back], s_win[back], NEG_INF)
    s_win[n_w] = jnp.where(anti & live[n_w], s_win[n_w], NEG_INF)
    s_sel[0] = jnp.where(sel_mask(qb) & causal, s_sel[0] + bd_ref[...], NEG_INF)
    s_sel[1] = jnp.where(sel_mask(kts[1]) & live[1], s_sel[1] + bd1_ref[...], NEG_INF)
    _, acc_w = softmax_pv(s_win, VT_ROWS)
    ow_ref[...] = acc_w[0:NSA_HD, :] * (1.0 / acc_w[NSA_HD:NSA_HD + 1, :])
    mn_ref[...], accn_ref[...] = softmax_pv(s_sel, 0)

    def pair_body(j, carry):
        produce(2 * j + 1, sb_ref, cmb_ref)
        consume(2 * j, sa_ref, cma_ref)
        produce(jnp.minimum(2 * j + 2, n_far - 1), sa_ref, cma_ref)
        consume(2 * j + 1, sb_ref, cmb_ref)
        return carry

    lax.fori_loop(0, n_far // 2, pair_body, 0)

    @pl.when(n_far % 2 == 1)
    def _():
        consume(n_far - 1, sa_ref, cma_ref)

    m_far = m_ref[...]
    m_near = mn_ref[...]
    m_all = jnp.maximum(m_far, m_near)
    acc = jnp.exp(m_far - m_all) * acc_ref[...] + jnp.exp(m_near - m_all) * accn_ref[...]
    osel = acc[0:NSA_HD, :] * (1.0 / acc[NSA_HD:NSA_HD + 1, :])

    gates = jax.nn.sigmoid(gt_ref[...])
    oc = oc_ref[...]
    ow = ow_ref[...]
    parts = []
    for h in range(NSA_HEADS):
        sl = slice(h * QB, (h + 1) * QB)
        parts.append(gates[3 * h:3 * h + 1, :] * oc[:, sl] + gates[3 * h + 1:3 * h + 2, :] * osel[:, sl]
                     + gates[3 * h + 2:3 * h + 3, :] * ow[:, sl])
    y_ref[...] = jnp.concatenate(parts, axis=0).T.astype(BF16)


def _nsa(kvc, kvct, ks, kw, vt, qt, gt, ovl_t, bd, bd1, bc, *, batch, seq):
    nc = seq // CMP_STRIDE
    ns = seq // SLC_LEN
    nqb = seq // QB
    hq = NSA_HEADS * QB
    ks3 = ks.reshape(batch, seq, 128)
    kw3 = kw.reshape(batch, seq, 128)
    vt4 = vt.reshape(batch, nqb, 2 * VT_ROWS, LANES)
    per_step = next(p for p in (4, 2, 1) if nqb % p == 0)
    steps = nqb // per_step
    return pl.pallas_call(
        functools.partial(_nsa_kernel, nc=nc, ns=ns, n_top=min(N_SELECT, ns), per_step=per_step),
        grid=(batch, steps),
        in_specs=[pl.BlockSpec((None, nc, 128), lambda b, q: (b, 0, 0)),
                  pl.BlockSpec((None, 128, nc), lambda b, q: (b, 0, 0)),
                  pl.BlockSpec((None, seq, 128), lambda b, q: (b, 0, 0)),
                  pl.BlockSpec((None, seq, 128), lambda b, q: (b, 0, 0)),
                  pl.BlockSpec((None, nqb, 2 * VT_ROWS, LANES), lambda b, q: (b, 0, 0, 0)),
                  pl.BlockSpec((256, per_step * QB), lambda b, q: (0, b * steps + q)),
                  pl.BlockSpec((16, per_step * QB), lambda b, q: (0, b * steps + q)),
                  _const_spec(ovl_t.shape), _const_spec(bd.shape), _const_spec(bd1.shape),
                  _const_spec(bc.shape)],
        out_specs=pl.BlockSpec((per_step * QB, 256), lambda b, q: (b * steps + q, 0)),
        out_shape=jax.ShapeDtypeStruct((batch * seq, 256), BF16),
        scratch_shapes=[pltpu.VMEM((nc + 32, hq), F32),
                        pltpu.VMEM((ns, QB), jnp.int32),
                        pltpu.VMEM((ns, QB), jnp.int32),
                        pltpu.VMEM((ns, QB), F32),
                        pltpu.VMEM((ns, QB), F32),
                        pltpu.VMEM((1, hq), F32),
                        pltpu.VMEM((VT_ROWS, hq), F32),
                        pltpu.VMEM((1, hq), F32),
                        pltpu.VMEM((VT_ROWS, hq), F32),
                        pltpu.VMEM((NSA_HD, hq), F32),
                        pltpu.VMEM((NSA_HD, hq), F32),
                        *[pltpu.VMEM((FAR_TILE, hq), F32)] * 2,
                        *[pltpu.VMEM((1, hq), F32)] * 2],
        compiler_params=_cparams(("parallel", "parallel")),
        name="nsa_attention",
    )(kvc, kvct, ks3, kw3, vt4, qt, gt, ovl_t, bd, bd1, bc)


def _mla_prep_tile(u, mla, rows, k_ref, qt_ref, vt_ref):
    qg_ref, kg_ref, wqt_ref, wqpt_ref, wk_ref, wvt_ref, cosp_ref, sinp_ref, cost_ref, sint_ref = mla
    scale = (MLA_NOPE + MLA_ROPE) ** -0.5 * LOG2E
    cqn = _rms(u[:, 0:Q_LORA], qg_ref[...]).astype(BF16)
    ckvn = _rms(u[:, Q_LORA:Q_LORA + KV_LORA], kg_ref[...]).astype(BF16)
    qta = _dot_nt(wqt_ref[...], cqn)
    qtp = _dot_nt(wqpt_ref[...], cqn)
    cos_t = cost_ref[:, rows]
    sin_t = sint_ref[:, rows]
    n_tok = u.shape[0]
    for h in range(MLA_HEADS):
        r0 = h * 128
        qt_ref[r0:r0 + MLA_NOPE, rows] = (qta[r0:r0 + MLA_NOPE, :] * scale).astype(BF16)
        rope = qta[r0 + MLA_NOPE:r0 + MLA_NOPE + MLA_ROPE, :] * cos_t + qtp[h * MLA_ROPE:(h + 1) * MLA_ROPE, :] * sin_t
        qt_ref[r0 + MLA_NOPE:r0 + MLA_NOPE + MLA_ROPE, rows] = (rope * scale).astype(BF16)
        qt_ref[r0 + MLA_NOPE + MLA_ROPE:r0 + 128, rows] = jnp.zeros((128 - MLA_NOPE - MLA_ROPE, n_tok), BF16)
    krope = u[:, 384:512] * cosp_ref[rows, :] + u[:, 512:640] * sinp_ref[rows, :]
    knope = _dot(ckvn, wk_ref[...])
    for h in range(MLA_HEADS):
        k_ref[rows, h * 128:(h + 1) * 128] = (knope[:, h * 128:(h + 1) * 128] + krope).astype(BF16)
    vt = _dot_nt(wvt_ref[...], ckvn).astype(BF16)
    for h in range(MLA_HEADS):
        vt_ref[h * VT_ROWS:h * VT_ROWS + MLA_V, :] = vt[h * MLA_V:(h + 1) * MLA_V, :]
        vt_ref[h * VT_ROWS + MLA_V:(h + 1) * VT_ROWS, :] = jnp.ones((VT_ROWS - MLA_V, n_tok), BF16)


def _mla_attn_kernel(k_ref, vt_ref, qt_ref, o_ref, m_ref, acc_ref, sa_ref, sb_ref, cma_ref, cmb_ref, *, tile):
    qi = pl.program_id(1)
    heads = range(MLA_HEADS)
    m_ref[...] = jnp.full(m_ref.shape, NEG_INF, F32)
    acc_ref[...] = jnp.zeros(acc_ref.shape, F32)

    def scores(t, h):
        hs = slice(h * 128, (h + 1) * 128)
        return _dot(k_ref[pl.ds(pl.multiple_of(t * tile, tile), tile), hs], qt_ref[hs, :])

    def produce(t, s_x, cm_x):
        for h in heads:
            s = scores(t, h)
            s_x[h] = s
            cm_x[h:h + 1, :] = jnp.max(s, axis=0, keepdims=True)

    def consume(t, tiles, maxima):
        probs, alphas = [], []
        for h in heads:
            m_old = m_ref[h:h + 1, :]
            m_new = jnp.maximum(m_old, maxima[h])
            probs.append(jnp.exp2(tiles[h] - m_new).astype(BF16))
            alphas.append(jnp.exp2(m_old - m_new))
            m_ref[h:h + 1, :] = m_new
        for h in heads:
            acc_ref[h] = alphas[h] * acc_ref[h] + _dot(vt_ref[t, h * VT_ROWS:(h + 1) * VT_ROWS, :], probs[h])

    def consume_from(t, s_x, cm_x):
        consume(t, [s_x[h] for h in heads], [cm_x[h:h + 1, :] for h in heads])

    produce(0, sa_ref, cma_ref)
    kl = lax.broadcasted_iota(jnp.int32, (tile, tile), 0)
    iq = lax.broadcasted_iota(jnp.int32, (tile, tile), 1)
    diag = [jnp.where(kl <= iq, scores(qi, h), NEG_INF) for h in heads]
    consume(qi, diag, [jnp.max(s, axis=0, keepdims=True) for s in diag])

    def pair_body(j, carry):
        produce(2 * j + 1, sb_ref, cmb_ref)
        consume_from(2 * j, sa_ref, cma_ref)
        produce(jnp.minimum(2 * j + 2, qi - 1), sa_ref, cma_ref)
        consume_from(2 * j + 1, sb_ref, cmb_ref)
        return carry

    lax.fori_loop(0, qi // 2, pair_body, 0)

    @pl.when(qi % 2 == 1)
    def _():
        consume_from(qi - 1, sa_ref, cma_ref)

    outs = [acc_ref[h, 0:MLA_V, :] * (1.0 / acc_ref[h, MLA_V:MLA_V + 1, :]) for h in heads]
    o_ref[...] = jnp.concatenate(outs, axis=0).T.astype(BF16)


def _mla_attn(k, qt, vt, *, batch, seq, tile=MLA_TILE):
    nt = seq // tile
    y = pl.pallas_call(
        functools.partial(_mla_attn_kernel, tile=tile),
        grid=(batch, nt),
        in_specs=[pl.BlockSpec((None, seq, 512), lambda b, q: (b, 0, 0)),
                  pl.BlockSpec((None, nt, MLA_HEADS * VT_ROWS, tile), lambda b, q: (b, 0, 0, 0)),
                  pl.BlockSpec((None, 512, tile), lambda b, q: (b, 0, q))],
        out_specs=pl.BlockSpec((None, tile, 256), lambda b, q: (b, q, 0)),
        out_shape=jax.ShapeDtypeStruct((batch, seq, 256), BF16),
        scratch_shapes=[pltpu.VMEM((8, tile), F32),
                        pltpu.VMEM((MLA_HEADS, VT_ROWS, tile), F32),
                        *[pltpu.VMEM((MLA_HEADS, tile, tile), F32)] * 2,
                        *[pltpu.VMEM((8, tile), F32)] * 2],
        compiler_params=_cparams(("parallel", "parallel")),
        name="mla_attention",
    )(k, vt, qt)
    return y.reshape(batch * seq, 256)


def _mlp_kernel(h_ref, yc_ref, yn_ref, ym_ref, yp_ref, wo_ref, g_ref, w1_ref, w2_ref, fg_ref, o_ref, *, final, chunk):
    y = jnp.concatenate([yc_ref[...], yn_ref[...], ym_ref[...], yp_ref[...]], axis=1)
    h2 = h_ref[...] + _dot(y, wo_ref[...])
    xn = _rms(h2, g_ref[...]).astype(BF16)
    acc = h2
    for c in range(D_FF // chunk):
        z = _dot(xn, w1_ref[:, c * chunk:(c + 1) * chunk])
        a = jnp.square(jnp.maximum(z, 0.0)).astype(BF16)
        acc = acc + _dot(a, w2_ref[c * chunk:(c + 1) * chunk, :])
    if final:
        acc = _rms(acc, fg_ref[...])
    o_ref[...] = acc


def _out_mlp(h, yc, yn, ym, yp, wo, g, w1, w2, fg, *, final, tile=1024, chunk=1024):
    n = h.shape[0]
    row = lambda w: pl.BlockSpec((tile, w), lambda i: (i, 0))
    once = lambda shape: pl.BlockSpec(shape, lambda i: (0,) * len(shape), pipeline_mode=pl.Buffered(1))
    return pl.pallas_call(
        functools.partial(_mlp_kernel, final=final, chunk=chunk),
        grid=(n // tile,),
        in_specs=[row(D_MODEL), row(256), row(256), row(256), row(256),
                  once(wo.shape), _const_spec((1, D_MODEL)), once(w1.shape), once(w2.shape),
                  _const_spec((1, D_MODEL))],
        out_specs=row(D_MODEL),
        out_shape=jax.ShapeDtypeStruct((n, D_MODEL), F32),
        compiler_params=_cparams(("parallel",)),
        name="out_mlp",
    )(h, yc, yn, ym, yp, wo, g, w1, w2, fg)


def _bucket_np(n):
    n = np.maximum(n, 0)
    max_exact = N_BUCKETS // 2
    nf = np.maximum(n, 1).astype(np.float32)
    large = max_exact + (np.log(nf / np.float32(max_exact)) / np.float32(math.log(MAX_DIST / max_exact))
                         * np.float32(N_BUCKETS - max_exact)).astype(np.int32)
    large = np.minimum(large, N_BUCKETS - 1)
    return np.where(n < max_exact, n, large).astype(np.int32)


def _bias_tiles(rel_table):
    shifted = rel_table - rel_table[N_BUCKETS - 1][None, :]
    kl = np.arange(QB)[:, None]
    iq = np.arange(QB)[None, :]

    def tile(dist):
        onehot = (_bucket_np(dist)[..., None] == np.arange(N_BUCKETS)).astype(np.float32)
        t = jnp.einsum("kib,bh->khi", jnp.asarray(onehot), shifted, precision=lax.Precision.HIGHEST)
        return t.reshape(dist.shape[0], NSA_HEADS * QB)

    bd = tile(iq - kl)
    bd1 = tile(iq + QB - kl)
    cc = np.arange(32)[:, None]
    bc = tile(iq - CMP_STRIDE * (cc - 16) - (CMP_LEN - 1))
    return bd, bd1, bc


def _overlap_t(seq):
    nc = seq // CMP_STRIDE
    ns = seq // SLC_LEN
    c_lo = np.arange(nc)[None, :] * CMP_STRIDE
    s_lo = np.arange(ns)[:, None] * SLC_LEN
    ov = np.clip(np.minimum(c_lo + CMP_LEN, s_lo + SLC_LEN) - np.maximum(c_lo, s_lo), 0, None) / CMP_LEN
    ov[:, nc - 1] = 0.0
    return jnp.asarray(ov, BF16)


def _rope_tables(seq):
    pos = jnp.arange(seq, dtype=F32)
    inv_freq = ROPE_THETA ** (-jnp.arange(0, MLA_ROPE, 2, dtype=F32) / MLA_ROPE)
    ang = pos[:, None] * inv_freq[None, :]
    cos = jnp.concatenate([jnp.cos(ang)] * 2, axis=1)
    sin = jnp.concatenate([jnp.sin(ang)] * 2, axis=1)
    place = lambda a: jnp.pad(a, ((0, 0), (MLA_NOPE, 128 - MLA_NOPE - MLA_ROPE)))
    return place(cos), place(sin), cos.T, sin.T


def _rot_cols(w):
    half = MLA_ROPE // 2
    return jnp.concatenate([-w[..., half:], w[..., :half]], axis=-1)


def _layer_weights(w_in, conv_dw, cmp_pos, cmp_w1, cmp_w2, w_uq, w_ukv, pool_w):
    kv = lambda i: w_in[:, _C_KV + 64 * i:_C_KV + 64 * (i + 1)]
    kr = w_in[:, _C_KR:_C_KR + MLA_ROPE]
    place = lambda a: jnp.pad(a, ((0, 0), (MLA_NOPE, 128 - MLA_NOPE - MLA_ROPE)))
    wrow = jnp.concatenate([
        w_in[:, _C_CONV:_C_CONV + 512], w_in[:, _C_POOL:_C_POOL + 256], kv(0), kv(1),
        jnp.pad(kv(2), ((0, 0), (0, 64))), jnp.pad(kv(4), ((0, 0), (0, 64))),
        w_in[:, _C_CQ:_C_CQ + Q_LORA], w_in[:, _C_CKV:_C_CKV + KV_LORA], place(kr), place(_rot_cols(kr))],
        axis=1).astype(BF16)
    wt = jnp.concatenate([
        w_in[:, _C_Q:_C_Q + 256].T * (NSA_HD ** -0.5), kv(3).T, kv(5).T,
        jnp.pad(w_in[:, _C_G:_C_G + 12].T, ((0, 4), (0, 0)))], axis=0).astype(BF16)

    dw = jnp.pad(conv_dw, ((0, 1), (0, 0)))

    w1r = cmp_w1.reshape(2, CMP_LEN, NSA_HD, 128)
    zk = jnp.zeros_like(w1r[0])
    w1kv = jnp.concatenate([jnp.concatenate([w1r[0], zk], axis=-1),
                            jnp.concatenate([zk, w1r[1]], axis=-1)], axis=1)
    w1a = w1kv[:16].reshape(16 * 128, 256).astype(BF16)
    w1b = w1kv[16:].reshape(16 * 128, 256).astype(BF16)
    pos_kv = jnp.concatenate([cmp_pos[0], cmp_pos[1]], axis=-1)
    posa = pos_kv[:16].reshape(1, 16 * 128)
    posb = pos_kv[16:].reshape(1, 16 * 128)
    z2 = jnp.zeros_like(cmp_w2[0])
    w2bd = jnp.concatenate([jnp.concatenate([cmp_w2[0], z2], axis=1),
                            jnp.concatenate([z2, cmp_w2[1]], axis=1)], axis=0).astype(BF16)

    wq = w_uq.reshape(Q_LORA, MLA_HEADS, MLA_NOPE + MLA_ROPE)
    wqt = jnp.pad(wq, ((0, 0), (0, 0), (0, 128 - MLA_NOPE - MLA_ROPE))).reshape(Q_LORA, 512).T.astype(BF16)
    wqpt = _rot_cols(wq[:, :, MLA_NOPE:]).reshape(Q_LORA, MLA_HEADS * MLA_ROPE).T.astype(BF16)
    wkv = w_ukv.reshape(KV_LORA, MLA_HEADS, MLA_NOPE + MLA_V)
    wk = jnp.pad(wkv[:, :, :MLA_NOPE], ((0, 0), (0, 0), (0, 128 - MLA_NOPE))).reshape(KV_LORA, 512).astype(BF16)
    wvt = wkv[:, :, MLA_NOPE:].reshape(KV_LORA, MLA_HEADS * MLA_V).T.astype(BF16)

    poolw = jax.scipy.linalg.block_diag(*[pool_w[g] for g in range(len(POOL_WINDOWS))]).astype(BF16)
    return dict(wrow=wrow, wt=wt, dw=dw, w1a=w1a, w1b=w1b, posa=posa, posb=posb, w2bd=w2bd,
                wqt=wqt, wqpt=wqpt, wk=wk, wvt=wvt, poolw=poolw)


def kernel(x, w_in, w_out, ln_mix_g, ln_mlp_g, conv_dw, conv_dw_b, conv_ln_g, conv_ln_b, conv_pw,
           nsa_cmp_pos, nsa_cmp_w1, nsa_cmp_w2, mla_q_norm_g, mla_w_uq, mla_kv_norm_g, mla_w_ukv,
           pool_w, pool_scale, mlp_w1, mlp_w2, rel_bias_table, final_norm_g):
    batch, seq, _ = x.shape
    depth = w_in.shape[0]
    bd, bd1, bc = _bias_tiles(rel_bias_table)
    ovl_t = _overlap_t(seq)
    cosp, sinp, cost, sint = _rope_tables(seq)
    r1 = lambda v: v.reshape(1, -1)
    h = x.reshape(batch * seq, D_MODEL)
    for l in range(depth):
        w = _layer_weights(w_in[l], conv_dw[l], nsa_cmp_pos[l], nsa_cmp_w1[l], nsa_cmp_w2[l],
                           mla_w_uq[l], mla_w_ukv[l], pool_w[l])
        mla_args = (r1(mla_q_norm_g[l]), r1(mla_kv_norm_g[l]), w["wqt"], w["wqpt"], w["wk"], w["wvt"],
                    cosp, sinp, cost, sint)
        loc_args = (w["dw"], r1(conv_dw_b[l]), r1(conv_ln_g[l]), r1(conv_ln_b[l]), conv_pw[l].astype(BF16),
                    w["poolw"], r1(pool_scale[l]))
        yc, yp, kcvc, ks, kw, km, qtm, vtm, qt, vt, gt = _proj(
            h, r1(ln_mix_g[l]), w["wrow"], w["wt"], mla_args, loc_args, batch=batch, seq=seq)
        kvc, kvct = _compress(kcvc, w["posa"], w["posb"], w["w1a"], w["w1b"], w["w2bd"], batch=batch, seq=seq)
        yn = _nsa(kvc, kvct, ks, kw, vt, qt, gt, ovl_t, bd, bd1, bc, batch=batch, seq=seq)
        ym = _mla_attn(km.reshape(batch, seq, 512), qtm, vtm, batch=batch, seq=seq)
        h = _out_mlp(h, yc, yn, ym, yp, w_out[l].astype(BF16), r1(ln_mlp_g[l]), mlp_w1[l].astype(BF16),
                     mlp_w2[l].astype(BF16), r1(final_norm_g), final=(l == depth - 1))
    return h.reshape(batch, seq, D_MODEL)
```
